```python
import math
import jax, jax.numpy as jnp
from jax import lax
import numpy as np

D_MODEL = 1024
BATCH = 8
SEQ = 4096
DEPTH = 1

N_META = 16
MIX_WIDTH = D_MODEL
S5_GROUP_CH = 16
S5_STATE = 64
S5_WIDTH = MIX_WIDTH // 4
S5_GROUPS = S5_WIDTH // S5_GROUP_CH
RET_HEAD_DIM = 128
RET_WIDTH = MIX_WIDTH - S5_WIDTH
RET_HEADS = RET_WIDTH // RET_HEAD_DIM
CHUNK = 128
ROPE_BASE = 10000.0
D_FF = 4 * D_MODEL
LN_EPS = 1e-5
GN_EPS = 1e-5
IN_PROJ_WIDTH = S5_WIDTH + 4 * RET_WIDTH
DEEPNORM_ALPHA = (2.0 * DEPTH) ** 0.25
DEEPNORM_BETA = (8.0 * DEPTH) ** -0.25

kernel_name = "hymba_s5_retnet_deepnorm_layer"


def layer_norm(x, g, b):
    xf = x.astype(jnp.float32)
    mu = jnp.mean(xf, axis=-1, keepdims=True)
    xc = xf - mu
    var = jnp.mean(xc * xc, axis=-1, keepdims=True)
    y = xc * lax.rsqrt(var + LN_EPS)
    return (y * g.astype(jnp.float32) + b.astype(jnp.float32)).astype(x.dtype)


def _complex_linear_combine(e1, e2):
    a1r, a1i, b1r, b1i = e1
    a2r, a2i, b2r, b2i = e2
    ar = a2r * a1r - a2i * a1i
    ai = a2r * a1i + a2i * a1r
    br = a2r * b1r - a2i * b1i + b2r
    bi = a2r * b1i + a2i * b1r + b2i
    return (ar, ai, br, bi)


def s5_mixer(u, lam_re, lam_im, log_dt, b_re, b_im, c_re, c_im, d, w_glu, b_glu):
    bsz, L, _ = u.shape
    ug = u.reshape(bsz, L, S5_GROUPS, S5_GROUP_CH)
    dt = jnp.exp(log_dt)[:, None]
    mag = jnp.exp(lam_re * dt)
    lbr = mag * jnp.cos(lam_im * dt)
    lbi = mag * jnp.sin(lam_im * dt)
    den = lam_re * lam_re + lam_im * lam_im
    nr = lbr - 1.0
    qr = (nr * lam_re + lbi * lam_im) / den
    qi = (lbi * lam_re - nr * lam_im) / den
    bbr = qr[..., None] * b_re - qi[..., None] * b_im
    bbi = qr[..., None] * b_im + qi[..., None] * b_re
    bur = jnp.einsum('blgh,gph->blgp', ug, bbr)
    bui = jnp.einsum('blgh,gph->blgp', ug, bbi)
    ar = jnp.broadcast_to(lbr[None, None], (1, L, S5_GROUPS, S5_STATE))
    ai = jnp.broadcast_to(lbi[None, None], (1, L, S5_GROUPS, S5_STATE))
    _, _, xr, xi = lax.associative_scan(_complex_linear_combine, (ar, ai, bur, bui), axis=1)
    y = jnp.einsum('blgp,ghp->blgh', xr, c_re) - jnp.einsum('blgp,ghp->blgh', xi, c_im)
    y = y.reshape(bsz, L, S5_WIDTH) + d * u
    y = jax.nn.gelu(y)
    return y * jax.nn.sigmoid(y @ w_glu + b_glu)


def _rotate(x, cos, sin):
    half = x.shape[-1] // 2
    x1, x2 = x[..., :half], x[..., half:]
    return jnp.concatenate([x1 * cos - x2 * sin, x1 * sin + x2 * cos], axis=-1)


def retention_mixer(q, k, v, g, gn_g, gn_b):
    bsz, L, _ = q.shape
    dtype = q.dtype
    q = q.reshape(bsz, L, RET_HEADS, RET_HEAD_DIM)
    k = k.reshape(bsz, L, RET_HEADS, RET_HEAD_DIM)
    v = v.reshape(bsz, L, RET_HEADS, RET_HEAD_DIM)
    pos = jnp.arange(L, dtype=jnp.float32)
    inv_freq = 1.0 / (ROPE_BASE ** (jnp.arange(0, RET_HEAD_DIM, 2, dtype=jnp.float32) / RET_HEAD_DIM))
    ang = pos[:, None] * inv_freq[None, :]
    cos = jnp.cos(ang)[None, :, None, :].astype(dtype)
    sin = jnp.sin(ang)[None, :, None, :].astype(dtype)
    q = _rotate(q, cos, sin)
    k = _rotate(k, cos, sin) * (RET_HEAD_DIM ** -0.5)

    pad = CHUNK - N_META
    n_chunks = (L + pad) // CHUNK

    def to_chunks(t):
        t = jnp.pad(t, ((0, 0), (pad, 0), (0, 0), (0, 0)))
        return t.reshape(bsz, n_chunks, CHUNK, RET_HEADS, RET_HEAD_DIM).transpose(1, 0, 3, 2, 4)

    qc, kc, vc = to_chunks(q), to_chunks(k), to_chunks(v)

    log_gamma = jnp.log1p(-jnp.exp2(-5.0 - jnp.arange(RET_HEADS, dtype=jnp.float32)))
    idx = jnp.arange(CHUNK, dtype=jnp.float32)
    diff = idx[:, None] - idx[None, :]
    dmat = jnp.where(diff[None] >= 0,
                     jnp.exp(jnp.maximum(diff, 0.0)[None] * log_gamma[:, None, None]),
                     0.0).astype(dtype)
    zeta = jnp.exp((CHUNK - 1.0 - idx)[None] * log_gamma[:, None]).astype(dtype)
    xi = jnp.exp((idx + 1.0)[None] * log_gamma[:, None]).astype(dtype)
    gamma_chunk = jnp.exp(CHUNK * log_gamma).astype(dtype)

    def step(state, inp):
        qb, kb, vb = inp
        scores = jnp.einsum('bhid,bhjd->bhij', qb, kb) * dmat[None]
        inner = jnp.einsum('bhij,bhje->bhie', scores, vb)
        cross = jnp.einsum('bhid,bhde->bhie', qb, state) * xi[None, :, :, None]
        new_state = (gamma_chunk[None, :, None, None] * state
                     + jnp.einsum('bhjd,bhje->bhde', kb * zeta[None, :, :, None], vb))
        return new_state, inner + cross

    state0 = jnp.zeros((bsz, RET_HEADS, RET_HEAD_DIM, RET_HEAD_DIM), dtype=dtype)
    _, out = lax.scan(step, state0, (qc, kc, vc))
    out = out.transpose(1, 0, 3, 2, 4).reshape(bsz, n_chunks * CHUNK, RET_HEADS, RET_HEAD_DIM)[:, pad:]

    of = out.astype(jnp.float32)
    mu = jnp.mean(of, axis=-1, keepdims=True)
    oc = of - mu
    var = jnp.mean(oc * oc, axis=-1, keepdims=True)
    on = (oc * lax.rsqrt(var + GN_EPS)).reshape(bsz, L, RET_WIDTH)
    on = (on * gn_g.astype(jnp.float32) + gn_b.astype(jnp.float32)).astype(dtype)
    return jax.nn.silu(g) * on


def setup_inputs(seed: int = 0) -> dict:
    key = jax.random.key(seed)
    ks = jax.random.split(key, 24)
    f32 = jnp.float32
    nrm = lambda k, s, sc: jax.random.normal(k, s, f32) * sc
    P, G, H = S5_STATE, S5_GROUPS, S5_GROUP_CH
    x = jax.random.normal(ks[0], (BATCH, SEQ, D_MODEL), f32)
    meta_tokens = nrm(ks[1], (N_META, D_MODEL), 1.0)
    ln_in_g = 1.0 + nrm(ks[2], (D_MODEL,), 0.02)
    ln_in_b = nrm(ks[3], (D_MODEL,), 0.02)
    w_in = nrm(ks[4], (DEPTH, D_MODEL, IN_PROJ_WIDTH), D_MODEL ** -0.5)
    s5_lambda_re = -0.5 + nrm(ks[5], (DEPTH, G, P), 0.01)
    s5_lambda_im = math.pi * jnp.broadcast_to(jnp.arange(P, dtype=f32), (DEPTH, G, P)) + nrm(ks[6], (DEPTH, G, P), 0.01)
    s5_log_dt = jax.random.uniform(ks[7], (DEPTH, G), f32, math.log(1e-3), math.log(1e-1))
    s5_b_re = nrm(ks[8], (DEPTH, G, P, H), (2.0 * H) ** -0.5)
    s5_b_im = nrm(ks[9], (DEPTH, G, P, H), (2.0 * H) ** -0.5)
    s5_c_re = nrm(ks[10], (DEPTH, G, H, P), (2.0 * P) ** -0.5)
    s5_c_im = nrm(ks[11], (DEPTH, G, H, P), (2.0 * P) ** -0.5)
    s5_d = nrm(ks[12], (DEPTH, S5_WIDTH), 1.0)
    s5_w_glu = nrm(ks[13], (DEPTH, S5_WIDTH, S5_WIDTH), S5_WIDTH ** -0.5)
    s5_b_glu = nrm(ks[14], (DEPTH, S5_WIDTH), 0.01)
    ret_gn_g = 1.0 + nrm(ks[15], (DEPTH, RET_WIDTH), 0.02)
    ret_gn_b = nrm(ks[16], (DEPTH, RET_WIDTH), 0.02)
    w_out = nrm(ks[17], (DEPTH, MIX_WIDTH, D_MODEL), MIX_WIDTH ** -0.5 * DEEPNORM_BETA)
    ln1_g = 1.0 + nrm(ks[18], (DEPTH, D_MODEL), 0.02)
    ln1_b = nrm(ks[19], (DEPTH, D_MODEL), 0.02)
    w_up = nrm(ks[20], (DEPTH, D_MODEL, D_FF), D_MODEL ** -0.5)
    w_down = nrm(ks[21], (DEPTH, D_FF, D_MODEL), D_FF ** -0.5 * DEEPNORM_BETA)
    ln2_g = 1.0 + nrm(ks[22], (DEPTH, D_MODEL), 0.02)
    ln2_b = nrm(ks[23], (DEPTH, D_MODEL), 0.02)
    return {"x": x, "meta_tokens": meta_tokens, "ln_in_g": ln_in_g, "ln_in_b": ln_in_b,
            "w_in": w_in, "s5_lambda_re": s5_lambda_re, "s5_lambda_im": s5_lambda_im,
            "s5_log_dt": s5_log_dt, "s5_b_re": s5_b_re, "s5_b_im": s5_b_im,
            "s5_c_re": s5_c_re, "s5_c_im": s5_c_im, "s5_d": s5_d, "s5_w_glu": s5_w_glu,
            "s5_b_glu": s5_b_glu, "ret_gn_g": ret_gn_g, "ret_gn_b": ret_gn_b, "w_out": w_out,
            "ln1_g": ln1_g, "ln1_b": ln1_b, "w_up": w_up, "w_down": w_down,
            "ln2_g": ln2_g, "ln2_b": ln2_b}


def reference(x, meta_tokens, ln_in_g, ln_in_b, w_in, s5_lambda_re, s5_lambda_im, s5_log_dt,
              s5_b_re, s5_b_im, s5_c_re, s5_c_im, s5_d, s5_w_glu, s5_b_glu, ret_gn_g, ret_gn_b,
              w_out, ln1_g, ln1_b, w_up, w_down, ln2_g, ln2_b):
    bsz = x.shape[0]
    meta = jnp.broadcast_to(meta_tokens[None].astype(x.dtype), (bsz, N_META, D_MODEL))
    h = jnp.concatenate([meta, x], axis=1)
    h = layer_norm(h, ln_in_g, ln_in_b)
    splits = [S5_WIDTH, S5_WIDTH + RET_WIDTH, S5_WIDTH + 2 * RET_WIDTH, S5_WIDTH + 3 * RET_WIDTH]
    for l in range(DEPTH):
        proj = h @ w_in[l]
        u, q, k, v, g = jnp.split(proj, splits, axis=-1)
        y_s5 = s5_mixer(u, s5_lambda_re[l], s5_lambda_im[l], s5_log_dt[l], s5_b_re[l], s5_b_im[l],
                        s5_c_re[l], s5_c_im[l], s5_d[l], s5_w_glu[l], s5_b_glu[l])
        y_ret = retention_mixer(q, k, v, g, ret_gn_g[l], ret_gn_b[l])
        mixed = jnp.concatenate([y_s5, y_ret], axis=-1) @ w_out[l]
        h = layer_norm(DEEPNORM_ALPHA * h + mixed, ln1_g[l], ln1_b[l])
        ff = jnp.square(jax.nn.relu(h @ w_up[l])) @ w_down[l]
        h = layer_norm(DEEPNORM_ALPHA * h + ff, ln2_g[l], ln2_b[l])
    return h[:, N_META:]
```

```python
import functools
import math

import jax
import jax.numpy as jnp
from jax import lax
from jax.experimental import pallas as pl
from jax.experimental.pallas import tpu as pltpu

D_MODEL = 1024
BATCH = 8
SEQ = 4096
N_META = 16
S5_GROUP_CH = 16
S5_STATE = 64
S5_WIDTH = 256
S5_GROUPS = S5_WIDTH // S5_GROUP_CH
S5_NSTATE = S5_GROUPS * S5_STATE
RET_HEAD_DIM = 128
RET_WIDTH = 768
RET_HEADS = RET_WIDTH // RET_HEAD_DIM
CHUNK = 128
ROPE_BASE = 10000.0
D_FF = 4 * D_MODEL
LANES = 128
S5_SLABS = S5_WIDTH // LANES
LN_EPS = 1e-5
GN_EPS = 1e-5
IN_PROJ_WIDTH = S5_WIDTH + 4 * RET_WIDTH
DEEPNORM_ALPHA = 2.0 ** 0.25

_OFF_Q = S5_WIDTH
_OFF_K = _OFF_Q + RET_WIDTH
_OFF_V = _OFF_K + RET_WIDTH
_OFF_G = _OFF_V + RET_WIDTH

TM_PROJ = 512
TT_S5 = 128
TR_RET = 512
TM_FFN = 512
FF_CHUNK = 1024
VMEM_LIMIT = 56 * 1024 * 1024

_BF = jnp.bfloat16
_F32 = jnp.float32


def _const_spec(shape):
    nd = len(shape)
    return pl.BlockSpec(shape, lambda *_: (0,) * nd, pipeline_mode=pl.Buffered(1))


def _layer_norm(x, g, b):
    mu = jnp.mean(x, axis=-1, keepdims=True)
    xc = x - mu
    var = jnp.mean(xc * xc, axis=-1, keepdims=True)
    return xc * lax.rsqrt(var + LN_EPS) * g + b


def _dot(a, b):
    return jnp.dot(a, b, preferred_element_type=_F32)


def _rope_head(t, cos2, sin2):
    return t * cos2 + pltpu.roll(t, RET_HEAD_DIM // 2, 1) * sin2


def _meta_kernel(meta_ref, g_ref, b_ref, w_ref, bblk_ref, ar_ref, ai_ref, cos_ref, sin_ref,
                 zmeta_ref, s5_ref, s0_ref):
    hm = _layer_norm(meta_ref[...], g_ref[...], b_ref[...]).astype(_BF)
    u = _dot(hm, w_ref[:, 0:S5_WIDTH])
    bu = _dot(u.astype(_BF), bblk_ref[...])
    ar = ar_ref[...]
    ai = ai_ref[...]
    xr = jnp.zeros((1, S5_NSTATE), _F32)
    xi = jnp.zeros((1, S5_NSTATE), _F32)
    for t in range(N_META):
        br = bu[t:t + 1, 0:S5_NSTATE]
        bi = bu[t:t + 1, S5_NSTATE:2 * S5_NSTATE]
        xr, xi = ar * xr - ai * xi + br, ar * xi + ai * xr + bi
    s5_ref[:, 0:S5_NSTATE] = xr
    s5_ref[:, S5_NSTATE:2 * S5_NSTATE] = xi

    k = _dot(hm, w_ref[:, _OFF_K:_OFF_V])
    v = _dot(hm, w_ref[:, _OFF_V:_OFF_G]).astype(_BF)
    cos2 = cos_ref[...]
    sin2 = sin_ref[...]
    for h in range(RET_HEADS):
        sl = slice(h * RET_HEAD_DIM, (h + 1) * RET_HEAD_DIM)
        kh = _rope_head(k[:, sl], cos2, sin2) * (RET_HEAD_DIM ** -0.5)
        kz = (kh * zmeta_ref[h]).astype(_BF)
        s0_ref[h] = lax.dot_general(kz, v[:, sl], (((0,), (0,)), ((), ())),
                                    preferred_element_type=_F32)


def _in_proj_kernel(x_ref, g_ref, b_ref, w_ref, cos_ref, sin_ref,
                    u_ref, q_ref, k_ref, v_ref, sg_ref):
    bidx = pl.program_id(1)
    hn = _layer_norm(x_ref[...], g_ref[...], b_ref[...]).astype(_BF)
    u = _dot(hn, w_ref[:, 0:S5_WIDTH])
    for j in range(S5_SLABS):
        u_ref[j, pl.ds(bidx, TM_PROJ, stride=BATCH), :] = u[:, j * LANES:(j + 1) * LANES]
    cos2 = cos_ref[...]
    sin2 = sin_ref[...]
    q = _dot(hn, w_ref[:, _OFF_Q:_OFF_K])
    for h in range(RET_HEADS):
        sl = slice(h * RET_HEAD_DIM, (h + 1) * RET_HEAD_DIM)
        q_ref[:, sl] = _rope_head(q[:, sl], cos2, sin2).astype(_BF)
    k = _dot(hn, w_ref[:, _OFF_K:_OFF_V])
    for h in range(RET_HEADS):
        sl = slice(h * RET_HEAD_DIM, (h + 1) * RET_HEAD_DIM)
        k_ref[:, sl] = (_rope_head(k[:, sl], cos2, sin2) * (RET_HEAD_DIM ** -0.5)).astype(_BF)
    v_ref[...] = _dot(hn, w_ref[:, _OFF_V:_OFF_G]).astype(_BF)
    g = _dot(hn, w_ref[:, _OFF_G:IN_PROJ_WIDTH])
    sg_ref[...] = (g * jax.nn.sigmoid(g)).astype(_BF)


def _s5_kernel(u_ref, init_ref, bblk_ref, cblk_ref, ar_ref, ai_ref, d_ref, wglu_ref, bglu_ref,
               y_ref, x_scr, st_scr):
    @pl.when(pl.program_id(0) == 0)
    def _():
        st_scr[...] = jnp.broadcast_to(init_ref[...], (BATCH, 2 * S5_NSTATE))

    u = jnp.concatenate([u_ref[j] for j in range(S5_SLABS)], axis=1)
    x_scr[...] = _dot(u.astype(_BF), bblk_ref[...])
    ar = jnp.broadcast_to(ar_ref[...], (BATCH, S5_NSTATE))
    ai = jnp.broadcast_to(ai_ref[...], (BATCH, S5_NSTATE))

    def step(t, carry):
        xr, xi = carry
        row = pl.multiple_of(t * BATCH, BATCH)
        br = x_scr[pl.ds(row, BATCH), 0:S5_NSTATE]
        bi = x_scr[pl.ds(row, BATCH), S5_NSTATE:2 * S5_NSTATE]
        nr = ar * xr - ai * xi + br
        ni = ar * xi + ai * xr + bi
        x_scr[pl.ds(row, BATCH), 0:S5_NSTATE] = nr
        x_scr[pl.ds(row, BATCH), S5_NSTATE:2 * S5_NSTATE] = ni
        return nr, ni

    xr, xi = lax.fori_loop(0, TT_S5, step,
                           (st_scr[:, 0:S5_NSTATE], st_scr[:, S5_NSTATE:2 * S5_NSTATE]),
                           unroll=4)
    st_scr[:, 0:S5_NSTATE] = xr
    st_scr[:, S5_NSTATE:2 * S5_NSTATE] = xi

    y = _dot(x_scr[...].astype(_BF), cblk_ref[...]) + d_ref[...] * u
    y = jax.nn.gelu(y)
    gate = jax.nn.sigmoid(_dot(y.astype(_BF), wglu_ref[...]) + bglu_ref[...])
    y = y * gate
    for j in range(S5_SLABS):
        y_ref[j] = y[:, j * LANES:(j + 1) * LANES]


def _ret_kernel(q_ref, k_ref, v_ref, sg_ref, s0_ref, dmat_ref, xi_ref, zeta_ref, gc_ref,
                gng_ref, gnb_ref, y_ref, s_scr):
    @pl.when(pl.program_id(1) == 0)
    def _():
        s_scr[...] = s0_ref[...]

    for c in range(TR_RET // CHUNK):
        rows = slice(c * CHUNK, (c + 1) * CHUNK)
        for h in range(RET_HEADS):
            cols = slice(h * RET_HEAD_DIM, (h + 1) * RET_HEAD_DIM)
            qh = q_ref[rows, cols]
            kh = k_ref[rows, cols]
            vh = v_ref[rows, cols]
            state = s_scr[h]
            scores = lax.dot_general(qh, kh, (((1,), (1,)), ((), ())),
                                     preferred_element_type=_F32) * dmat_ref[h]
            inner = _dot(scores.astype(_BF), vh)
            cross = _dot(qh, state.astype(_BF)) * xi_ref[h]
            kz = (kh.astype(_F32) * zeta_ref[h]).astype(_BF)
            s_scr[h] = gc_ref[h] * state + lax.dot_general(
                kz, vh, (((0,), (0,)), ((), ())), preferred_element_type=_F32)
            o = inner + cross
            mu = jnp.mean(o, axis=-1, keepdims=True)
            oc = o - mu
            var = jnp.mean(oc * oc, axis=-1, keepdims=True)
            on = oc * lax.rsqrt(var + GN_EPS) * gng_ref[:, cols] + gnb_ref[:, cols]
            y_ref[rows, cols] = (sg_ref[rows, cols].astype(_F32) * on).astype(_BF)


def _ffn_kernel(x_ref, ys5_ref, yret_ref, lig_ref, lib_ref, wo1_ref, wo2_ref, l1g_ref, l1b_ref,
                wup_ref, wdn_ref, l2g_ref, l2b_ref, o_ref):
    bidx = pl.program_id(1)
    h = _layer_norm(x_ref[...], lig_ref[...], lib_ref[...])
    ys5 = jnp.concatenate([ys5_ref[j, pl.ds(bidx, TM_FFN, stride=BATCH), :]
                           for j in range(S5_SLABS)], axis=1).astype(_BF)
    mixed = _dot(ys5, wo1_ref[...]) + _dot(yret_ref[...], wo2_ref[...])
    h1 = _layer_norm(DEEPNORM_ALPHA * h + mixed, l1g_ref[...], l1b_ref[...])
    h1b = h1.astype(_BF)
    ff = jnp.zeros((TM_FFN, D_MODEL), _F32)
    for c in range(D_FF // FF_CHUNK):
        cs = slice(c * FF_CHUNK, (c + 1) * FF_CHUNK)
        up = jnp.maximum(_dot(h1b, wup_ref[:, cs]), 0.0)
        ff = ff + _dot((up * up).astype(_BF), wdn_ref[cs, :])
    o_ref[...] = _layer_norm(DEEPNORM_ALPHA * h1 + ff, l2g_ref[...], l2b_ref[...])


def _row(v):
    return v.reshape(1, -1).astype(_F32)


def kernel(x, meta_tokens, ln_in_g, ln_in_b, w_in, s5_lambda_re, s5_lambda_im, s5_log_dt, s5_b_re, s5_b_im, s5_c_re, s5_c_im, s5_d, s5_w_glu, s5_b_glu, ret_gn_g, ret_gn_b, w_out, ln1_g, ln1_b, w_up, w_down, ln2_g, ln2_b):
    assert x.shape == (BATCH, SEQ, D_MODEL) and w_in.shape[0] == 1
    G, P, H = S5_GROUPS, S5_STATE, S5_GROUP_CH
    nrows = BATCH * SEQ
    arb2 = pltpu.CompilerParams(dimension_semantics=("arbitrary", "arbitrary"),
                                vmem_limit_bytes=VMEM_LIMIT)
    arb1 = pltpu.CompilerParams(dimension_semantics=("arbitrary",), vmem_limit_bytes=VMEM_LIMIT)

    lam_re, lam_im = s5_lambda_re[0], s5_lambda_im[0]
    dt = jnp.exp(s5_log_dt[0])[:, None]
    mag = jnp.exp(lam_re * dt)
    lbr = mag * jnp.cos(lam_im * dt)
    lbi = mag * jnp.sin(lam_im * dt)
    den = lam_re * lam_re + lam_im * lam_im
    nr = lbr - 1.0
    qr = (nr * lam_re + lbi * lam_im) / den
    qi = (lbi * lam_re - nr * lam_im) / den
    bbr = qr[..., None] * s5_b_re[0] - qi[..., None] * s5_b_im[0]
    bbi = qr[..., None] * s5_b_im[0] + qi[..., None] * s5_b_re[0]
    eye = jnp.eye(G, dtype=_F32)

    def blk_in(m):
        return (eye[:, None, :, None] * m.transpose(0, 2, 1)[:, :, None, :]).reshape(G * H, G * P)

    def blk_out(m):
        return (eye[:, None, :, None] * m.transpose(0, 2, 1)[:, :, None, :]).reshape(G * P, G * H)

    bblk = jnp.concatenate([blk_in(bbr), blk_in(bbi)], axis=1).astype(_BF)
    cblk = jnp.concatenate([blk_out(s5_c_re[0]), -blk_out(s5_c_im[0])], axis=0).astype(_BF)
    ar = lbr.reshape(1, S5_NSTATE)
    ai = lbi.reshape(1, S5_NSTATE)

    pos = jnp.arange(N_META + SEQ, dtype=_F32)
    inv_freq = 1.0 / (ROPE_BASE ** (jnp.arange(0, RET_HEAD_DIM, 2, dtype=_F32) / RET_HEAD_DIM))
    ang = pos[:, None] * inv_freq[None, :]
    cos, sin = jnp.cos(ang), jnp.sin(ang)
    cos2 = jnp.concatenate([cos, cos], axis=1)
    sin2 = jnp.concatenate([-sin, sin], axis=1)

    log_gamma = jnp.log1p(-jnp.exp2(-5.0 - jnp.arange(RET_HEADS, dtype=_F32)))
    idx = jnp.arange(CHUNK, dtype=_F32)
    diff = idx[:, None] - idx[None, :]
    dmat = jnp.where(diff[None] >= 0,
                     jnp.exp(jnp.maximum(diff, 0.0)[None] * log_gamma[:, None, None]), 0.0)
    zeta = jnp.exp((CHUNK - 1.0 - idx)[None] * log_gamma[:, None])
    xi = jnp.exp((idx + 1.0)[None] * log_gamma[:, None])
    gamma_chunk = jnp.exp(CHUNK * log_gamma)
    hd = RET_HEAD_DIM
    zeta_b = jnp.broadcast_to(zeta[:, :, None], (RET_HEADS, CHUNK, hd))
    xi_b = jnp.broadcast_to(xi[:, :, None], (RET_HEADS, CHUNK, hd))
    gc_b = jnp.broadcast_to(gamma_chunk[:, None, None], (RET_HEADS, hd, hd))
    zmeta_b = zeta_b[:, CHUNK - N_META:, :]

    w_in_b = w_in[0].astype(_BF)
    w_out_b = w_out[0].astype(_BF)
    w_up_b = w_up[0].astype(_BF)
    w_dn_b = w_down[0].astype(_BF)
    lig, lib = _row(ln_in_g), _row(ln_in_b)

    s5_init, s0 = pl.pallas_call(
        _meta_kernel,
        out_shape=(jax.ShapeDtypeStruct((1, 2 * S5_NSTATE), _F32),
                   jax.ShapeDtypeStruct((RET_HEADS, hd, hd), _F32)),
        compiler_params=pltpu.CompilerParams(vmem_limit_bytes=VMEM_LIMIT),
        name="meta_prologue",
    )(meta_tokens.astype(_F32), lig, lib, w_in_b, bblk, ar, ai,
      cos2[:N_META], sin2[:N_META], zmeta_b)

    x2 = x.reshape(nrows, D_MODEL)
    n_t = SEQ // TM_PROJ
    rows_spec = lambda w: pl.BlockSpec((TM_PROJ, w), lambda i, b: (b * n_t + i, 0))
    u_tm, q, k, v, sg = pl.pallas_call(
        _in_proj_kernel,
        grid=(n_t, BATCH),
        in_specs=[rows_spec(D_MODEL), _const_spec((1, D_MODEL)), _const_spec((1, D_MODEL)),
                  _const_spec((D_MODEL, IN_PROJ_WIDTH)),
                  pl.BlockSpec((TM_PROJ, hd), lambda i, b: (i, 0)),
                  pl.BlockSpec((TM_PROJ, hd), lambda i, b: (i, 0))],
        out_specs=[pl.BlockSpec((S5_SLABS, TM_PROJ * BATCH, LANES), lambda i, b: (0, i, 0)),
                   rows_spec(RET_WIDTH), rows_spec(RET_WIDTH), rows_spec(RET_WIDTH),
                   rows_spec(RET_WIDTH)],
        out_shape=(jax.ShapeDtypeStruct((S5_SLABS, SEQ * BATCH, LANES), _F32),
                   jax.ShapeDtypeStruct((nrows, RET_WIDTH), _BF),
                   jax.ShapeDtypeStruct((nrows, RET_WIDTH), _BF),
                   jax.ShapeDtypeStruct((nrows, RET_WIDTH), _BF),
                   jax.ShapeDtypeStruct((nrows, RET_WIDTH), _BF)),
        compiler_params=arb2,
        name="in_proj",
    )(x2, lig, lib, w_in_b, cos2[N_META:], sin2[N_META:])

    rows_s5 = TT_S5 * BATCH
    ys5_tm = pl.pallas_call(
        _s5_kernel,
        grid=(SEQ // TT_S5,),
        in_specs=[pl.BlockSpec((S5_SLABS, rows_s5, LANES), lambda i: (0, i, 0)),
                  _const_spec((1, 2 * S5_NSTATE)),
                  _const_spec((S5_WIDTH, 2 * S5_NSTATE)), _const_spec((2 * S5_NSTATE, S5_WIDTH)),
                  _const_spec((1, S5_NSTATE)), _const_spec((1, S5_NSTATE)),
                  _const_spec((1, S5_WIDTH)), _const_spec((S5_WIDTH, S5_WIDTH)),
                  _const_spec((1, S5_WIDTH))],
        out_specs=pl.BlockSpec((S5_SLABS, rows_s5, LANES), lambda i: (0, i, 0)),
        out_shape=jax.ShapeDtypeStruct((S5_SLABS, SEQ * BATCH, LANES), _F32),
        scratch_shapes=[pltpu.VMEM((rows_s5, 2 * S5_NSTATE), _F32),
                        pltpu.VMEM((BATCH, 2 * S5_NSTATE), _F32)],
        compiler_params=arb1,
        name="s5_scan",
    )(u_tm, s5_init, bblk, cblk, ar, ai, _row(s5_d[0]), s5_w_glu[0].astype(_BF), _row(s5_b_glu[0]))

    n_r = SEQ // TR_RET
    ret_spec = pl.BlockSpec((TR_RET, RET_WIDTH), lambda b, c: (b * n_r + c, 0))
    tab = lambda n: _const_spec((RET_HEADS, n, hd))
    y_ret = pl.pallas_call(
        _ret_kernel,
        grid=(BATCH, n_r),
        in_specs=[ret_spec, ret_spec, ret_spec, ret_spec, tab(hd), tab(CHUNK), tab(CHUNK),
                  tab(CHUNK), tab(hd), _const_spec((1, RET_WIDTH)), _const_spec((1, RET_WIDTH))],
        out_specs=ret_spec,
        out_shape=jax.ShapeDtypeStruct((nrows, RET_WIDTH), _BF),
        scratch_shapes=[pltpu.VMEM((RET_HEADS, hd, hd), _F32)],
        compiler_params=arb2,
        name="retention",
    )(q, k, v, sg, s0, dmat, xi_b, zeta_b, gc_b, _row(ret_gn_g[0]), _row(ret_gn_b[0]))

    n_f = SEQ // TM_FFN
    frow = lambda w: pl.BlockSpec((TM_FFN, w), lambda i, b: (b * n_f + i, 0))
    out = pl.pallas_call(
        _ffn_kernel,
        grid=(n_f, BATCH),
        in_specs=[frow(D_MODEL),
                  pl.BlockSpec((S5_SLABS, TM_FFN * BATCH, LANES), lambda i, b: (0, i, 0)),
                  frow(RET_WIDTH),
                  _const_spec((1, D_MODEL)), _const_spec((1, D_MODEL)),
                  _const_spec((S5_WIDTH, D_MODEL)), _const_spec((RET_WIDTH, D_MODEL)),
                  _const_spec((1, D_MODEL)), _const_spec((1, D_MODEL)),
                  _const_spec((D_MODEL, D_FF)), _const_spec((D_FF, D_MODEL)),
                  _const_spec((1, D_MODEL)), _const_spec((1, D_MODEL))],
        out_specs=frow(D_MODEL),
        out_shape=jax.ShapeDtypeStruct((nrows, D_MODEL), _F32),
        compiler_params=arb2,
        name="out_ffn",
    )(x2, ys5_tm, y_ret, lig, lib, w_out_b[:S5_WIDTH], w_out_b[S5_WIDTH:],
      _row(ln1_g[0]), _row(ln1_b[0]), w_up_b, w_dn_b, _row(ln2_g[0]), _row(ln2_b[0]))

    return out.reshape(BATCH, SEQ, D_MODEL)
```

```python
import functools
import math

import jax
import jax.numpy as jnp
from jax import lax
from jax.experimental import pallas as pl
from jax.experimental.pallas import tpu as pltpu

D_MODEL = 1024
BATCH = 8
SEQ = 4096
N_META = 16
S5_GROUP_CH = 16
S5_STATE = 64
S5_WIDTH = 256
S5_GROUPS = S5_WIDTH // S5_GROUP_CH
S5_NSTATE = S5_GROUPS * S5_STATE
RET_HEAD_DIM = 128
RET_WIDTH = 768
RET_HEADS = RET_WIDTH // RET_HEAD_DIM
CHUNK = 128
ROPE_BASE = 10000.0
D_FF = 4 * D_MODEL
LANES = 128
S5_SLABS = S5_WIDTH // LANES
LN_EPS = 1e-5
GN_EPS = 1e-5
IN_PROJ_WIDTH = S5_WIDTH + 4 * RET_WIDTH
DEEPNORM_ALPHA = 2.0 ** 0.25

_OFF_Q = S5_WIDTH
_OFF_K = _OFF_Q + RET_WIDTH
_OFF_V = _OFF_K + RET_WIDTH
_OFF_G = _OFF_V + RET_WIDTH

TM_PROJ = 512
SUB_PROJ = 256
TT_S5 = 128
TR_RET = 512
TM_FFN = 512
SUB_FFN = 256
FF_CHUNK = 1024
VMEM_LIMIT = 56 * 1024 * 1024

_BF = jnp.bfloat16
_F32 = jnp.float32


def _const_spec(shape):
    nd = len(shape)
    return pl.BlockSpec(shape, lambda *_: (0,) * nd, pipeline_mode=pl.Buffered(1))


def _layer_norm(x, g, b):
    mu = jnp.mean(x, axis=-1, keepdims=True)
    xc = x - mu
    var = jnp.mean(xc * xc, axis=-1, keepdims=True)
    return xc * lax.rsqrt(var + LN_EPS) * g + b


def _dot(a, b):
    return jnp.dot(a, b, preferred_element_type=_F32)


def _rope_head(t, cos2, sin2):
    return t * cos2 + pltpu.roll(t, RET_HEAD_DIM // 2, 1) * sin2


def _meta_kernel(meta_ref, g_ref, b_ref, w_ref, bblk_ref, ar_ref, ai_ref, cos_ref, sin_ref,
                 zmeta_ref, s5_ref, s0_ref):
    hm = _layer_norm(meta_ref[...], g_ref[...], b_ref[...]).astype(_BF)
    u = _dot(hm, w_ref[:, 0:S5_WIDTH])
    bu = _dot(u.astype(_BF), bblk_ref[...])
    ar = ar_ref[...]
    ai = ai_ref[...]
    xr = jnp.zeros((1, S5_NSTATE), _F32)
    xi = jnp.zeros((1, S5_NSTATE), _F32)
    for t in range(N_META):
        br = bu[t:t + 1, 0:S5_NSTATE]
        bi = bu[t:t + 1, S5_NSTATE:2 * S5_NSTATE]
        xr, xi = ar * xr - ai * xi + br, ar * xi + ai * xr + bi
    s5_ref[:, 0:S5_NSTATE] = xr
    s5_ref[:, S5_NSTATE:2 * S5_NSTATE] = xi

    k = _dot(hm, w_ref[:, _OFF_K:_OFF_V])
    v = _dot(hm, w_ref[:, _OFF_V:_OFF_G]).astype(_BF)
    cos2 = cos_ref[...]
    sin2 = sin_ref[...]
    for h in range(RET_HEADS):
        sl = slice(h * RET_HEAD_DIM, (h + 1) * RET_HEAD_DIM)
        kh = _rope_head(k[:, sl], cos2, sin2) * (RET_HEAD_DIM ** -0.5)
        kz = (kh * zmeta_ref[h]).astype(_BF)
        s0_ref[h] = lax.dot_general(kz, v[:, sl], (((0,), (0,)), ((), ())),
                                    preferred_element_type=_F32)


def _in_proj_kernel(x_ref, g_ref, b_ref, w_ref, cos_ref, sin_ref,
                    u_ref, q_ref, k_ref, v_ref, sg_ref):
    bidx = pl.program_id(1)

    def normed(s):
        rows = slice(s * SUB_PROJ, (s + 1) * SUB_PROJ)
        return _layer_norm(x_ref[rows, :], g_ref[...], b_ref[...]).astype(_BF)

    def project(s, hn, hn_next):
        rows = slice(s * SUB_PROJ, (s + 1) * SUB_PROJ)
        cos2 = cos_ref[rows, :]
        sin2 = sin_ref[rows, :]
        u = _dot(hn, w_ref[:, 0:S5_WIDTH])
        for j in range(S5_SLABS):
            u_ref[j, pl.ds(bidx + s * SUB_PROJ * BATCH, SUB_PROJ, stride=BATCH), :] = (
                u[:, j * LANES:(j + 1) * LANES])
        q = _dot(hn, w_ref[:, _OFF_Q:_OFF_K])
        nxt = None if hn_next is None else hn_next()
        for h in range(RET_HEADS):
            sl = slice(h * RET_HEAD_DIM, (h + 1) * RET_HEAD_DIM)
            q_ref[rows, sl] = _rope_head(q[:, sl], cos2, sin2).astype(_BF)
        k = _dot(hn, w_ref[:, _OFF_K:_OFF_V])
        for h in range(RET_HEADS):
            sl = slice(h * RET_HEAD_DIM, (h + 1) * RET_HEAD_DIM)
            k_ref[rows, sl] = (_rope_head(k[:, sl], cos2, sin2)
                               * (RET_HEAD_DIM ** -0.5)).astype(_BF)
        g = _dot(hn, w_ref[:, _OFF_G:IN_PROJ_WIDTH])
        sg_ref[rows, :] = (g * jax.nn.sigmoid(g)).astype(_BF)
        v_ref[rows, :] = _dot(hn, w_ref[:, _OFF_V:_OFF_G]).astype(_BF)
        return nxt

    n_sub = TM_PROJ // SUB_PROJ
    hn = normed(0)
    for s in range(n_sub):
        nxt = (lambda s=s: normed(s + 1)) if s + 1 < n_sub else None
        hn = project(s, hn, nxt)


def _s5_kernel(u_ref, init_ref, bblk_ref, cblk_ref, ar_ref, ai_ref, d_ref, wglu_ref, bglu_ref,
               y_ref, x_scr, st_scr):
    @pl.when(pl.program_id(0) == 0)
    def _():
        st_scr[...] = jnp.broadcast_to(init_ref[...], (BATCH, 2 * S5_NSTATE))

    u = jnp.concatenate([u_ref[j] for j in range(S5_SLABS)], axis=1)
    x_scr[...] = _dot(u.astype(_BF), bblk_ref[...])
    ar = jnp.broadcast_to(ar_ref[...], (BATCH, S5_NSTATE))
    ai = jnp.broadcast_to(ai_ref[...], (BATCH, S5_NSTATE))

    def step(t, carry):
        xr, xi = carry
        row = pl.multiple_of(t * BATCH, BATCH)
        br = x_scr[pl.ds(row, BATCH), 0:S5_NSTATE]
        bi = x_scr[pl.ds(row, BATCH), S5_NSTATE:2 * S5_NSTATE]
        nr = ar * xr - ai * xi + br
        ni = ar * xi + ai * xr + bi
        x_scr[pl.ds(row, BATCH), 0:S5_NSTATE] = nr
        x_scr[pl.ds(row, BATCH), S5_NSTATE:2 * S5_NSTATE] = ni
        return nr, ni

    xr, xi = lax.fori_loop(0, TT_S5, step,
                           (st_scr[:, 0:S5_NSTATE], st_scr[:, S5_NSTATE:2 * S5_NSTATE]),
                           unroll=4)
    st_scr[:, 0:S5_NSTATE] = xr
    st_scr[:, S5_NSTATE:2 * S5_NSTATE] = xi

    y = _dot(x_scr[...].astype(_BF), cblk_ref[...]) + d_ref[...] * u
    y = jax.nn.gelu(y)
    gate = jax.nn.sigmoid(_dot(y.astype(_BF), wglu_ref[...]) + bglu_ref[...])
    y = y * gate
    for j in range(S5_SLABS):
        y_ref[j] = y[:, j * LANES:(j + 1) * LANES]


def _ret_kernel(q_ref, k_ref, v_ref, sg_ref, s0_ref, dmat_ref, xi_ref, zeta_ref, gc_ref,
                gng_ref, gnb_ref, y_ref, s_scr):
    @pl.when(pl.program_id(1) == 0)
    def _():
        s_scr[...] = s0_ref[...]

    for c in range(TR_RET // CHUNK):
        rows = slice(c * CHUNK, (c + 1) * CHUNK)
        for h in range(RET_HEADS):
            cols = slice(h * RET_HEAD_DIM, (h + 1) * RET_HEAD_DIM)
            qh = q_ref[rows, cols]
            kh = k_ref[rows, cols]
            vh = v_ref[rows, cols]
            state = s_scr[h]
            scores = lax.dot_general(qh, kh, (((1,), (1,)), ((), ())),
                                     preferred_element_type=_F32) * dmat_ref[h]
            inner = _dot(scores.astype(_BF), vh)
            cross = _dot(qh, state.astype(_BF)) * xi_ref[h]
            kz = (kh.astype(_F32) * zeta_ref[h]).astype(_BF)
            s_scr[h] = gc_ref[h] * state + lax.dot_general(
                kz, vh, (((0,), (0,)), ((), ())), preferred_element_type=_F32)
            o = inner + cross
            mu = jnp.mean(o, axis=-1, keepdims=True)
            oc = o - mu
            var = jnp.mean(oc * oc, axis=-1, keepdims=True)
            on = oc * lax.rsqrt(var + GN_EPS) * gng_ref[:, cols] + gnb_ref[:, cols]
            y_ref[rows, cols] = (sg_ref[rows, cols].astype(_F32) * on).astype(_BF)


def _ffn_kernel(x_ref, ys5_ref, yret_ref, lig_ref, lib_ref, wo1_ref, wo2_ref, l1g_ref, l1b_ref,
                wup_ref, wdn_ref, l2g_ref, l2b_ref, o_ref):
    bidx = pl.program_id(1)
    n_ch = D_FF // FF_CHUNK
    piece = SUB_FFN // n_ch

    def mixed_of(s):
        ys5 = jnp.concatenate(
            [ys5_ref[j, pl.ds(bidx + s * SUB_FFN * BATCH, SUB_FFN, stride=BATCH), :]
             for j in range(S5_SLABS)], axis=1).astype(_BF)
        return (_dot(ys5, wo1_ref[...])
                + _dot(yret_ref[s * SUB_FFN:(s + 1) * SUB_FFN, :], wo2_ref[...]))

    def prep_rows(s, mixed, lo, hi):
        h = _layer_norm(x_ref[s * SUB_FFN + lo:s * SUB_FFN + hi, :], lig_ref[...], lib_ref[...])
        return _layer_norm(DEEPNORM_ALPHA * h + mixed[lo:hi], l1g_ref[...], l1b_ref[...])

    def ffn_chunk(h1b, c):
        cs = slice(c * FF_CHUNK, (c + 1) * FF_CHUNK)
        up = jnp.maximum(_dot(h1b, wup_ref[:, cs]), 0.0)
        return _dot((up * up).astype(_BF), wdn_ref[cs, :])

    def finish_rows(s, pre, lo, hi):
        o_ref[s * SUB_FFN + lo:s * SUB_FFN + hi, :] = _layer_norm(
            pre[lo:hi], l2g_ref[...], l2b_ref[...])

    h1_a = prep_rows(0, mixed_of(0), 0, SUB_FFN)
    h1b_a = h1_a.astype(_BF)
    mixed_b = mixed_of(1)
    pre_a = DEEPNORM_ALPHA * h1_a
    h1_b_parts = []
    for c in range(n_ch):
        pre_a = pre_a + ffn_chunk(h1b_a, c)
        h1_b_parts.append(prep_rows(1, mixed_b, c * piece, (c + 1) * piece))
    h1_b = jnp.concatenate(h1_b_parts, axis=0)
    h1b_b = h1_b.astype(_BF)
    pre_b = DEEPNORM_ALPHA * h1_b
    for c in range(n_ch):
        pre_b = pre_b + ffn_chunk(h1b_b, c)
        finish_rows(0, pre_a, c * piece, (c + 1) * piece)
    finish_rows(1, pre_b, 0, SUB_FFN)


def _row(v):
    return v.reshape(1, -1).astype(_F32)


def kernel(x, meta_tokens, ln_in_g, ln_in_b, w_in, s5_lambda_re, s5_lambda_im, s5_log_dt, s5_b_re, s5_b_im, s5_c_re, s5_c_im, s5_d, s5_w_glu, s5_b_glu, ret_gn_g, ret_gn_b, w_out, ln1_g, ln1_b, w_up, w_down, ln2_g, ln2_b):
    assert x.shape == (BATCH, SEQ, D_MODEL) and w_in.shape[0] == 1
    G, P, H = S5_GROUPS, S5_STATE, S5_GROUP_CH
    nrows = BATCH * SEQ
    arb2 = pltpu.CompilerParams(dimension_semantics=("arbitrary", "arbitrary"),
                                vmem_limit_bytes=VMEM_LIMIT)
    arb1 = pltpu.CompilerParams(dimension_semantics=("arbitrary",), vmem_limit_bytes=VMEM_LIMIT)

    lam_re, lam_im = s5_lambda_re[0], s5_lambda_im[0]
    dt = jnp.exp(s5_log_dt[0])[:, None]
    mag = jnp.exp(lam_re * dt)
    lbr = mag * jnp.cos(lam_im * dt)
    lbi = mag * jnp.sin(lam_im * dt)
    den = lam_re * lam_re + lam_im * lam_im
    nr = lbr - 1.0
    qr = (nr * lam_re + lbi * lam_im) / den
    qi = (lbi * lam_re - nr * lam_im) / den
    bbr = qr[..., None] * s5_b_re[0] - qi[..., None] * s5_b_im[0]
    bbi = qr[..., None] * s5_b_im[0] + qi[..., None] * s5_b_re[0]
    eye = jnp.eye(G, dtype=_F32)

    def blk_in(m):
        return (eye[:, None, :, None] * m.transpose(0, 2, 1)[:, :, None, :]).reshape(G * H, G * P)

    def blk_out(m):
        return (eye[:, None, :, None] * m.transpose(0, 2, 1)[:, :, None, :]).reshape(G * P, G * H)

    bblk = jnp.concatenate([blk_in(bbr), blk_in(bbi)], axis=1).astype(_BF)
    cblk = jnp.concatenate([blk_out(s5_c_re[0]), -blk_out(s5_c_im[0])], axis=0).astype(_BF)
    ar = lbr.reshape(1, S5_NSTATE)
    ai = lbi.reshape(1, S5_NSTATE)

    pos = jnp.arange(N_META + SEQ, dtype=_F32)
    inv_freq = 1.0 / (ROPE_BASE ** (jnp.arange(0, RET_HEAD_DIM, 2, dtype=_F32) / RET_HEAD_DIM))
    ang = pos[:, None] * inv_freq[None, :]
    cos, sin = jnp.cos(ang), jnp.sin(ang)
    cos2 = jnp.concatenate([cos, cos], axis=1)
    sin2 = jnp.concatenate([-sin, sin], axis=1)

    log_gamma = jnp.log1p(-jnp.exp2(-5.0 - jnp.arange(RET_HEADS, dtype=_F32)))
    idx = jnp.arange(CHUNK, dtype=_F32)
    diff = idx[:, None] - idx[None, :]
    dmat = jnp.where(diff[None] >= 0,
                     jnp.exp(jnp.maximum(diff, 0.0)[None] * log_gamma[:, None, None]), 0.0)
    zeta = jnp.exp((CHUNK - 1.0 - idx)[None] * log_gamma[:, None])
    xi = jnp.exp((idx + 1.0)[None] * log_gamma[:, None])
    gamma_chunk = jnp.exp(CHUNK * log_gamma)
    hd = RET_HEAD_DIM
    zeta_b = jnp.broadcast_to(zeta[:, :, None], (RET_HEADS, CHUNK, hd))
    xi_b = jnp.broadcast_to(xi[:, :, None], (RET_HEADS, CHUNK, hd))
    gc_b = jnp.broadcast_to(gamma_chunk[:, None, None], (RET_HEADS, hd, hd))
    zmeta_b = zeta_b[:, CHUNK - N_META:, :]

    w_in_b = w_in[0].astype(_BF)
    w_out_b = w_out[0].astype(_BF)
    w_up_b = w_up[0].astype(_BF)
    w_dn_b = w_down[0].astype(_BF)
    lig, lib = _row(ln_in_g), _row(ln_in_b)

    s5_init, s0 = pl.pallas_call(
        _meta_kernel,
        out_shape=(jax.ShapeDtypeStruct((1, 2 * S5_NSTATE), _F32),
                   jax.ShapeDtypeStruct((RET_HEADS, hd, hd), _F32)),
        compiler_params=pltpu.CompilerParams(vmem_limit_bytes=VMEM_LIMIT),
        name="meta_prologue",
    )(meta_tokens.astype(_F32), lig, lib, w_in_b, bblk, ar, ai,
      cos2[:N_META], sin2[:N_META], zmeta_b)

    x2 = x.reshape(nrows, D_MODEL)
    n_t = SEQ // TM_PROJ
    rows_spec = lambda w: pl.BlockSpec((TM_PROJ, w), lambda i, b: (b * n_t + i, 0))
    u_tm, q, k, v, sg = pl.pallas_call(
        _in_proj_kernel,
        grid=(n_t, BATCH),
        in_specs=[rows_spec(D_MODEL), _const_spec((1, D_MODEL)), _const_spec((1, D_MODEL)),
                  _const_spec((D_MODEL, IN_PROJ_WIDTH)),
                  pl.BlockSpec((TM_PROJ, hd), lambda i, b: (i, 0)),
                  pl.BlockSpec((TM_PROJ, hd), lambda i, b: (i, 0))],
        out_specs=[pl.BlockSpec((S5_SLABS, TM_PROJ * BATCH, LANES), lambda i, b: (0, i, 0)),
                   rows_spec(RET_WIDTH), rows_spec(RET_WIDTH), rows_spec(RET_WIDTH),
                   rows_spec(RET_WIDTH)],
        out_shape=(jax.ShapeDtypeStruct((S5_SLABS, SEQ * BATCH, LANES), _F32),
                   jax.ShapeDtypeStruct((nrows, RET_WIDTH), _BF),
                   jax.ShapeDtypeStruct((nrows, RET_WIDTH), _BF),
                   jax.ShapeDtypeStruct((nrows, RET_WIDTH), _BF),
                   jax.ShapeDtypeStruct((nrows, RET_WIDTH), _BF)),
        compiler_params=arb2,
        name="in_proj",
    )(x2, lig, lib, w_in_b, cos2[N_META:], sin2[N_META:])

    rows_s5 = TT_S5 * BATCH
    ys5_tm = pl.pallas_call(
        _s5_kernel,
        grid=(SEQ // TT_S5,),
        in_specs=[pl.BlockSpec((S5_SLABS, rows_s5, LANES), lambda i: (0, i, 0)),
                  _const_spec((1, 2 * S5_NSTATE)),
                  _const_spec((S5_WIDTH, 2 * S5_NSTATE)), _const_spec((2 * S5_NSTATE, S5_WIDTH)),
                  _const_spec((1, S5_NSTATE)), _const_spec((1, S5_NSTATE)),
                  _const_spec((1, S5_WIDTH)), _const_spec((S5_WIDTH, S5_WIDTH)),
                  _const_spec((1, S5_WIDTH))],
        out_specs=pl.BlockSpec((S5_SLABS, rows_s5, LANES), lambda i: (0, i, 0)),
        out_shape=jax.ShapeDtypeStruct((S5_SLABS, SEQ * BATCH, LANES), _F32),
        scratch_shapes=[pltpu.VMEM((rows_s5, 2 * S5_NSTATE), _F32),
                        pltpu.VMEM((BATCH, 2 * S5_NSTATE), _F32)],
        compiler_params=arb1,
        name="s5_scan",
    )(u_tm, s5_init, bblk, cblk, ar, ai, _row(s5_d[0]), s5_w_glu[0].astype(_BF), _row(s5_b_glu[0]))

    n_r = SEQ // TR_RET
    ret_spec = pl.BlockSpec((TR_RET, RET_WIDTH), lambda b, c: (b * n_r + c, 0))
    tab = lambda n: _const_spec((RET_HEADS, n, hd))
    y_ret = pl.pallas_call(
        _ret_kernel,
        grid=(BATCH, n_r),
        in_specs=[ret_spec, ret_spec, ret_spec, ret_spec, tab(hd), tab(CHUNK), tab(CHUNK),
                  tab(CHUNK), tab(hd), _const_spec((1, RET_WIDTH)), _const_spec((1, RET_WIDTH))],
        out_specs=ret_spec,
        out_shape=jax.ShapeDtypeStruct((nrows, RET_WIDTH), _BF),
        scratch_shapes=[pltpu.VMEM((RET_HEADS, hd, hd), _F32)],
        compiler_params=arb2,
        name="retention",
    )(q, k, v, sg, s0, dmat, xi_b, zeta_b, gc_b, _row(ret_gn_g[0]), _row(ret_gn_b[0]))

    n_f = SEQ // TM_FFN
    frow = lambda w: pl.BlockSpec((TM_FFN, w), lambda i, b: (b * n_f + i, 0))
    out = pl.pallas_call(
        _ffn_kernel,
        grid=(n_f, BATCH),
        in_specs=[frow(D_MODEL),
                  pl.BlockSpec((S5_SLABS, TM_FFN * BATCH, LANES), lambda i, b: (0, i, 0)),
                  frow(RET_WIDTH),
                  _const_spec((1, D_MODEL)), _const_spec((1, D_MODEL)),
                  _const_spec((S5_WIDTH, D_MODEL)), _const_spec((RET_WIDTH, D_MODEL)),
                  _const_spec((1, D_MODEL)), _const_spec((1, D_MODEL)),
                  _const_spec((D_MODEL, D_FF)), _const_spec((D_FF, D_MODEL)),
                  _const_spec((1, D_MODEL)), _const_spec((1, D_MODEL))],
        out_specs=frow(D_MODEL),
        out_shape=jax.ShapeDtypeStruct((nrows, D_MODEL), _F32),
        compiler_params=arb2,
        name="out_ffn",
    )(x2, ys5_tm, y_ret, lig, lib, w_out_b[:S5_WIDTH], w_out_b[S5_WIDTH:],
      _row(ln1_g[0]), _row(ln1_b[0]), w_up_b, w_dn_b, _row(ln2_g[0]), _row(ln2_b[0]))

    return out.reshape(BATCH, SEQ, D_MODEL)
```

```python
import functools
import math

import jax
import jax.numpy as jnp
from jax import lax
from jax.experimental import pallas as pl
from jax.experimental.pallas import tpu as pltpu

D_MODEL = 1024
BATCH = 8
SEQ = 4096
N_META = 16
S5_GROUP_CH = 16
S5_STATE = 64
S5_WIDTH = 256
S5_GROUPS = S5_WIDTH // S5_GROUP_CH
S5_NSTATE = S5_GROUPS * S5_STATE
RET_HEAD_DIM = 128
RET_WIDTH = 768
RET_HEADS = RET_WIDTH // RET_HEAD_DIM
CHUNK = 128
ROPE_BASE = 10000.0
D_FF = 4 * D_MODEL
LANES = 128
S5_SLABS = S5_WIDTH // LANES
LN_EPS = 1e-5
GN_EPS = 1e-5
IN_PROJ_WIDTH = S5_WIDTH + 4 * RET_WIDTH
DEEPNORM_ALPHA = 2.0 ** 0.25

_OFF_Q = S5_WIDTH
_OFF_K = _OFF_Q + RET_WIDTH
_OFF_V = _OFF_K + RET_WIDTH
_OFF_G = _OFF_V + RET_WIDTH

TM_PROJ = 512
SUB_PROJ = 256
MXU_TILE = 256
TT_S5 = 1024
SB_S5 = 32
TR_RET = 512
TM_FFN = 512
SUB_FFN = 256
FF_CHUNK = 1024
VMEM_LIMIT = 56 * 1024 * 1024

_BF = jnp.bfloat16
_F32 = jnp.float32


def _const_spec(shape):
    nd = len(shape)
    return pl.BlockSpec(shape, lambda *_: (0,) * nd, pipeline_mode=pl.Buffered(1))


def _layer_norm(x, g, b):
    mu = jnp.mean(x, axis=-1, keepdims=True)
    xc = x - mu
    var = jnp.mean(xc * xc, axis=-1, keepdims=True)
    return xc * lax.rsqrt(var + LN_EPS) * g + b


def _dot(a, b):
    return jnp.dot(a, b, preferred_element_type=_F32)


def _rope_head(t, cos2, sin2):
    return t * cos2 + pltpu.roll(t, RET_HEAD_DIM // 2, 1) * sin2


def _meta_kernel(meta_ref, g_ref, b_ref, w_ref, bblk_ref, ar_ref, ai_ref, cos_ref, sin_ref,
                 zmeta_ref, s5_ref, s0_ref):
    hm = _layer_norm(meta_ref[...], g_ref[...], b_ref[...]).astype(_BF)
    u = _dot(hm, w_ref[:, 0:S5_WIDTH])
    bu = _dot(u.astype(_BF), bblk_ref[...])
    ar = ar_ref[...]
    ai = ai_ref[...]
    xr = jnp.zeros((1, S5_NSTATE), _F32)
    xi = jnp.zeros((1, S5_NSTATE), _F32)
    for t in range(N_META):
        br = bu[t:t + 1, 0:S5_NSTATE]
        bi = bu[t:t + 1, S5_NSTATE:2 * S5_NSTATE]
        xr, xi = ar * xr - ai * xi + br, ar * xi + ai * xr + bi
    s5_ref[:, 0:S5_NSTATE] = xr
    s5_ref[:, S5_NSTATE:2 * S5_NSTATE] = xi

    k = _dot(hm, w_ref[:, _OFF_K:_OFF_V])
    v = _dot(hm, w_ref[:, _OFF_V:_OFF_G]).astype(_BF)
    cos2 = cos_ref[...]
    sin2 = sin_ref[...]
    for h in range(RET_HEADS):
        sl = slice(h * RET_HEAD_DIM, (h + 1) * RET_HEAD_DIM)
        kh = _rope_head(k[:, sl], cos2, sin2) * (RET_HEAD_DIM ** -0.5)
        kz = (kh * zmeta_ref[h]).astype(_BF)
        s0_ref[h] = lax.dot_general(kz, v[:, sl], (((0,), (0,)), ((), ())),
                                    preferred_element_type=_F32)


def _in_proj_kernel(x_ref, g_ref, b_ref, w_ref, cos_ref, sin_ref,
                    u_ref, q_ref, k_ref, v_ref, sg_ref):
    bidx = pl.program_id(1)

    def normed(s):
        rows = slice(s * SUB_PROJ, (s + 1) * SUB_PROJ)
        return _layer_norm(x_ref[rows, :], g_ref[...], b_ref[...]).astype(_BF)

    def project(s, hn, hn_next):
        rows = slice(s * SUB_PROJ, (s + 1) * SUB_PROJ)
        cos2 = cos_ref[rows, :]
        sin2 = sin_ref[rows, :]
        u = _dot(hn, w_ref[:, 0:S5_WIDTH])
        for j in range(S5_SLABS):
            u_ref[j, pl.ds(bidx + s * SUB_PROJ * BATCH, SUB_PROJ, stride=BATCH), :] = (
                u[:, j * LANES:(j + 1) * LANES])
        q = _dot(hn, w_ref[:, _OFF_Q:_OFF_K])
        nxt = None if hn_next is None else hn_next()
        for h in range(RET_HEADS):
            sl = slice(h * RET_HEAD_DIM, (h + 1) * RET_HEAD_DIM)
            q_ref[rows, sl] = _rope_head(q[:, sl], cos2, sin2).astype(_BF)
        k = _dot(hn, w_ref[:, _OFF_K:_OFF_V])
        for h in range(RET_HEADS):
            sl = slice(h * RET_HEAD_DIM, (h + 1) * RET_HEAD_DIM)
            k_ref[rows, sl] = (_rope_head(k[:, sl], cos2, sin2)
                               * (RET_HEAD_DIM ** -0.5)).astype(_BF)
        g = _dot(hn, w_ref[:, _OFF_G:IN_PROJ_WIDTH])
        sg_ref[rows, :] = (g * jax.nn.sigmoid(g)).astype(_BF)
        v_ref[rows, :] = _dot(hn, w_ref[:, _OFF_V:_OFF_G]).astype(_BF)
        return nxt

    n_sub = TM_PROJ // SUB_PROJ
    hn = normed(0)
    for s in range(n_sub):
        nxt = (lambda s=s: normed(s + 1)) if s + 1 < n_sub else None
        hn = project(s, hn, nxt)


def _s5_kernel(u_ref, init_ref, bblk_ref, cblk_ref, ar_ref, ai_ref, d_ref, wglu_ref, bglu_ref,
               y_ref, bu0_scr, bu1_scr, xb0_scr, xb1_scr, st_scr):
    @pl.when(pl.program_id(0) == 0)
    def _():
        st_scr[...] = jnp.broadcast_to(init_ref[...], (BATCH, 2 * S5_NSTATE))

    rows_sb = SB_S5 * BATCH
    n_sb = TT_S5 // SB_S5
    n_piece = 2 * S5_NSTATE // MXU_TILE
    steps_piece = SB_S5 // n_piece
    re = slice(0, S5_NSTATE)
    im = slice(S5_NSTATE, 2 * S5_NSTATE)
    ar = jnp.broadcast_to(ar_ref[...], (BATCH, S5_NSTATE))
    ai = jnp.broadcast_to(ai_ref[...], (BATCH, S5_NSTATE))

    def rows_of(j):
        if isinstance(j, int):
            return pl.ds(j * rows_sb, rows_sb)
        return pl.ds(pl.multiple_of(j * rows_sb, rows_sb), rows_sb)

    def load_u(j):
        return jnp.concatenate([u_ref[s, rows_of(j), :] for s in range(S5_SLABS)], axis=1)

    bu_scr = (bu0_scr, bu1_scr)
    xb_scr = (xb0_scr, xb1_scr)

    def a_piece(ub, slot, c):
        cols = slice(c * MXU_TILE, (c + 1) * MXU_TILE)
        bu_scr[slot][:, cols] = _dot(ub, bblk_ref[:, cols])

    def b_piece(slot, c, xr, xi):
        for t in range(c * steps_piece, (c + 1) * steps_piece, 2):
            out_r, out_i = [], []
            for tt in (t, t + 1):
                rows = slice(tt * BATCH, (tt + 1) * BATCH)
                br = bu_scr[slot][rows, re]
                bi = bu_scr[slot][rows, im]
                xr, xi = ar * xr - ai * xi + br, ar * xi + ai * xr + bi
                out_r.append(xr)
                out_i.append(xi)
            rows2 = slice(t * BATCH, (t + 2) * BATCH)
            xb_scr[slot][rows2, re] = jnp.concatenate(out_r, axis=0).astype(_BF)
            xb_scr[slot][rows2, im] = jnp.concatenate(out_i, axis=0).astype(_BF)
        return xr, xi

    def c_piece(slot, c, acc):
        cols = slice(c * MXU_TILE, (c + 1) * MXU_TILE)
        part = _dot(xb_scr[slot][:, cols], cblk_ref[cols, :])
        return part if acc is None else acc + part

    def c_finish(j, acc):
        y = jax.nn.gelu(acc + d_ref[...] * load_u(j))
        gate = jax.nn.sigmoid(_dot(y.astype(_BF), wglu_ref[...]) + bglu_ref[...])
        y = y * gate
        for s in range(S5_SLABS):
            y_ref[s, rows_of(j), :] = y[:, s * LANES:(s + 1) * LANES]

    def iteration(j, slot, xr, xi, do_a, do_c):
        other = 1 - slot
        ub = load_u(j + 1).astype(_BF) if do_a else None
        acc = None
        for c in range(n_piece):
            if do_a:
                a_piece(ub, other, c)
            xr, xi = b_piece(slot, c, xr, xi)
            if do_c:
                acc = c_piece(other, c, acc)
        if do_c:
            c_finish(j - 1, acc)
        return xr, xi

    ub0 = load_u(0).astype(_BF)
    for c in range(n_piece):
        a_piece(ub0, 0, c)
    xr, xi = iteration(0, 0, st_scr[:, re], st_scr[:, im], True, False)

    def pair(p, carry):
        j = 2 * p + 1
        xr, xi = iteration(j, 1, carry[0], carry[1], True, True)
        return iteration(j + 1, 0, xr, xi, True, True)

    xr, xi = lax.fori_loop(0, (n_sb - 2) // 2, pair, (xr, xi))
    xr, xi = iteration(n_sb - 1, 1, xr, xi, False, True)
    st_scr[:, re] = xr
    st_scr[:, im] = xi
    acc = None
    for c in range(n_piece):
        acc = c_piece(1, c, acc)
    c_finish(n_sb - 1, acc)


def _ret_kernel(q_ref, k_ref, v_ref, sg_ref, s0_ref, dmat_ref, xi_ref, zeta_ref, gc_ref,
                gng_ref, gnb_ref, y_ref, s_scr):
    @pl.when(pl.program_id(1) == 0)
    def _():
        s_scr[...] = s0_ref[...]

    for c in range(TR_RET // CHUNK):
        rows = slice(c * CHUNK, (c + 1) * CHUNK)
        for h in range(RET_HEADS):
            cols = slice(h * RET_HEAD_DIM, (h + 1) * RET_HEAD_DIM)
            qh = q_ref[rows, cols]
            kh = k_ref[rows, cols]
            vh = v_ref[rows, cols]
            state = s_scr[h]
            scores = lax.dot_general(qh, kh, (((1,), (1,)), ((), ())),
                                     preferred_element_type=_F32) * dmat_ref[h]
            inner = _dot(scores.astype(_BF), vh)
            cross = _dot(qh, state.astype(_BF)) * xi_ref[h]
            kz = (kh.astype(_F32) * zeta_ref[h]).astype(_BF)
            s_scr[h] = gc_ref[h] * state + lax.dot_general(
                kz, vh, (((0,), (0,)), ((), ())), preferred_element_type=_F32)
            o = inner + cross
            mu = jnp.mean(o, axis=-1, keepdims=True)
            oc = o - mu
            var = jnp.mean(oc * oc, axis=-1, keepdims=True)
            on = oc * lax.rsqrt(var + GN_EPS) * gng_ref[:, cols] + gnb_ref[:, cols]
            y_ref[rows, cols] = (sg_ref[rows, cols].astype(_F32) * on).astype(_BF)


def _ffn_kernel(x_ref, ys5_ref, yret_ref, lig_ref, lib_ref, wo1_ref, wo2_ref, l1g_ref, l1b_ref,
                wup_ref, wdn_ref, l2g_ref, l2b_ref, o_ref):
    bidx = pl.program_id(1)
    n_ch = D_FF // FF_CHUNK
    piece = SUB_FFN // n_ch

    def mixed_of(s):
        ys5 = jnp.concatenate(
            [ys5_ref[j, pl.ds(bidx + s * SUB_FFN * BATCH, SUB_FFN, stride=BATCH), :]
             for j in range(S5_SLABS)], axis=1).astype(_BF)
        return (_dot(ys5, wo1_ref[...])
                + _dot(yret_ref[s * SUB_FFN:(s + 1) * SUB_FFN, :], wo2_ref[...]))

    def prep_rows(s, mixed, lo, hi):
        h = _layer_norm(x_ref[s * SUB_FFN + lo:s * SUB_FFN + hi, :], lig_ref[...], lib_ref[...])
        return _layer_norm(DEEPNORM_ALPHA * h + mixed[lo:hi], l1g_ref[...], l1b_ref[...])

    def ffn_chunk(h1b, c):
        cs = slice(c * FF_CHUNK, (c + 1) * FF_CHUNK)
        up = jnp.maximum(_dot(h1b, wup_ref[:, cs]), 0.0)
        return _dot((up * up).astype(_BF), wdn_ref[cs, :])

    def finish_rows(s, pre, lo, hi):
        o_ref[s * SUB_FFN + lo:s * SUB_FFN + hi, :] = _layer_norm(
            pre[lo:hi], l2g_ref[...], l2b_ref[...])

    h1_a = prep_rows(0, mixed_of(0), 0, SUB_FFN)
    h1b_a = h1_a.astype(_BF)
    mixed_b = mixed_of(1)
    pre_a = DEEPNORM_ALPHA * h1_a
    h1_b_parts = []
    for c in range(n_ch):
        pre_a = pre_a + ffn_chunk(h1b_a, c)
        h1_b_parts.append(prep_rows(1, mixed_b, c * piece, (c + 1) * piece))
    h1_b = jnp.concatenate(h1_b_parts, axis=0)
    h1b_b = h1_b.astype(_BF)
    pre_b = DEEPNORM_ALPHA * h1_b
    for c in range(n_ch):
        pre_b = pre_b + ffn_chunk(h1b_b, c)
        finish_rows(0, pre_a, c * piece, (c + 1) * piece)
    finish_rows(1, pre_b, 0, SUB_FFN)


def _row(v):
    return v.reshape(1, -1).astype(_F32)


def kernel(x, meta_tokens, ln_in_g, ln_in_b, w_in, s5_lambda_re, s5_lambda_im, s5_log_dt, s5_b_re, s5_b_im, s5_c_re, s5_c_im, s5_d, s5_w_glu, s5_b_glu, ret_gn_g, ret_gn_b, w_out, ln1_g, ln1_b, w_up, w_down, ln2_g, ln2_b):
    assert x.shape == (BATCH, SEQ, D_MODEL) and w_in.shape[0] == 1
    G, P, H = S5_GROUPS, S5_STATE, S5_GROUP_CH
    nrows = BATCH * SEQ
    arb2 = pltpu.CompilerParams(dimension_semantics=("arbitrary", "arbitrary"),
                                vmem_limit_bytes=VMEM_LIMIT)
    arb1 = pltpu.CompilerParams(dimension_semantics=("arbitrary",), vmem_limit_bytes=VMEM_LIMIT)

    lam_re, lam_im = s5_lambda_re[0], s5_lambda_im[0]
    dt = jnp.exp(s5_log_dt[0])[:, None]
    mag = jnp.exp(lam_re * dt)
    lbr = mag * jnp.cos(lam_im * dt)
    lbi = mag * jnp.sin(lam_im * dt)
    den = lam_re * lam_re + lam_im * lam_im
    nr = lbr - 1.0
    qr = (nr * lam_re + lbi * lam_im) / den
    qi = (lbi * lam_re - nr * lam_im) / den
    bbr = qr[..., None] * s5_b_re[0] - qi[..., None] * s5_b_im[0]
    bbi = qr[..., None] * s5_b_im[0] + qi[..., None] * s5_b_re[0]
    eye = jnp.eye(G, dtype=_F32)

    def blk_in(m):
        return (eye[:, None, :, None] * m.transpose(0, 2, 1)[:, :, None, :]).reshape(G * H, G * P)

    def blk_out(m):
        return (eye[:, None, :, None] * m.transpose(0, 2, 1)[:, :, None, :]).reshape(G * P, G * H)

    bblk = jnp.concatenate([blk_in(bbr), blk_in(bbi)], axis=1).astype(_BF)
    cblk = jnp.concatenate([blk_out(s5_c_re[0]), -blk_out(s5_c_im[0])], axis=0).astype(_BF)
    ar = lbr.reshape(1, S5_NSTATE)
    ai = lbi.reshape(1, S5_NSTATE)

    pos = jnp.arange(N_META + SEQ, dtype=_F32)
    inv_freq = 1.0 / (ROPE_BASE ** (jnp.arange(0, RET_HEAD_DIM, 2, dtype=_F32) / RET_HEAD_DIM))
    ang = pos[:, None] * inv_freq[None, :]
    cos, sin = jnp.cos(ang), jnp.sin(ang)
    cos2 = jnp.concatenate([cos, cos], axis=1)
    sin2 = jnp.concatenate([-sin, sin], axis=1)

    log_gamma = jnp.log1p(-jnp.exp2(-5.0 - jnp.arange(RET_HEADS, dtype=_F32)))
    idx = jnp.arange(CHUNK, dtype=_F32)
    diff = idx[:, None] - idx[None, :]
    dmat = jnp.where(diff[None] >= 0,
                     jnp.exp(jnp.maximum(diff, 0.0)[None] * log_gamma[:, None, None]), 0.0)
    zeta = jnp.exp((CHUNK - 1.0 - idx)[None] * log_gamma[:, None])
    xi = jnp.exp((idx + 1.0)[None] * log_gamma[:, None])
    gamma_chunk = jnp.exp(CHUNK * log_gamma)
    hd = RET_HEAD_DIM
    zeta_b = jnp.broadcast_to(zeta[:, :, None], (RET_HEADS, CHUNK, hd))
    xi_b = jnp.broadcast_to(xi[:, :, None], (RET_HEADS, CHUNK, hd))
    gc_b = jnp.broadcast_to(gamma_chunk[:, None, None], (RET_HEADS, hd, hd))
    zmeta_b = zeta_b[:, CHUNK - N_META:, :]

    w_in_b = w_in[0].astype(_BF)
    w_out_b = w_out[0].astype(_BF)
    w_up_b = w_up[0].astype(_BF)
    w_dn_b = w_down[0].astype(_BF)
    lig, lib = _row(ln_in_g), _row(ln_in_b)

    s5_init, s0 = pl.pallas_call(
        _meta_kernel,
        out_shape=(jax.ShapeDtypeStruct((1, 2 * S5_NSTATE), _F32),
                   jax.ShapeDtypeStruct((RET_HEADS, hd, hd), _F32)),
        compiler_params=pltpu.CompilerParams(vmem_limit_bytes=VMEM_LIMIT),
        name="meta_prologue",
    )(meta_tokens.astype(_F32), lig, lib, w_in_b, bblk, ar, ai,
      cos2[:N_META], sin2[:N_META], zmeta_b)

    x2 = x.reshape(nrows, D_MODEL)
    n_t = SEQ // TM_PROJ
    rows_spec = lambda w: pl.BlockSpec((TM_PROJ, w), lambda i, b: (b * n_t + i, 0))
    u_tm, q, k, v, sg = pl.pallas_call(
        _in_proj_kernel,
        grid=(n_t, BATCH),
        in_specs=[rows_spec(D_MODEL), _const_spec((1, D_MODEL)), _const_spec((1, D_MODEL)),
                  _const_spec((D_MODEL, IN_PROJ_WIDTH)),
                  pl.BlockSpec((TM_PROJ, hd), lambda i, b: (i, 0)),
                  pl.BlockSpec((TM_PROJ, hd), lambda i, b: (i, 0))],
        out_specs=[pl.BlockSpec((S5_SLABS, TM_PROJ * BATCH, LANES), lambda i, b: (0, i, 0)),
                   rows_spec(RET_WIDTH), rows_spec(RET_WIDTH), rows_spec(RET_WIDTH),
                   rows_spec(RET_WIDTH)],
        out_shape=(jax.ShapeDtypeStruct((S5_SLABS, SEQ * BATCH, LANES), _F32),
                   jax.ShapeDtypeStruct((nrows, RET_WIDTH), _BF),
                   jax.ShapeDtypeStruct((nrows, RET_WIDTH), _BF),
                   jax.ShapeDtypeStruct((nrows, RET_WIDTH), _BF),
                   jax.ShapeDtypeStruct((nrows, RET_WIDTH), _BF)),
        compiler_params=arb2,
        name="in_proj",
    )(x2, lig, lib, w_in_b, cos2[N_META:], sin2[N_META:])

    rows_s5 = TT_S5 * BATCH
    ys5_tm = pl.pallas_call(
        _s5_kernel,
        grid=(SEQ // TT_S5,),
        in_specs=[pl.BlockSpec((S5_SLABS, rows_s5, LANES), lambda i: (0, i, 0)),
                  _const_spec((1, 2 * S5_NSTATE)),
                  _const_spec((S5_WIDTH, 2 * S5_NSTATE)), _const_spec((2 * S5_NSTATE, S5_WIDTH)),
                  _const_spec((1, S5_NSTATE)), _const_spec((1, S5_NSTATE)),
                  _const_spec((1, S5_WIDTH)), _const_spec((S5_WIDTH, S5_WIDTH)),
                  _const_spec((1, S5_WIDTH))],
        out_specs=pl.BlockSpec((S5_SLABS, rows_s5, LANES), lambda i: (0, i, 0)),
        out_shape=jax.ShapeDtypeStruct((S5_SLABS, SEQ * BATCH, LANES), _F32),
        scratch_shapes=[pltpu.VMEM((SB_S5 * BATCH, 2 * S5_NSTATE), _F32),
                        pltpu.VMEM((SB_S5 * BATCH, 2 * S5_NSTATE), _F32),
                        pltpu.VMEM((SB_S5 * BATCH, 2 * S5_NSTATE), _BF),
                        pltpu.VMEM((SB_S5 * BATCH, 2 * S5_NSTATE), _BF),
                        pltpu.VMEM((BATCH, 2 * S5_NSTATE), _F32)],
        compiler_params=arb1,
        name="s5_scan",
    )(u_tm, s5_init, bblk, cblk, ar, ai, _row(s5_d[0]), s5_w_glu[0].astype(_BF), _row(s5_b_glu[0]))

    n_r = SEQ // TR_RET
    ret_spec = pl.BlockSpec((TR_RET, RET_WIDTH), lambda b, c: (b * n_r + c, 0))
    tab = lambda n: _const_spec((RET_HEADS, n, hd))
    y_ret = pl.pallas_call(
        _ret_kernel,
        grid=(BATCH, n_r),
        in_specs=[ret_spec, ret_spec, ret_spec, ret_spec, tab(hd), tab(CHUNK), tab(CHUNK),
                  tab(CHUNK), tab(hd), _const_spec((1, RET_WIDTH)), _const_spec((1, RET_WIDTH))],
        out_specs=ret_spec,
        out_shape=jax.ShapeDtypeStruct((nrows, RET_WIDTH), _BF),
        scratch_shapes=[pltpu.VMEM((RET_HEADS, hd, hd), _F32)],
        compiler_params=arb2,
        name="retention",
    )(q, k, v, sg, s0, dmat, xi_b, zeta_b, gc_b, _row(ret_gn_g[0]), _row(ret_gn_b[0]))

    n_f = SEQ // TM_FFN
    frow = lambda w: pl.BlockSpec((TM_FFN, w), lambda i, b: (b * n_f + i, 0))
    out = pl.pallas_call(
        _ffn_kernel,
        grid=(n_f, BATCH),
        in_specs=[frow(D_MODEL),
                  pl.BlockSpec((S5_SLABS, TM_FFN * BATCH, LANES), lambda i, b: (0, i, 0)),
                  frow(RET_WIDTH),
                  _const_spec((1, D_MODEL)), _const_spec((1, D_MODEL)),
                  _const_spec((S5_WIDTH, D_MODEL)), _const_spec((RET_WIDTH, D_MODEL)),
                  _const_spec((1, D_MODEL)), _const_spec((1, D_MODEL)),
                  _const_spec((D_MODEL, D_FF)), _const_spec((D_FF, D_MODEL)),
                  _const_spec((1, D_MODEL)), _const_spec((1, D_MODEL))],
        out_specs=frow(D_MODEL),
        out_shape=jax.ShapeDtypeStruct((nrows, D_MODEL), _F32),
        compiler_params=arb2,
        name="out_ffn",
    )(x2, ys5_tm, y_ret, lig, lib, w_out_b[:S5_WIDTH], w_out_b[S5_WIDTH:],
      _row(ln1_g[0]), _row(ln1_b[0]), w_up_b, w_dn_b, _row(ln2_g[0]), _row(ln2_b[0]))

    return out.reshape(BATCH, SEQ, D_MODEL)
```

```python
import functools
import math

import jax
import jax.numpy as jnp
import numpy as np
from jax import lax
from jax.experimental import pallas as pl
from jax.experimental.pallas import tpu as pltpu

D_MODEL = 1024
BATCH = 8
SEQ = 4096
N_META = 16
S5_GROUP_CH = 16
S5_STATE = 64
S5_WIDTH = 256
S5_GROUPS = S5_WIDTH // S5_GROUP_CH
S5_NSTATE = S5_GROUPS * S5_STATE
RET_HEAD_DIM = 128
RET_WIDTH = 768
RET_HEADS = RET_WIDTH // RET_HEAD_DIM
CHUNK = 128
ROPE_BASE = 10000.0
D_FF = 4 * D_MODEL
LANES = 128
S5_SLABS = S5_WIDTH // LANES
LN_EPS = 1e-5
GN_EPS = 1e-5
IN_PROJ_WIDTH = S5_WIDTH + 4 * RET_WIDTH
DEEPNORM_ALPHA = 2.0 ** 0.25

_OFF_Q = S5_WIDTH
_OFF_K = _OFF_Q + RET_WIDTH
_OFF_V = _OFF_K + RET_WIDTH
_OFF_G = _OFF_V + RET_WIDTH

TM_PROJ = 512
SUB_PROJ = 256
MXU_TILE = 256
TT_S5 = 1024
SB_S5 = 32
TR_RET = 512
TM_FFN = 512
SUB_FFN = 256
FF_CHUNK = 1024
VMEM_LIMIT = 56 * 1024 * 1024

_BF = jnp.bfloat16
_F32 = jnp.float32


def _const_spec(shape):
    nd = len(shape)
    return pl.BlockSpec(shape, lambda *_: (0,) * nd, pipeline_mode=pl.Buffered(1))


def _layer_norm(x, g, b):
    mu = jnp.mean(x, axis=-1, keepdims=True)
    xc = x - mu
    var = jnp.mean(xc * xc, axis=-1, keepdims=True)
    return xc * lax.rsqrt(var + LN_EPS) * g + b


def _dot(a, b):
    return jnp.dot(a, b, preferred_element_type=_F32)


def _rope_head(t, cos2, sin2):
    return t * cos2 + pltpu.roll(t, RET_HEAD_DIM // 2, 1) * sin2


def _meta_kernel(meta_ref, g_ref, b_ref, w_ref, bblk_ref, ar_ref, ai_ref, cos_ref, sin_ref,
                 zmeta_ref, s5_ref, s0_ref):
    hm = _layer_norm(meta_ref[...], g_ref[...], b_ref[...]).astype(_BF)
    u = _dot(hm, w_ref[:, 0:S5_WIDTH])
    bu = _dot(u.astype(_BF), bblk_ref[...])
    ar = ar_ref[...]
    ai = ai_ref[...]
    xr = jnp.zeros((1, S5_NSTATE), _F32)
    xi = jnp.zeros((1, S5_NSTATE), _F32)
    for t in range(N_META):
        br = bu[t:t + 1, 0:S5_NSTATE]
        bi = bu[t:t + 1, S5_NSTATE:2 * S5_NSTATE]
        xr, xi = ar * xr - ai * xi + br, ar * xi + ai * xr + bi
    s5_ref[:, 0:S5_NSTATE] = xr
    s5_ref[:, S5_NSTATE:2 * S5_NSTATE] = xi

    k = _dot(hm, w_ref[:, _OFF_K:_OFF_V])
    v = _dot(hm, w_ref[:, _OFF_V:_OFF_G]).astype(_BF)
    cos2 = cos_ref[...]
    sin2 = sin_ref[...]
    for h in range(RET_HEADS):
        sl = slice(h * RET_HEAD_DIM, (h + 1) * RET_HEAD_DIM)
        kh = _rope_head(k[:, sl], cos2, sin2) * (RET_HEAD_DIM ** -0.5)
        kz = (kh * zmeta_ref[h]).astype(_BF)
        s0_ref[h] = lax.dot_general(kz, v[:, sl], (((0,), (0,)), ((), ())),
                                    preferred_element_type=_F32)


def _in_proj_kernel(x_ref, g_ref, b_ref, w_ref, cos_ref, sin_ref,
                    u_ref, q_ref, k_ref, v_ref, sg_ref):
    bidx = pl.program_id(1)

    def normed(s):
        rows = slice(s * SUB_PROJ, (s + 1) * SUB_PROJ)
        return _layer_norm(x_ref[rows, :], g_ref[...], b_ref[...]).astype(_BF)

    def project(s, hn, hn_next):
        rows = slice(s * SUB_PROJ, (s + 1) * SUB_PROJ)
        cos2 = cos_ref[rows, :]
        sin2 = sin_ref[rows, :]
        u = _dot(hn, w_ref[:, 0:S5_WIDTH])
        for j in range(S5_SLABS):
            u_ref[j, pl.ds(bidx + s * SUB_PROJ * BATCH, SUB_PROJ, stride=BATCH), :] = (
                u[:, j * LANES:(j + 1) * LANES])
        q = _dot(hn, w_ref[:, _OFF_Q:_OFF_K])
        nxt = None if hn_next is None else hn_next()
        for h in range(RET_HEADS):
            sl = slice(h * RET_HEAD_DIM, (h + 1) * RET_HEAD_DIM)
            q_ref[rows, sl] = _rope_head(q[:, sl], cos2, sin2).astype(_BF)
        k = _dot(hn, w_ref[:, _OFF_K:_OFF_V])
        for h in range(RET_HEADS):
            sl = slice(h * RET_HEAD_DIM, (h + 1) * RET_HEAD_DIM)
            k_ref[rows, sl] = (_rope_head(k[:, sl], cos2, sin2)
                               * (RET_HEAD_DIM ** -0.5)).astype(_BF)
        g = _dot(hn, w_ref[:, _OFF_G:IN_PROJ_WIDTH])
        sg_ref[rows, :] = (g * jax.nn.sigmoid(g)).astype(_BF)
        v_ref[rows, :] = _dot(hn, w_ref[:, _OFF_V:_OFF_G]).astype(_BF)
        return nxt

    n_sub = TM_PROJ // SUB_PROJ
    hn = normed(0)
    for s in range(n_sub):
        nxt = (lambda s=s: normed(s + 1)) if s + 1 < n_sub else None
        hn = project(s, hn, nxt)


def _s5_kernel(u_ref, init_ref, bblk_ref, cblk_ref, ar_ref, ai_ref, d_ref, wglu_ref, bglu_ref,
               y_ref, bu0_scr, bu1_scr, xb0_scr, xb1_scr, st_scr):
    @pl.when(pl.program_id(0) == 0)
    def _():
        st_scr[...] = jnp.broadcast_to(init_ref[...], (BATCH, 2 * S5_NSTATE))

    rows_sb = SB_S5 * BATCH
    n_sb = TT_S5 // SB_S5
    n_piece = 2 * S5_NSTATE // MXU_TILE
    steps_piece = SB_S5 // n_piece
    re = slice(0, S5_NSTATE)
    im = slice(S5_NSTATE, 2 * S5_NSTATE)
    ar = jnp.broadcast_to(ar_ref[...], (BATCH, S5_NSTATE))
    ai = jnp.broadcast_to(ai_ref[...], (BATCH, S5_NSTATE))

    def rows_of(j):
        if isinstance(j, int):
            return pl.ds(j * rows_sb, rows_sb)
        return pl.ds(pl.multiple_of(j * rows_sb, rows_sb), rows_sb)

    def load_u(j):
        return jnp.concatenate([u_ref[s, rows_of(j), :] for s in range(S5_SLABS)], axis=1)

    bu_scr = (bu0_scr, bu1_scr)
    xb_scr = (xb0_scr, xb1_scr)

    def a_piece(ub, slot, c):
        cols = slice(c * MXU_TILE, (c + 1) * MXU_TILE)
        bu_scr[slot][:, cols] = _dot(ub, bblk_ref[:, cols])

    def b_piece(slot, c, xr, xi):
        for t in range(c * steps_piece, (c + 1) * steps_piece, 2):
            out_r, out_i = [], []
            for tt in (t, t + 1):
                rows = slice(tt * BATCH, (tt + 1) * BATCH)
                br = bu_scr[slot][rows, re]
                bi = bu_scr[slot][rows, im]
                xr, xi = ar * xr - ai * xi + br, ar * xi + ai * xr + bi
                out_r.append(xr)
                out_i.append(xi)
            rows2 = slice(t * BATCH, (t + 2) * BATCH)
            xb_scr[slot][rows2, re] = jnp.concatenate(out_r, axis=0).astype(_BF)
            xb_scr[slot][rows2, im] = jnp.concatenate(out_i, axis=0).astype(_BF)
        return xr, xi

    def c_piece(slot, c, acc):
        cols = slice(c * MXU_TILE, (c + 1) * MXU_TILE)
        part = _dot(xb_scr[slot][:, cols], cblk_ref[cols, :])
        return part if acc is None else acc + part

    def c_finish(j, acc):
        y = jax.nn.gelu(acc + d_ref[...] * load_u(j))
        gate = jax.nn.sigmoid(_dot(y.astype(_BF), wglu_ref[...]) + bglu_ref[...])
        y = y * gate
        for s in range(S5_SLABS):
            y_ref[s, rows_of(j), :] = y[:, s * LANES:(s + 1) * LANES]

    def iteration(j, slot, xr, xi, do_a, do_c):
        other = 1 - slot
        ub = load_u(j + 1).astype(_BF) if do_a else None
        acc = None
        for c in range(n_piece):
            if do_a:
                a_piece(ub, other, c)
            xr, xi = b_piece(slot, c, xr, xi)
            if do_c:
                acc = c_piece(other, c, acc)
        if do_c:
            c_finish(j - 1, acc)
        return xr, xi

    ub0 = load_u(0).astype(_BF)
    for c in range(n_piece):
        a_piece(ub0, 0, c)
    xr, xi = iteration(0, 0, st_scr[:, re], st_scr[:, im], True, False)

    def pair(p, carry):
        j = 2 * p + 1
        xr, xi = iteration(j, 1, carry[0], carry[1], True, True)
        return iteration(j + 1, 0, xr, xi, True, True)

    xr, xi = lax.fori_loop(0, (n_sb - 2) // 2, pair, (xr, xi))
    xr, xi = iteration(n_sb - 1, 1, xr, xi, False, True)
    st_scr[:, re] = xr
    st_scr[:, im] = xi
    acc = None
    for c in range(n_piece):
        acc = c_piece(1, c, acc)
    c_finish(n_sb - 1, acc)


def _ret_kernel(q_ref, k_ref, v_ref, sg_ref, s0_ref, dmat_ref, xi_ref, zeta_ref, gc_ref,
                gng_ref, gnb_ref, y_ref, s_scr):
    @pl.when(pl.program_id(1) == 0)
    def _():
        s_scr[...] = s0_ref[...]

    heads = [slice(h * RET_HEAD_DIM, (h + 1) * RET_HEAD_DIM) for h in range(RET_HEADS)]
    for c in range(TR_RET // CHUNK):
        rows = slice(c * CHUNK, (c + 1) * CHUNK)
        scores, cross = [], []
        for h, cols in enumerate(heads):
            qh = q_ref[rows, cols]
            kh = k_ref[rows, cols]
            state = s_scr[h]
            scores.append(lax.dot_general(qh, kh, (((1,), (1,)), ((), ())),
                                          preferred_element_type=_F32))
            cross.append(_dot(qh, state.astype(_BF)))
            kz = (kh.astype(_F32) * zeta_ref[h]).astype(_BF)
            s_scr[h] = gc_ref[h] * state + lax.dot_general(
                kz, v_ref[rows, cols], (((0,), (0,)), ((), ())), preferred_element_type=_F32)
        outs = []
        for h, cols in enumerate(heads):
            p = (scores[h] * dmat_ref[h]).astype(_BF)
            outs.append(_dot(p, v_ref[rows, cols]) + cross[h] * xi_ref[h])
        for h, cols in enumerate(heads):
            o = outs[h]
            mu = jnp.mean(o, axis=-1, keepdims=True)
            oc = o - mu
            var = jnp.mean(oc * oc, axis=-1, keepdims=True)
            on = oc * lax.rsqrt(var + GN_EPS) * gng_ref[:, cols] + gnb_ref[:, cols]
            y_ref[rows, cols] = (sg_ref[rows, cols].astype(_F32) * on).astype(_BF)


def _ffn_kernel(x_ref, ys5_ref, yret_ref, lig_ref, lib_ref, wo_ref, l1g_ref, l1b_ref,
                wup_ref, wdn_ref, l2g_ref, l2b_ref, o_ref):
    bidx = pl.program_id(1)
    n_ch = D_FF // FF_CHUNK
    piece = SUB_FFN // n_ch

    def mixed_of(s):
        ys5 = jnp.concatenate(
            [ys5_ref[j, pl.ds(bidx + s * SUB_FFN * BATCH, SUB_FFN, stride=BATCH), :]
             for j in range(S5_SLABS)], axis=1).astype(_BF)
        return (_dot(ys5, wo_ref[0:S5_WIDTH, :])
                + _dot(yret_ref[s * SUB_FFN:(s + 1) * SUB_FFN, :], wo_ref[S5_WIDTH:D_MODEL, :]))

    def prep_rows(s, mixed, lo, hi):
        h = _layer_norm(x_ref[s * SUB_FFN + lo:s * SUB_FFN + hi, :], lig_ref[...], lib_ref[...])
        return _layer_norm(DEEPNORM_ALPHA * h + mixed[lo:hi], l1g_ref[...], l1b_ref[...])

    def ffn_chunk(h1b, c):
        cs = slice(c * FF_CHUNK, (c + 1) * FF_CHUNK)
        up = jnp.maximum(_dot(h1b, wup_ref[:, cs]), 0.0)
        return _dot((up * up).astype(_BF), wdn_ref[cs, :])

    def finish_rows(s, pre, lo, hi):
        o_ref[s * SUB_FFN + lo:s * SUB_FFN + hi, :] = _layer_norm(
            pre[lo:hi], l2g_ref[...], l2b_ref[...])

    h1_a = prep_rows(0, mixed_of(0), 0, SUB_FFN)
    h1b_a = h1_a.astype(_BF)
    mixed_b = mixed_of(1)
    pre_a = DEEPNORM_ALPHA * h1_a
    h1_b_parts = []
    for c in range(n_ch):
        pre_a = pre_a + ffn_chunk(h1b_a, c)
        h1_b_parts.append(prep_rows(1, mixed_b, c * piece, (c + 1) * piece))
    h1_b = jnp.concatenate(h1_b_parts, axis=0)
    h1b_b = h1_b.astype(_BF)
    pre_b = DEEPNORM_ALPHA * h1_b
    for c in range(n_ch):
        pre_b = pre_b + ffn_chunk(h1b_b, c)
        finish_rows(0, pre_a, c * piece, (c + 1) * piece)
    finish_rows(1, pre_b, 0, SUB_FFN)


def _row(v):
    return v.reshape(1, -1).astype(_F32)


@functools.lru_cache(maxsize=None)
def _position_tables():
    f32 = np.float32

    def fn(f, a):
        return f(a.astype(np.float64)).astype(f32)

    pos = np.arange(N_META + SEQ, dtype=f32)
    expo = np.arange(0, RET_HEAD_DIM, 2, dtype=f32) / f32(RET_HEAD_DIM)
    inv_freq = (f32(1.0) / fn(lambda e: np.power(ROPE_BASE, e), expo)).astype(f32)
    ang = pos[:, None] * inv_freq[None, :]
    cos, sin = fn(np.cos, ang), fn(np.sin, ang)
    cos2 = np.concatenate([cos, cos], axis=1)
    sin2 = np.concatenate([-sin, sin], axis=1)

    heads = np.arange(RET_HEADS, dtype=f32)
    log_gamma = fn(np.log1p, -fn(np.exp2, f32(-5.0) - heads))
    idx = np.arange(CHUNK, dtype=f32)
    diff = idx[:, None] - idx[None, :]
    dmat = np.where(diff[None] >= 0,
                    fn(np.exp, np.maximum(diff, f32(0.0))[None] * log_gamma[:, None, None]),
                    f32(0.0)).astype(f32)
    zeta = fn(np.exp, (f32(CHUNK - 1.0) - idx)[None] * log_gamma[:, None])
    xi = fn(np.exp, (idx + f32(1.0))[None] * log_gamma[:, None])
    gamma_chunk = fn(np.exp, f32(CHUNK) * log_gamma)
    hd = RET_HEAD_DIM
    zeta_b = np.ascontiguousarray(np.broadcast_to(zeta[:, :, None], (RET_HEADS, CHUNK, hd)))
    xi_b = np.ascontiguousarray(np.broadcast_to(xi[:, :, None], (RET_HEADS, CHUNK, hd)))
    gc_b = np.ascontiguousarray(np.broadcast_to(gamma_chunk[:, None, None], (RET_HEADS, 1, hd)))
    return cos2, sin2, dmat, zeta_b, xi_b, gc_b


def kernel(x, meta_tokens, ln_in_g, ln_in_b, w_in, s5_lambda_re, s5_lambda_im, s5_log_dt, s5_b_re, s5_b_im, s5_c_re, s5_c_im, s5_d, s5_w_glu, s5_b_glu, ret_gn_g, ret_gn_b, w_out, ln1_g, ln1_b, w_up, w_down, ln2_g, ln2_b):
    assert x.shape == (BATCH, SEQ, D_MODEL) and w_in.shape[0] == 1
    G, P, H = S5_GROUPS, S5_STATE, S5_GROUP_CH
    nrows = BATCH * SEQ
    arb2 = pltpu.CompilerParams(dimension_semantics=("arbitrary", "arbitrary"),
                                vmem_limit_bytes=VMEM_LIMIT)
    arb1 = pltpu.CompilerParams(dimension_semantics=("arbitrary",), vmem_limit_bytes=VMEM_LIMIT)

    lam_re, lam_im = s5_lambda_re[0], s5_lambda_im[0]
    dt = jnp.exp(s5_log_dt[0])[:, None]
    mag = jnp.exp(lam_re * dt)
    lbr = mag * jnp.cos(lam_im * dt)
    lbi = mag * jnp.sin(lam_im * dt)
    den = lam_re * lam_re + lam_im * lam_im
    nr = lbr - 1.0
    qr = (nr * lam_re + lbi * lam_im) / den
    qi = (lbi * lam_re - nr * lam_im) / den
    bbr = qr[..., None] * s5_b_re[0] - qi[..., None] * s5_b_im[0]
    bbi = qr[..., None] * s5_b_im[0] + qi[..., None] * s5_b_re[0]
    eye = jnp.eye(G, dtype=_F32)

    def blk_in(m):
        return (eye[:, None, :, None] * m.transpose(0, 2, 1)[:, :, None, :]).reshape(G * H, G * P)

    def blk_out(m):
        return (eye[:, None, :, None] * m.transpose(0, 2, 1)[:, :, None, :]).reshape(G * P, G * H)

    bblk = jnp.concatenate([blk_in(bbr), blk_in(bbi)], axis=1).astype(_BF)
    cblk = jnp.concatenate([blk_out(s5_c_re[0]), -blk_out(s5_c_im[0])], axis=0).astype(_BF)
    ar = lbr.reshape(1, S5_NSTATE)
    ai = lbi.reshape(1, S5_NSTATE)

    hd = RET_HEAD_DIM
    cos2, sin2, dmat, zeta_b, xi_b, gc_b = _position_tables()
    zmeta_b = zeta_b[:, CHUNK - N_META:, :]

    w_in_b = w_in[0].astype(_BF)
    w_out_b = w_out[0].astype(_BF)
    w_up_b = w_up[0].astype(_BF)
    w_dn_b = w_down[0].astype(_BF)
    lig, lib = _row(ln_in_g), _row(ln_in_b)

    s5_init, s0 = pl.pallas_call(
        _meta_kernel,
        out_shape=(jax.ShapeDtypeStruct((1, 2 * S5_NSTATE), _F32),
                   jax.ShapeDtypeStruct((RET_HEADS, hd, hd), _F32)),
        compiler_params=pltpu.CompilerParams(vmem_limit_bytes=VMEM_LIMIT),
        name="meta_prologue",
    )(meta_tokens.astype(_F32), lig, lib, w_in_b, bblk, ar, ai,
      cos2[:N_META], sin2[:N_META], zmeta_b)

    x2 = x.reshape(nrows, D_MODEL)
    n_t = SEQ // TM_PROJ
    rows_spec = lambda w: pl.BlockSpec((TM_PROJ, w), lambda i, b: (b * n_t + i, 0))
    u_tm, q, k, v, sg = pl.pallas_call(
        _in_proj_kernel,
        grid=(n_t, BATCH),
        in_specs=[rows_spec(D_MODEL), _const_spec((1, D_MODEL)), _const_spec((1, D_MODEL)),
                  _const_spec((D_MODEL, IN_PROJ_WIDTH)),
                  pl.BlockSpec((TM_PROJ, hd), lambda i, b: (i, 0)),
                  pl.BlockSpec((TM_PROJ, hd), lambda i, b: (i, 0))],
        out_specs=[pl.BlockSpec((S5_SLABS, TM_PROJ * BATCH, LANES), lambda i, b: (0, i, 0)),
                   rows_spec(RET_WIDTH), rows_spec(RET_WIDTH), rows_spec(RET_WIDTH),
                   rows_spec(RET_WIDTH)],
        out_shape=(jax.ShapeDtypeStruct((S5_SLABS, SEQ * BATCH, LANES), _F32),
                   jax.ShapeDtypeStruct((nrows, RET_WIDTH), _BF),
                   jax.ShapeDtypeStruct((nrows, RET_WIDTH), _BF),
                   jax.ShapeDtypeStruct((nrows, RET_WIDTH), _BF),
                   jax.ShapeDtypeStruct((nrows, RET_WIDTH), _BF)),
        compiler_params=arb2,
        name="in_proj",
    )(x2, lig, lib, w_in_b, cos2[N_META:], sin2[N_META:])

    rows_s5 = TT_S5 * BATCH
    ys5_tm = pl.pallas_call(
        _s5_kernel,
        grid=(SEQ // TT_S5,),
        in_specs=[pl.BlockSpec((S5_SLABS, rows_s5, LANES), lambda i: (0, i, 0)),
                  _const_spec((1, 2 * S5_NSTATE)),
                  _const_spec((S5_WIDTH, 2 * S5_NSTATE)), _const_spec((2 * S5_NSTATE, S5_WIDTH)),
                  _const_spec((1, S5_NSTATE)), _const_spec((1, S5_NSTATE)),
                  _const_spec((1, S5_WIDTH)), _const_spec((S5_WIDTH, S5_WIDTH)),
                  _const_spec((1, S5_WIDTH))],
        out_specs=pl.BlockSpec((S5_SLABS, rows_s5, LANES), lambda i: (0, i, 0)),
        out_shape=jax.ShapeDtypeStruct((S5_SLABS, SEQ * BATCH, LANES), _F32),
        scratch_shapes=[pltpu.VMEM((SB_S5 * BATCH, 2 * S5_NSTATE), _F32),
                        pltpu.VMEM((SB_S5 * BATCH, 2 * S5_NSTATE), _F32),
                        pltpu.VMEM((SB_S5 * BATCH, 2 * S5_NSTATE), _BF),
                        pltpu.VMEM((SB_S5 * BATCH, 2 * S5_NSTATE), _BF),
                        pltpu.VMEM((BATCH, 2 * S5_NSTATE), _F32)],
        compiler_params=arb1,
        name="s5_scan",
    )(u_tm, s5_init, bblk, cblk, ar, ai, _row(s5_d[0]), s5_w_glu[0].astype(_BF), _row(s5_b_glu[0]))

    n_r = SEQ // TR_RET
    ret_spec = pl.BlockSpec((TR_RET, RET_WIDTH), lambda b, c: (b * n_r + c, 0))
    tab = lambda n: _const_spec((RET_HEADS, n, hd))
    y_ret = pl.pallas_call(
        _ret_kernel,
        grid=(BATCH, n_r),
        in_specs=[ret_spec, ret_spec, ret_spec, ret_spec, tab(hd), tab(CHUNK), tab(CHUNK),
                  tab(CHUNK), tab(1), _const_spec((1, RET_WIDTH)), _const_spec((1, RET_WIDTH))],
        out_specs=ret_spec,
        out_shape=jax.ShapeDtypeStruct((nrows, RET_WIDTH), _BF),
        scratch_shapes=[pltpu.VMEM((RET_HEADS, hd, hd), _F32)],
        compiler_params=arb2,
        name="retention",
    )(q, k, v, sg, s0, dmat, xi_b, zeta_b, gc_b, _row(ret_gn_g[0]), _row(ret_gn_b[0]))

    n_f = SEQ // TM_FFN
    frow = lambda w: pl.BlockSpec((TM_FFN, w), lambda i, b: (b * n_f + i, 0))
    out = pl.pallas_call(
        _ffn_kernel,
        grid=(n_f, BATCH),
        in_specs=[frow(D_MODEL),
                  pl.BlockSpec((S5_SLABS, TM_FFN * BATCH, LANES), lambda i, b: (0, i, 0)),
                  frow(RET_WIDTH),
                  _const_spec((1, D_MODEL)), _const_spec((1, D_MODEL)),
                  _const_spec((D_MODEL, D_MODEL)),
                  _const_spec((1, D_MODEL)), _const_spec((1, D_MODEL)),
                  _const_spec((D_MODEL, D_FF)), _const_spec((D_FF, D_MODEL)),
                  _const_spec((1, D_MODEL)), _const_spec((1, D_MODEL))],
        out_specs=frow(D_MODEL),
        out_shape=jax.ShapeDtypeStruct((nrows, D_MODEL), _F32),
        compiler_params=arb2,
        name="out_ffn",
    )(x2, ys5_tm, y_ret, lig, lib, w_out_b,
      _row(ln1_g[0]), _row(ln1_b[0]), w_up_b, w_dn_b, _row(ln2_g[0]), _row(ln2_b[0]))

    return out.reshape(BATCH, SEQ, D_MODEL)
```

```python
import functools
import math

import jax
import jax.numpy as jnp
import numpy as np
from jax import lax
from jax.experimental import pallas as pl
from jax.experimental.pallas import tpu as pltpu

D_MODEL = 1024
BATCH = 8
SEQ = 4096
N_META = 16
S5_GROUP_CH = 16
S5_STATE = 64
S5_WIDTH = 256
S5_GROUPS = S5_WIDTH // S5_GROUP_CH
S5_NSTATE = S5_GROUPS * S5_STATE
RET_HEAD_DIM = 128
RET_WIDTH = 768
RET_HEADS = RET_WIDTH // RET_HEAD_DIM
CHUNK = 128
ROPE_BASE = 10000.0
D_FF = 4 * D_MODEL
LANES = 128
S5_SLABS = S5_WIDTH // LANES
LN_EPS = 1e-5
GN_EPS = 1e-5
IN_PROJ_WIDTH = S5_WIDTH + 4 * RET_WIDTH
DEEPNORM_ALPHA = 2.0 ** 0.25

_OFF_Q = S5_WIDTH
_OFF_K = _OFF_Q + RET_WIDTH
_OFF_V = _OFF_K + RET_WIDTH
_OFF_G = _OFF_V + RET_WIDTH

TM_PROJ = 512
SUB_PROJ = 256
MXU_TILE = 256
TT_S5 = 1024
SB_S5 = 32
TR_RET = 512
TM_FFN = 512
BPS_FFN = 2
SUB_FFN = 256
FF_CHUNK = 1024
VMEM_LIMIT = 56 * 1024 * 1024

_BF = jnp.bfloat16
_F32 = jnp.float32


def _const_spec(shape):
    nd = len(shape)
    return pl.BlockSpec(shape, lambda *_: (0,) * nd, pipeline_mode=pl.Buffered(1))


def _layer_norm(x, g, b):
    mu = jnp.mean(x, axis=-1, keepdims=True)
    xc = x - mu
    var = jnp.mean(xc * xc, axis=-1, keepdims=True)
    return xc * lax.rsqrt(var + LN_EPS) * g + b


def _dot(a, b):
    return jnp.dot(a, b, preferred_element_type=_F32)


def _zero_after(values):
    if not values:
        return 0.0
    tok = values[0][0:1, 0:1]
    for v in values[1:]:
        tok = tok + v[0:1, 0:1]
    bits = lax.shift_right_logical(lax.shift_right_logical(tok.astype(jnp.int32), 16), 16)
    return bits.astype(_F32)


def _rope_head(t, cos2, sin2):
    return t * cos2 + pltpu.roll(t, RET_HEAD_DIM // 2, 1) * sin2


def _meta_kernel(meta_ref, g_ref, b_ref, w_ref, bblk_ref, ar_ref, ai_ref, cos_ref, sin_ref,
                 zmeta_ref, s5_ref, s0_ref):
    hm = _layer_norm(meta_ref[...], g_ref[...], b_ref[...]).astype(_BF)
    u = _dot(hm, w_ref[:, 0:S5_WIDTH])
    bu = _dot(u.astype(_BF), bblk_ref[...])
    ar = ar_ref[...]
    ai = ai_ref[...]
    xr = jnp.zeros((1, S5_NSTATE), _F32)
    xi = jnp.zeros((1, S5_NSTATE), _F32)
    for t in range(N_META):
        br = bu[t:t + 1, 0:S5_NSTATE]
        bi = bu[t:t + 1, S5_NSTATE:2 * S5_NSTATE]
        xr, xi = ar * xr - ai * xi + br, ar * xi + ai * xr + bi
    s5_ref[:, 0:S5_NSTATE] = xr
    s5_ref[:, S5_NSTATE:2 * S5_NSTATE] = xi

    k = _dot(hm, w_ref[:, _OFF_K:_OFF_V])
    v = _dot(hm, w_ref[:, _OFF_V:_OFF_G]).astype(_BF)
    cos2 = cos_ref[...]
    sin2 = sin_ref[...]
    for h in range(RET_HEADS):
        sl = slice(h * RET_HEAD_DIM, (h + 1) * RET_HEAD_DIM)
        kh = _rope_head(k[:, sl], cos2, sin2) * (RET_HEAD_DIM ** -0.5)
        kz = (kh * zmeta_ref[h]).astype(_BF)
        s0_ref[h] = lax.dot_general(kz, v[:, sl], (((0,), (0,)), ((), ())),
                                    preferred_element_type=_F32)


def _in_proj_kernel(x_ref, g_ref, b_ref, w_ref, cos_ref, sin_ref,
                    u_ref, q_ref, k_ref, v_ref, sg_ref):
    bidx = pl.program_id(1)

    def normed(s):
        rows = slice(s * SUB_PROJ, (s + 1) * SUB_PROJ)
        return _layer_norm(x_ref[rows, :], g_ref[...], b_ref[...]).astype(_BF)

    def project(s, hn, hn_next):
        rows = slice(s * SUB_PROJ, (s + 1) * SUB_PROJ)
        cos2 = cos_ref[rows, :]
        sin2 = sin_ref[rows, :]
        u = _dot(hn, w_ref[:, 0:S5_WIDTH])
        for j in range(S5_SLABS):
            u_ref[j, pl.ds(bidx + s * SUB_PROJ * BATCH, SUB_PROJ, stride=BATCH), :] = (
                u[:, j * LANES:(j + 1) * LANES])
        q = _dot(hn, w_ref[:, _OFF_Q:_OFF_K])
        nxt = None if hn_next is None else hn_next()
        for h in range(RET_HEADS):
            sl = slice(h * RET_HEAD_DIM, (h + 1) * RET_HEAD_DIM)
            q_ref[rows, sl] = _rope_head(q[:, sl], cos2, sin2).astype(_BF)
        k = _dot(hn, w_ref[:, _OFF_K:_OFF_V])
        for h in range(RET_HEADS):
            sl = slice(h * RET_HEAD_DIM, (h + 1) * RET_HEAD_DIM)
            k_ref[rows, sl] = (_rope_head(k[:, sl], cos2, sin2)
                               * (RET_HEAD_DIM ** -0.5)).astype(_BF)
        g = _dot(hn, w_ref[:, _OFF_G:IN_PROJ_WIDTH])
        sg_ref[rows, :] = (g * jax.nn.sigmoid(g)).astype(_BF)
        v_ref[rows, :] = _dot(hn, w_ref[:, _OFF_V:_OFF_G]).astype(_BF)
        return nxt

    n_sub = TM_PROJ // SUB_PROJ
    hn = normed(0)
    for s in range(n_sub):
        nxt = (lambda s=s: normed(s + 1)) if s + 1 < n_sub else None
        hn = project(s, hn, nxt)


def _s5_kernel(u_ref, init_ref, bblk_ref, cblk_ref, ar_ref, ai_ref, d_ref, wglu_ref, bglu_ref,
               y_ref, bu0_scr, bu1_scr, xb0_scr, xb1_scr, st_scr):
    @pl.when(pl.program_id(0) == 0)
    def _():
        st_scr[...] = jnp.broadcast_to(init_ref[...], (BATCH, 2 * S5_NSTATE))

    rows_sb = SB_S5 * BATCH
    n_sb = TT_S5 // SB_S5
    n_piece = 2 * S5_NSTATE // MXU_TILE
    steps_piece = SB_S5 // n_piece
    re = slice(0, S5_NSTATE)
    im = slice(S5_NSTATE, 2 * S5_NSTATE)
    ar = jnp.broadcast_to(ar_ref[...], (BATCH, S5_NSTATE))
    ai = jnp.broadcast_to(ai_ref[...], (BATCH, S5_NSTATE))

    def rows_of(j):
        if isinstance(j, int):
            return pl.ds(j * rows_sb, rows_sb)
        return pl.ds(pl.multiple_of(j * rows_sb, rows_sb), rows_sb)

    def load_u(j):
        return jnp.concatenate([u_ref[s, rows_of(j), :] for s in range(S5_SLABS)], axis=1)

    bu_scr = (bu0_scr, bu1_scr)
    xb_scr = (xb0_scr, xb1_scr)

    def a_piece(ub, slot, c):
        cols = slice(c * MXU_TILE, (c + 1) * MXU_TILE)
        bu_scr[slot][:, cols] = _dot(ub, bblk_ref[:, cols])

    def b_piece(slot, c, xr, xi):
        for t in range(c * steps_piece, (c + 1) * steps_piece, 2):
            out_r, out_i = [], []
            for tt in (t, t + 1):
                rows = slice(tt * BATCH, (tt + 1) * BATCH)
                br = bu_scr[slot][rows, re]
                bi = bu_scr[slot][rows, im]
                xr, xi = ar * xr - ai * xi + br, ar * xi + ai * xr + bi
                out_r.append(xr)
                out_i.append(xi)
            rows2 = slice(t * BATCH, (t + 2) * BATCH)
            xb_scr[slot][rows2, re] = jnp.concatenate(out_r, axis=0).astype(_BF)
            xb_scr[slot][rows2, im] = jnp.concatenate(out_i, axis=0).astype(_BF)
        return xr, xi

    def c_piece(slot, c, acc):
        cols = slice(c * MXU_TILE, (c + 1) * MXU_TILE)
        part = _dot(xb_scr[slot][:, cols], cblk_ref[cols, :])
        return part if acc is None else acc + part

    def c_finish(j, acc):
        y = jax.nn.gelu(acc + d_ref[...] * load_u(j))
        gate = jax.nn.sigmoid(_dot(y.astype(_BF), wglu_ref[...]) + bglu_ref[...])
        y = y * gate
        for s in range(S5_SLABS):
            y_ref[s, rows_of(j), :] = y[:, s * LANES:(s + 1) * LANES]

    def iteration(j, slot, xr, xi, do_a, do_c):
        other = 1 - slot
        ub = load_u(j + 1).astype(_BF) if do_a else None
        acc = None
        for c in range(n_piece):
            if do_a:
                a_piece(ub, other, c)
            xr, xi = b_piece(slot, c, xr, xi)
            if do_c:
                acc = c_piece(other, c, acc)
        if do_c:
            c_finish(j - 1, acc)
        return xr, xi

    ub0 = load_u(0).astype(_BF)
    for c in range(n_piece):
        a_piece(ub0, 0, c)
    xr, xi = iteration(0, 0, st_scr[:, re], st_scr[:, im], True, False)

    def pair(p, carry):
        j = 2 * p + 1
        xr, xi = iteration(j, 1, carry[0], carry[1], True, True)
        return iteration(j + 1, 0, xr, xi, True, True)

    xr, xi = lax.fori_loop(0, (n_sb - 2) // 2, pair, (xr, xi))
    xr, xi = iteration(n_sb - 1, 1, xr, xi, False, True)
    st_scr[:, re] = xr
    st_scr[:, im] = xi
    acc = None
    for c in range(n_piece):
        acc = c_piece(1, c, acc)
    c_finish(n_sb - 1, acc)


def _ret_kernel(q_ref, k_ref, v_ref, sg_ref, s0_ref, dmat_ref, xi_ref, zeta_ref, gc_ref,
                gng_ref, gnb_ref, y_ref, s_scr):
    @pl.when(pl.program_id(1) == 0)
    def _():
        s_scr[...] = s0_ref[...]

    heads = [slice(h * RET_HEAD_DIM, (h + 1) * RET_HEAD_DIM) for h in range(RET_HEADS)]
    for c in range(TR_RET // CHUNK):
        rows = slice(c * CHUNK, (c + 1) * CHUNK)
        scores, cross = [], []
        for h, cols in enumerate(heads):
            qh = q_ref[rows, cols]
            kh = k_ref[rows, cols]
            state = s_scr[h]
            scores.append(lax.dot_general(qh, kh, (((1,), (1,)), ((), ())),
                                          preferred_element_type=_F32))
            cross.append(_dot(qh, state.astype(_BF)))
            kz = (kh.astype(_F32) * zeta_ref[h]).astype(_BF)
            s_scr[h] = gc_ref[h] * state + lax.dot_general(
                kz, v_ref[rows, cols], (((0,), (0,)), ((), ())), preferred_element_type=_F32)
        outs = []
        for h, cols in enumerate(heads):
            p = (scores[h] * dmat_ref[h]).astype(_BF)
            outs.append(_dot(p, v_ref[rows, cols]) + cross[h] * xi_ref[h])
        for h, cols in enumerate(heads):
            o = outs[h]
            mu = jnp.mean(o, axis=-1, keepdims=True)
            oc = o - mu
            var = jnp.mean(oc * oc, axis=-1, keepdims=True)
            on = oc * lax.rsqrt(var + GN_EPS) * gng_ref[:, cols] + gnb_ref[:, cols]
            y_ref[rows, cols] = (sg_ref[rows, cols].astype(_F32) * on).astype(_BF)


def _ffn_kernel(x_ref, ys5_ref, yret_ref, lig_ref, lib_ref, wo_ref, l1g_ref, l1b_ref,
                wup_ref, wdn_ref, l2g_ref, l2b_ref, o_ref):
    pair = pl.program_id(1)
    n_ch = D_FF // FF_CHUNK
    piece = SUB_FFN // (2 * n_ch)
    halves = TM_FFN // SUB_FFN
    n_sub = BPS_FFN * halves

    def where(t):
        return t // halves, (t % halves) * SUB_FFN

    def mixed_of(t):
        bb, r0 = where(t)
        start = pair * BPS_FFN + bb + r0 * BATCH
        ys5 = jnp.concatenate(
            [ys5_ref[j, pl.ds(start, SUB_FFN, stride=BATCH), :] for j in range(S5_SLABS)],
            axis=1).astype(_BF)
        return (_dot(ys5, wo_ref[0:S5_WIDTH, :])
                + _dot(yret_ref[bb, r0:r0 + SUB_FFN, :], wo_ref[S5_WIDTH:D_MODEL, :]))

    def prep_rows(t, mixed, lo, hi):
        bb, r0 = where(t)
        h = _layer_norm(x_ref[bb, r0 + lo:r0 + hi, :], lig_ref[...], lib_ref[...])
        return _layer_norm(DEEPNORM_ALPHA * h + mixed[lo:hi], l1g_ref[...], l1b_ref[...])

    def ffn_up(h1b, c, floor):
        up = jnp.maximum(_dot(h1b, wup_ref[:, c * FF_CHUNK:(c + 1) * FF_CHUNK]), floor)
        return (up * up).astype(_BF)

    def ffn_down(act, c):
        return _dot(act, wdn_ref[c * FF_CHUNK:(c + 1) * FF_CHUNK, :])

    def finish_rows(t, pre, lo, hi):
        bb, r0 = where(t)
        out = _layer_norm(pre[lo:hi], l2g_ref[...], l2b_ref[...])
        o_ref[bb, r0 + lo:r0 + hi, :] = out
        return out

    h1 = prep_rows(0, mixed_of(0), 0, SUB_FFN)
    pre_prev = None
    floor = 0.0
    for t in range(n_sub):
        h1b = h1.astype(_BF)
        pre = DEEPNORM_ALPHA * h1
        mixed_next = mixed_of(t + 1) if t + 1 < n_sub else None
        next_parts = []

        def side_work(k):
            lo, hi = k * piece, (k + 1) * piece
            done = []
            if mixed_next is not None:
                next_parts.append(prep_rows(t + 1, mixed_next, lo, hi))
                done.append(next_parts[-1])
            if pre_prev is not None:
                done.append(finish_rows(t - 1, pre_prev, lo, hi))
            return done

        for c in range(n_ch):
            act = ffn_up(h1b, c, floor)
            done = side_work(2 * c)
            pre = pre + ffn_down(act, c)
            done += side_work(2 * c + 1)
            floor = _zero_after(done)
        if next_parts:
            h1 = jnp.concatenate(next_parts, axis=0)
        pre_prev = pre
    finish_rows(n_sub - 1, pre_prev, 0, SUB_FFN)


def _row(v):
    return v.reshape(1, -1).astype(_F32)


@functools.lru_cache(maxsize=None)
def _position_tables():
    f32 = np.float32

    def fn(f, a):
        return f(a.astype(np.float64)).astype(f32)

    pos = np.arange(N_META + SEQ, dtype=f32)
    expo = np.arange(0, RET_HEAD_DIM, 2, dtype=f32) / f32(RET_HEAD_DIM)
    inv_freq = (f32(1.0) / fn(lambda e: np.power(ROPE_BASE, e), expo)).astype(f32)
    ang = pos[:, None] * inv_freq[None, :]
    cos, sin = fn(np.cos, ang), fn(np.sin, ang)
    cos2 = np.concatenate([cos, cos], axis=1)
    sin2 = np.concatenate([-sin, sin], axis=1)

    heads = np.arange(RET_HEADS, dtype=f32)
    log_gamma = fn(np.log1p, -fn(np.exp2, f32(-5.0) - heads))
    idx = np.arange(CHUNK, dtype=f32)
    diff = idx[:, None] - idx[None, :]
    dmat = np.where(diff[None] >= 0,
                    fn(np.exp, np.maximum(diff, f32(0.0))[None] * log_gamma[:, None, None]),
                    f32(0.0)).astype(f32)
    zeta = fn(np.exp, (f32(CHUNK - 1.0) - idx)[None] * log_gamma[:, None])
    xi = fn(np.exp, (idx + f32(1.0))[None] * log_gamma[:, None])
    gamma_chunk = fn(np.exp, f32(CHUNK) * log_gamma)
    hd = RET_HEAD_DIM
    zeta_b = np.ascontiguousarray(np.broadcast_to(zeta[:, :, None], (RET_HEADS, CHUNK, hd)))
    xi_b = np.ascontiguousarray(np.broadcast_to(xi[:, :, None], (RET_HEADS, CHUNK, hd)))
    gc_b = np.ascontiguousarray(np.broadcast_to(gamma_chunk[:, None, None], (RET_HEADS, 1, hd)))
    return cos2, sin2, dmat, zeta_b, xi_b, gc_b


def kernel(x, meta_tokens, ln_in_g, ln_in_b, w_in, s5_lambda_re, s5_lambda_im, s5_log_dt, s5_b_re, s5_b_im, s5_c_re, s5_c_im, s5_d, s5_w_glu, s5_b_glu, ret_gn_g, ret_gn_b, w_out, ln1_g, ln1_b, w_up, w_down, ln2_g, ln2_b):
    assert x.shape == (BATCH, SEQ, D_MODEL) and w_in.shape[0] == 1
    G, P, H = S5_GROUPS, S5_STATE, S5_GROUP_CH
    nrows = BATCH * SEQ
    arb2 = pltpu.CompilerParams(dimension_semantics=("arbitrary", "arbitrary"),
                                vmem_limit_bytes=VMEM_LIMIT)
    arb1 = pltpu.CompilerParams(dimension_semantics=("arbitrary",), vmem_limit_bytes=VMEM_LIMIT)

    lam_re, lam_im = s5_lambda_re[0], s5_lambda_im[0]
    dt = jnp.exp(s5_log_dt[0])[:, None]
    mag = jnp.exp(lam_re * dt)
    lbr = mag * jnp.cos(lam_im * dt)
    lbi = mag * jnp.sin(lam_im * dt)
    den = lam_re * lam_re + lam_im * lam_im
    nr = lbr - 1.0
    qr = (nr * lam_re + lbi * lam_im) / den
    qi = (lbi * lam_re - nr * lam_im) / den
    bbr = qr[..., None] * s5_b_re[0] - qi[..., None] * s5_b_im[0]
    bbi = qr[..., None] * s5_b_im[0] + qi[..., None] * s5_b_re[0]
    eye = jnp.eye(G, dtype=_F32)

    def blk_in(m):
        return (eye[:, None, :, None] * m.transpose(0, 2, 1)[:, :, None, :]).reshape(G * H, G * P)

    def blk_out(m):
        return (eye[:, None, :, None] * m.transpose(0, 2, 1)[:, :, None, :]).reshape(G * P, G * H)

    bblk = jnp.concatenate([blk_in(bbr), blk_in(bbi)], axis=1).astype(_BF)
    cblk = jnp.concatenate([blk_out(s5_c_re[0]), -blk_out(s5_c_im[0])], axis=0).astype(_BF)
    ar = lbr.reshape(1, S5_NSTATE)
    ai = lbi.reshape(1, S5_NSTATE)

    hd = RET_HEAD_DIM
    cos2, sin2, dmat, zeta_b, xi_b, gc_b = _position_tables()
    zmeta_b = zeta_b[:, CHUNK - N_META:, :]

    w_in_b = w_in[0].astype(_BF)
    w_out_b = w_out[0].astype(_BF)
    w_up_b = w_up[0].astype(_BF)
    w_dn_b = w_down[0].astype(_BF)
    lig, lib = _row(ln_in_g), _row(ln_in_b)

    s5_init, s0 = pl.pallas_call(
        _meta_kernel,
        out_shape=(jax.ShapeDtypeStruct((1, 2 * S5_NSTATE), _F32),
                   jax.ShapeDtypeStruct((RET_HEADS, hd, hd), _F32)),
        compiler_params=pltpu.CompilerParams(vmem_limit_bytes=VMEM_LIMIT),
        name="meta_prologue",
    )(meta_tokens.astype(_F32), lig, lib, w_in_b, bblk, ar, ai,
      cos2[:N_META], sin2[:N_META], zmeta_b)

    x2 = x.reshape(nrows, D_MODEL)
    n_t = SEQ // TM_PROJ
    rows_spec = lambda w: pl.BlockSpec((TM_PROJ, w), lambda i, b: (b * n_t + i, 0))
    u_tm, q, k, v, sg = pl.pallas_call(
        _in_proj_kernel,
        grid=(n_t, BATCH),
        in_specs=[rows_spec(D_MODEL), _const_spec((1, D_MODEL)), _const_spec((1, D_MODEL)),
                  _const_spec((D_MODEL, IN_PROJ_WIDTH)),
                  pl.BlockSpec((TM_PROJ, hd), lambda i, b: (i, 0)),
                  pl.BlockSpec((TM_PROJ, hd), lambda i, b: (i, 0))],
        out_specs=[pl.BlockSpec((S5_SLABS, TM_PROJ * BATCH, LANES), lambda i, b: (0, i, 0)),
                   rows_spec(RET_WIDTH), rows_spec(RET_WIDTH), rows_spec(RET_WIDTH),
                   rows_spec(RET_WIDTH)],
        out_shape=(jax.ShapeDtypeStruct((S5_SLABS, SEQ * BATCH, LANES), _F32),
                   jax.ShapeDtypeStruct((nrows, RET_WIDTH), _BF),
                   jax.ShapeDtypeStruct((nrows, RET_WIDTH), _BF),
                   jax.ShapeDtypeStruct((nrows, RET_WIDTH), _BF),
                   jax.ShapeDtypeStruct((nrows, RET_WIDTH), _BF)),
        compiler_params=arb2,
        name="in_proj",
    )(x2, lig, lib, w_in_b, cos2[N_META:], sin2[N_META:])

    rows_s5 = TT_S5 * BATCH
    ys5_tm = pl.pallas_call(
        _s5_kernel,
        grid=(SEQ // TT_S5,),
        in_specs=[pl.BlockSpec((S5_SLABS, rows_s5, LANES), lambda i: (0, i, 0)),
                  _const_spec((1, 2 * S5_NSTATE)),
                  _const_spec((S5_WIDTH, 2 * S5_NSTATE)), _const_spec((2 * S5_NSTATE, S5_WIDTH)),
                  _const_spec((1, S5_NSTATE)), _const_spec((1, S5_NSTATE)),
                  _const_spec((1, S5_WIDTH)), _const_spec((S5_WIDTH, S5_WIDTH)),
                  _const_spec((1, S5_WIDTH))],
        out_specs=pl.BlockSpec((S5_SLABS, rows_s5, LANES), lambda i: (0, i, 0)),
        out_shape=jax.ShapeDtypeStruct((S5_SLABS, SEQ * BATCH, LANES), _F32),
        scratch_shapes=[pltpu.VMEM((SB_S5 * BATCH, 2 * S5_NSTATE), _F32),
                        pltpu.VMEM((SB_S5 * BATCH, 2 * S5_NSTATE), _F32),
                        pltpu.VMEM((SB_S5 * BATCH, 2 * S5_NSTATE), _BF),
                        pltpu.VMEM((SB_S5 * BATCH, 2 * S5_NSTATE), _BF),
                        pltpu.VMEM((BATCH, 2 * S5_NSTATE), _F32)],
        compiler_params=arb1,
        name="s5_scan",
    )(u_tm, s5_init, bblk, cblk, ar, ai, _row(s5_d[0]), s5_w_glu[0].astype(_BF), _row(s5_b_glu[0]))

    n_r = SEQ // TR_RET
    ret_spec = pl.BlockSpec((TR_RET, RET_WIDTH), lambda b, c: (b * n_r + c, 0))
    tab = lambda n: _const_spec((RET_HEADS, n, hd))
    y_ret = pl.pallas_call(
        _ret_kernel,
        grid=(BATCH, n_r),
        in_specs=[ret_spec, ret_spec, ret_spec, ret_spec, tab(hd), tab(CHUNK), tab(CHUNK),
                  tab(CHUNK), tab(1), _const_spec((1, RET_WIDTH)), _const_spec((1, RET_WIDTH))],
        out_specs=ret_spec,
        out_shape=jax.ShapeDtypeStruct((nrows, RET_WIDTH), _BF),
        scratch_shapes=[pltpu.VMEM((RET_HEADS, hd, hd), _F32)],
        compiler_params=arb2,
        name="retention",
    )(q, k, v, sg, s0, dmat, xi_b, zeta_b, gc_b, _row(ret_gn_g[0]), _row(ret_gn_b[0]))

    n_f = SEQ // TM_FFN
    frow = lambda w: pl.BlockSpec((BPS_FFN, TM_FFN, w), lambda i, p: (p, i, 0))
    out = pl.pallas_call(
        _ffn_kernel,
        grid=(n_f, BATCH // BPS_FFN),
        in_specs=[frow(D_MODEL),
                  pl.BlockSpec((S5_SLABS, TM_FFN * BATCH, LANES), lambda i, p: (0, i, 0),
                               pipeline_mode=pl.Buffered(1)),
                  frow(RET_WIDTH),
                  _const_spec((1, D_MODEL)), _const_spec((1, D_MODEL)),
                  _const_spec((D_MODEL, D_MODEL)),
                  _const_spec((1, D_MODEL)), _const_spec((1, D_MODEL)),
                  _const_spec((D_MODEL, D_FF)), _const_spec((D_FF, D_MODEL)),
                  _const_spec((1, D_MODEL)), _const_spec((1, D_MODEL))],
        out_specs=frow(D_MODEL),
        out_shape=jax.ShapeDtypeStruct((BATCH, SEQ, D_MODEL), _F32),
        compiler_params=arb2,
        name="out_ffn",
    )(x, ys5_tm, y_ret.reshape(BATCH, SEQ, RET_WIDTH), lig, lib, w_out_b,
      _row(ln1_g[0]), _row(ln1_b[0]), w_up_b, w_dn_b, _row(ln2_g[0]), _row(ln2_b[0]))

    return out
```

```python
import functools
import math

import jax
import jax.numpy as jnp
import numpy as np
from jax import lax
from jax.experimental import pallas as pl
from jax.experimental.pallas import tpu as pltpu

D_MODEL = 1024
BATCH = 8
SEQ = 4096
N_META = 16
S5_GROUP_CH = 16
S5_STATE = 64
S5_WIDTH = 256
S5_GROUPS = S5_WIDTH // S5_GROUP_CH
S5_NSTATE = S5_GROUPS * S5_STATE
RET_HEAD_DIM = 128
RET_WIDTH = 768
RET_HEADS = RET_WIDTH // RET_HEAD_DIM
CHUNK = 128
ROPE_BASE = 10000.0
D_FF = 4 * D_MODEL
LANES = 128
S5_SLABS = S5_WIDTH // LANES
LN_EPS = 1e-5
GN_EPS = 1e-5
IN_PROJ_WIDTH = S5_WIDTH + 4 * RET_WIDTH
DEEPNORM_ALPHA = 2.0 ** 0.25

_OFF_Q = S5_WIDTH
_OFF_K = _OFF_Q + RET_WIDTH
_OFF_V = _OFF_K + RET_WIDTH
_OFF_G = _OFF_V + RET_WIDTH

TM_PROJ = 512
SUB_PROJ = 256
MXU_TILE = 256
TT_S5 = 1024
SB_S5 = 32
TR_RET = 512
TM_FFN = 512
BPS_FFN = 2
SUB_FFN = 256
FF_CHUNK = 1024
VMEM_LIMIT = 56 * 1024 * 1024

_BF = jnp.bfloat16
_F32 = jnp.float32


def _const_spec(shape):
    nd = len(shape)
    return pl.BlockSpec(shape, lambda *_: (0,) * nd, pipeline_mode=pl.Buffered(1))


def _layer_norm(x, g, b):
    mu = jnp.mean(x, axis=-1, keepdims=True)
    xc = x - mu
    var = jnp.mean(xc * xc, axis=-1, keepdims=True)
    return xc * lax.rsqrt(var + LN_EPS) * g + b


def _dot(a, b):
    return jnp.dot(a, b, preferred_element_type=_F32)


def _zero_after(values):
    if not values:
        return 0.0
    tok = values[0][0:1, 0:1]
    for v in values[1:]:
        tok = tok + v[0:1, 0:1]
    bits = lax.shift_right_logical(lax.shift_right_logical(tok.astype(jnp.int32), 16), 16)
    return bits.astype(_F32)


def _rope_head(t, cos2, sin2):
    return t * cos2 + pltpu.roll(t, RET_HEAD_DIM // 2, 1) * sin2


def _meta_kernel(meta_ref, g_ref, b_ref, w_ref, bblk_ref, ar_ref, ai_ref, cos_ref, sin_ref,
                 zmeta_ref, s5_ref, s0_ref):
    hm = _layer_norm(meta_ref[...], g_ref[...], b_ref[...]).astype(_BF)
    u = _dot(hm, w_ref[:, 0:S5_WIDTH])
    bu = _dot(u.astype(_BF), bblk_ref[...])
    ar = ar_ref[...]
    ai = ai_ref[...]
    xr = jnp.zeros((1, S5_NSTATE), _F32)
    xi = jnp.zeros((1, S5_NSTATE), _F32)
    for t in range(N_META):
        br = bu[t:t + 1, 0:S5_NSTATE]
        bi = bu[t:t + 1, S5_NSTATE:2 * S5_NSTATE]
        xr, xi = ar * xr - ai * xi + br, ar * xi + ai * xr + bi
    s5_ref[:, 0:S5_NSTATE] = xr
    s5_ref[:, S5_NSTATE:2 * S5_NSTATE] = xi

    k = _dot(hm, w_ref[:, _OFF_K:_OFF_V])
    v = _dot(hm, w_ref[:, _OFF_V:_OFF_G]).astype(_BF)
    cos2 = cos_ref[...]
    sin2 = sin_ref[...]
    for h in range(RET_HEADS):
        sl = slice(h * RET_HEAD_DIM, (h + 1) * RET_HEAD_DIM)
        kh = _rope_head(k[:, sl], cos2, sin2) * (RET_HEAD_DIM ** -0.5)
        kz = (kh * zmeta_ref[h]).astype(_BF)
        s0_ref[h] = lax.dot_general(kz, v[:, sl], (((0,), (0,)), ((), ())),
                                    preferred_element_type=_F32)


def _in_proj_kernel(x_ref, g_ref, b_ref, w_ref, cos_ref, sin_ref,
                    u_ref, q_ref, k_ref, v_ref, sg_ref):
    bidx = pl.program_id(1)

    def normed(s):
        rows = slice(s * SUB_PROJ, (s + 1) * SUB_PROJ)
        return _layer_norm(x_ref[rows, :], g_ref[...], b_ref[...]).astype(_BF)

    def project(s, hn, hn_next):
        rows = slice(s * SUB_PROJ, (s + 1) * SUB_PROJ)
        cos2 = cos_ref[rows, :]
        sin2 = sin_ref[rows, :]
        u = _dot(hn, w_ref[:, 0:S5_WIDTH])
        for j in range(S5_SLABS):
            u_ref[j, pl.ds(bidx + s * SUB_PROJ * BATCH, SUB_PROJ, stride=BATCH), :] = (
                u[:, j * LANES:(j + 1) * LANES])
        q = _dot(hn, w_ref[:, _OFF_Q:_OFF_K])
        nxt = None if hn_next is None else hn_next()
        for h in range(RET_HEADS):
            sl = slice(h * RET_HEAD_DIM, (h + 1) * RET_HEAD_DIM)
            q_ref[rows, sl] = _rope_head(q[:, sl], cos2, sin2).astype(_BF)
        k = _dot(hn, w_ref[:, _OFF_K:_OFF_V])
        for h in range(RET_HEADS):
            sl = slice(h * RET_HEAD_DIM, (h + 1) * RET_HEAD_DIM)
            k_ref[rows, sl] = (_rope_head(k[:, sl], cos2, sin2)
                               * (RET_HEAD_DIM ** -0.5)).astype(_BF)
        g = _dot(hn, w_ref[:, _OFF_G:IN_PROJ_WIDTH])
        sg_ref[rows, :] = (g * jax.nn.sigmoid(g)).astype(_BF)
        v_ref[rows, :] = _dot(hn, w_ref[:, _OFF_V:_OFF_G]).astype(_BF)
        return nxt

    n_sub = TM_PROJ // SUB_PROJ
    hn = normed(0)
    for s in range(n_sub):
        nxt = (lambda s=s: normed(s + 1)) if s + 1 < n_sub else None
        hn = project(s, hn, nxt)


def _s5_kernel(u_ref, init_ref, bblk_ref, cblk_ref, ar_ref, ai_ref, d_ref, wglu_ref, bglu_ref,
               y_ref, bu0_scr, bu1_scr, xb0_scr, xb1_scr, st_scr):
    @pl.when(pl.program_id(0) == 0)
    def _():
        st_scr[...] = jnp.broadcast_to(init_ref[...], (BATCH, 2 * S5_NSTATE))

    rows_sb = SB_S5 * BATCH
    n_sb = TT_S5 // SB_S5
    n_piece = 2 * S5_NSTATE // MXU_TILE
    steps_piece = SB_S5 // n_piece
    re = slice(0, S5_NSTATE)
    im = slice(S5_NSTATE, 2 * S5_NSTATE)
    ar = jnp.broadcast_to(ar_ref[...], (BATCH, S5_NSTATE))
    ai = jnp.broadcast_to(ai_ref[...], (BATCH, S5_NSTATE))

    def rows_of(j):
        if isinstance(j, int):
            return pl.ds(j * rows_sb, rows_sb)
        return pl.ds(pl.multiple_of(j * rows_sb, rows_sb), rows_sb)

    def load_u(j):
        return jnp.concatenate([u_ref[s, rows_of(j), :] for s in range(S5_SLABS)], axis=1)

    bu_scr = (bu0_scr, bu1_scr)
    xb_scr = (xb0_scr, xb1_scr)

    def a_piece(ub, slot, c):
        cols = slice(c * MXU_TILE, (c + 1) * MXU_TILE)
        bu_scr[slot][:, cols] = _dot(ub, bblk_ref[:, cols])

    def b_piece(slot, c, xr, xi):
        for t in range(c * steps_piece, (c + 1) * steps_piece, 2):
            out_r, out_i = [], []
            for tt in (t, t + 1):
                rows = slice(tt * BATCH, (tt + 1) * BATCH)
                br = bu_scr[slot][rows, re]
                bi = bu_scr[slot][rows, im]
                xr, xi = ar * xr - ai * xi + br, ar * xi + ai * xr + bi
                out_r.append(xr)
                out_i.append(xi)
            rows2 = slice(t * BATCH, (t + 2) * BATCH)
            xb_scr[slot][rows2, re] = jnp.concatenate(out_r, axis=0).astype(_BF)
            xb_scr[slot][rows2, im] = jnp.concatenate(out_i, axis=0).astype(_BF)
        return xr, xi

    def c_piece(slot, c, acc):
        cols = slice(c * MXU_TILE, (c + 1) * MXU_TILE)
        part = _dot(xb_scr[slot][:, cols], cblk_ref[cols, :])
        return part if acc is None else acc + part

    def c_finish(j, acc):
        y = jax.nn.gelu(acc + d_ref[...] * load_u(j))
        gate = jax.nn.sigmoid(_dot(y.astype(_BF), wglu_ref[...]) + bglu_ref[...])
        y = y * gate
        for s in range(S5_SLABS):
            y_ref[s, rows_of(j), :] = y[:, s * LANES:(s + 1) * LANES]

    def iteration(j, slot, xr, xi, do_a, do_c):
        other = 1 - slot
        ub = load_u(j + 1).astype(_BF) if do_a else None
        acc = None
        for c in range(n_piece):
            if do_a:
                a_piece(ub, other, c)
            xr, xi = b_piece(slot, c, xr, xi)
            if do_c:
                acc = c_piece(other, c, acc)
        if do_c:
            c_finish(j - 1, acc)
        return xr, xi

    ub0 = load_u(0).astype(_BF)
    for c in range(n_piece):
        a_piece(ub0, 0, c)
    xr, xi = iteration(0, 0, st_scr[:, re], st_scr[:, im], True, False)

    def pair(p, carry):
        j = 2 * p + 1
        xr, xi = iteration(j, 1, carry[0], carry[1], True, True)
        return iteration(j + 1, 0, xr, xi, True, True)

    xr, xi = lax.fori_loop(0, (n_sb - 2) // 2, pair, (xr, xi))
    xr, xi = iteration(n_sb - 1, 1, xr, xi, False, True)
    st_scr[:, re] = xr
    st_scr[:, im] = xi
    acc = None
    for c in range(n_piece):
        acc = c_piece(1, c, acc)
    c_finish(n_sb - 1, acc)


def _ret_kernel(q_ref, k_ref, v_ref, sg_ref, s0_ref, dmat_ref, xi_ref, zeta_ref, gc_ref,
                gng_ref, gnb_ref, y_ref, s_scr):
    @pl.when(pl.program_id(1) == 0)
    def _():
        s_scr[...] = s0_ref[...]

    heads = [slice(h * RET_HEAD_DIM, (h + 1) * RET_HEAD_DIM) for h in range(RET_HEADS)]
    for c in range(TR_RET // CHUNK):
        rows = slice(c * CHUNK, (c + 1) * CHUNK)
        scores, cross = [], []
        for h, cols in enumerate(heads):
            qh = q_ref[rows, cols]
            kh = k_ref[rows, cols]
            state = s_scr[h]
            scores.append(lax.dot_general(qh, kh, (((1,), (1,)), ((), ())),
                                          preferred_element_type=_F32))
            cross.append(_dot(qh, state.astype(_BF)))
            kz = (kh.astype(_F32) * zeta_ref[h]).astype(_BF)
            s_scr[h] = gc_ref[h] * state + lax.dot_general(
                kz, v_ref[rows, cols], (((0,), (0,)), ((), ())), preferred_element_type=_F32)
        outs = []
        for h, cols in enumerate(heads):
            p = (scores[h] * dmat_ref[h]).astype(_BF)
            outs.append(_dot(p, v_ref[rows, cols]) + cross[h] * xi_ref[h])
        for h, cols in enumerate(heads):
            o = outs[h]
            mu = jnp.mean(o, axis=-1, keepdims=True)
            oc = o - mu
            var = jnp.mean(oc * oc, axis=-1, keepdims=True)
            on = oc * lax.rsqrt(var + GN_EPS) * gng_ref[:, cols] + gnb_ref[:, cols]
            y_ref[rows, cols] = (sg_ref[rows, cols].astype(_F32) * on).astype(_BF)


def _ffn_kernel(x_ref, ys5_ref, yret_ref, lig_ref, lib_ref, wo_ref, l1g_ref, l1b_ref,
                wup_ref, wdn_ref, l2g_ref, l2b_ref, o_ref):
    pair = pl.program_id(1)
    n_ch = D_FF // FF_CHUNK
    piece = SUB_FFN // (2 * n_ch)
    halves = TM_FFN // SUB_FFN
    n_sub = BPS_FFN * halves

    def where(t):
        return t // halves, (t % halves) * SUB_FFN

    def mixed_of(t):
        bb, r0 = where(t)
        start = pair * BPS_FFN + bb + r0 * BATCH
        ys5 = jnp.concatenate(
            [ys5_ref[j, pl.ds(start, SUB_FFN, stride=BATCH), :] for j in range(S5_SLABS)],
            axis=1).astype(_BF)
        return (_dot(ys5, wo_ref[0:S5_WIDTH, :])
                + _dot(yret_ref[bb, r0:r0 + SUB_FFN, :], wo_ref[S5_WIDTH:D_MODEL, :]))

    def prep_rows(t, mixed, lo, hi):
        bb, r0 = where(t)
        h = _layer_norm(x_ref[bb, r0 + lo:r0 + hi, :], lig_ref[...], lib_ref[...])
        return _layer_norm(DEEPNORM_ALPHA * h + mixed[lo:hi], l1g_ref[...], l1b_ref[...])

    def ffn_up(h1b, c, floor):
        up = jnp.maximum(_dot(h1b, wup_ref[:, c * FF_CHUNK:(c + 1) * FF_CHUNK]), floor)
        return (up * up).astype(_BF)

    def ffn_down(act, c):
        return _dot(act, wdn_ref[c * FF_CHUNK:(c + 1) * FF_CHUNK, :])

    def finish_rows(t, pre, lo, hi):
        bb, r0 = where(t)
        out = _layer_norm(pre[lo:hi], l2g_ref[...], l2b_ref[...])
        o_ref[bb, r0 + lo:r0 + hi, :] = out
        return out

    h1 = prep_rows(0, mixed_of(0), 0, SUB_FFN)
    pre_prev = None
    floor = 0.0
    for t in range(n_sub):
        h1b = h1.astype(_BF)
        pre = DEEPNORM_ALPHA * h1
        mixed_next = mixed_of(t + 1) if t + 1 < n_sub else None
        next_parts = []

        def side_work(k):
            lo, hi = k * piece, (k + 1) * piece
            done = []
            if mixed_next is not None:
                next_parts.append(prep_rows(t + 1, mixed_next, lo, hi))
                done.append(next_parts[-1])
            if pre_prev is not None:
                done.append(finish_rows(t - 1, pre_prev, lo, hi))
            return done

        for c in range(n_ch):
            act = ffn_up(h1b, c, floor)
            done = side_work(2 * c)
            pre = pre + ffn_down(act, c)
            done += side_work(2 * c + 1)
            floor = _zero_after(done)
        if next_parts:
            h1 = jnp.concatenate(next_parts, axis=0)
        pre_prev = pre
    finish_rows(n_sub - 1, pre_prev, 0, SUB_FFN)


def _row(v):
    return v.reshape(1, -1).astype(_F32)


@functools.lru_cache(maxsize=None)
def _position_tables():
    f32 = np.float32

    def fn(f, a):
        return f(a.astype(np.float64)).astype(f32)

    pos = np.arange(N_META + SEQ, dtype=f32)
    expo = np.arange(0, RET_HEAD_DIM, 2, dtype=f32) / f32(RET_HEAD_DIM)
    inv_freq = (f32(1.0) / fn(lambda e: np.power(ROPE_BASE, e), expo)).astype(f32)
    ang = pos[:, None] * inv_freq[None, :]
    cos, sin = fn(np.cos, ang), fn(np.sin, ang)
    cos2 = np.concatenate([cos, cos], axis=1)
    sin2 = np.concatenate([-sin, sin], axis=1)

    heads = np.arange(RET_HEADS, dtype=f32)
    log_gamma = fn(np.log1p, -fn(np.exp2, f32(-5.0) - heads))
    idx = np.arange(CHUNK, dtype=f32)
    diff = idx[:, None] - idx[None, :]
    dmat = np.where(diff[None] >= 0,
                    fn(np.exp, np.maximum(diff, f32(0.0))[None] * log_gamma[:, None, None]),
                    f32(0.0)).astype(f32)
    zeta = fn(np.exp, (f32(CHUNK - 1.0) - idx)[None] * log_gamma[:, None])
    xi = fn(np.exp, (idx + f32(1.0))[None] * log_gamma[:, None])
    gamma_chunk = fn(np.exp, f32(CHUNK) * log_gamma)
    hd = RET_HEAD_DIM
    zeta_b = np.ascontiguousarray(np.broadcast_to(zeta[:, :, None], (RET_HEADS, CHUNK, hd)))
    xi_b = np.ascontiguousarray(np.broadcast_to(xi[:, :, None], (RET_HEADS, CHUNK, hd)))
    gc_b = np.ascontiguousarray(np.broadcast_to(gamma_chunk[:, None, None], (RET_HEADS, 1, hd)))
    return cos2, sin2, dmat, zeta_b, xi_b, gc_b


def kernel(x, meta_tokens, ln_in_g, ln_in_b, w_in, s5_lambda_re, s5_lambda_im, s5_log_dt, s5_b_re, s5_b_im, s5_c_re, s5_c_im, s5_d, s5_w_glu, s5_b_glu, ret_gn_g, ret_gn_b, w_out, ln1_g, ln1_b, w_up, w_down, ln2_g, ln2_b):
    assert x.shape == (BATCH, SEQ, D_MODEL) and w_in.shape[0] == 1
    G, P, H = S5_GROUPS, S5_STATE, S5_GROUP_CH
    nrows = BATCH * SEQ
    arb2 = pltpu.CompilerParams(dimension_semantics=("arbitrary", "arbitrary"),
                                vmem_limit_bytes=VMEM_LIMIT)
    arb1 = pltpu.CompilerParams(dimension_semantics=("arbitrary",), vmem_limit_bytes=VMEM_LIMIT)

    lam_re, lam_im = s5_lambda_re[0], s5_lambda_im[0]
    dt = jnp.exp(s5_log_dt[0])[:, None]
    mag = jnp.exp(lam_re * dt)
    lbr = mag * jnp.cos(lam_im * dt)
    lbi = mag * jnp.sin(lam_im * dt)
    den = lam_re * lam_re + lam_im * lam_im
    nr = lbr - 1.0
    qr = (nr * lam_re + lbi * lam_im) / den
    qi = (lbi * lam_re - nr * lam_im) / den
    bbr = qr[..., None] * s5_b_re[0] - qi[..., None] * s5_b_im[0]
    bbi = qr[..., None] * s5_b_im[0] + qi[..., None] * s5_b_re[0]
    eye = jnp.eye(G, dtype=_F32)

    def blk_in(m):
        return (eye[:, None, :, None] * m.transpose(0, 2, 1)[:, :, None, :]).reshape(G * H, G * P)

    def blk_out(m):
        return (eye[:, None, :, None] * m.transpose(0, 2, 1)[:, :, None, :]).reshape(G * P, G * H)

    bblk = jnp.concatenate([blk_in(bbr), blk_in(bbi)], axis=1).astype(_BF)
    cblk = jnp.concatenate([blk_out(s5_c_re[0]), -blk_out(s5_c_im[0])], axis=0).astype(_BF)
    ar = lbr.reshape(1, S5_NSTATE)
    ai = lbi.reshape(1, S5_NSTATE)

    hd = RET_HEAD_DIM
    cos2, sin2, dmat, zeta_b, xi_b, gc_b = _position_tables()
    zmeta_b = zeta_b[:, CHUNK - N_META:, :]

    w_in_b = w_in[0].astype(_BF)
    w_out_b = w_out[0].astype(_BF)
    w_up_b = w_up[0].astype(_BF)
    w_dn_b = w_down[0].astype(_BF)
    lig, lib = _row(ln_in_g), _row(ln_in_b)

    s5_init, s0 = pl.pallas_call(
        _meta_kernel,
        out_shape=(jax.ShapeDtypeStruct((1, 2 * S5_NSTATE), _F32),
                   jax.ShapeDtypeStruct((RET_HEADS, hd, hd), _F32)),
        compiler_params=pltpu.CompilerParams(vmem_limit_bytes=VMEM_LIMIT),
        name="meta_prologue",
    )(meta_tokens.astype(_F32), lig, lib, w_in_b, bblk, ar, ai,
      cos2[:N_META], sin2[:N_META], zmeta_b)

    x2 = x.reshape(nrows, D_MODEL)
    n_t = SEQ // TM_PROJ
    rows_spec = lambda w: pl.BlockSpec((TM_PROJ, w), lambda i, b: (b * n_t + i, 0))
    u_tm, q, k, v, sg = pl.pallas_call(
        _in_proj_kernel,
        grid=(n_t, BATCH),
        in_specs=[rows_spec(D_MODEL), _const_spec((1, D_MODEL)), _const_spec((1, D_MODEL)),
                  _const_spec((D_MODEL, IN_PROJ_WIDTH)),
                  pl.BlockSpec((TM_PROJ, hd), lambda i, b: (i, 0)),
                  pl.BlockSpec((TM_PROJ, hd), lambda i, b: (i, 0))],
        out_specs=[pl.BlockSpec((S5_SLABS, TM_PROJ * BATCH, LANES), lambda i, b: (0, i, 0)),
                   rows_spec(RET_WIDTH), rows_spec(RET_WIDTH), rows_spec(RET_WIDTH),
                   rows_spec(RET_WIDTH)],
        out_shape=(jax.ShapeDtypeStruct((S5_SLABS, SEQ * BATCH, LANES), _F32),
                   jax.ShapeDtypeStruct((nrows, RET_WIDTH), _BF),
                   jax.ShapeDtypeStruct((nrows, RET_WIDTH), _BF),
                   jax.ShapeDtypeStruct((nrows, RET_WIDTH), _BF),
                   jax.ShapeDtypeStruct((nrows, RET_WIDTH), _BF)),
        compiler_params=arb2,
        name="in_proj",
    )(x2, lig, lib, w_in_b, cos2[N_META:], sin2[N_META:])

    rows_s5 = TT_S5 * BATCH
    ys5_tm = pl.pallas_call(
        _s5_kernel,
        grid=(SEQ // TT_S5,),
        in_specs=[pl.BlockSpec((S5_SLABS, rows_s5, LANES), lambda i: (0, i, 0)),
                  _const_spec((1, 2 * S5_NSTATE)),
                  _const_spec((S5_WIDTH, 2 * S5_NSTATE)), _const_spec((2 * S5_NSTATE, S5_WIDTH)),
                  _const_spec((1, S5_NSTATE)), _const_spec((1, S5_NSTATE)),
                  _const_spec((1, S5_WIDTH)), _const_spec((S5_WIDTH, S5_WIDTH)),
                  _const_spec((1, S5_WIDTH))],
        out_specs=pl.BlockSpec((S5_SLABS, rows_s5, LANES), lambda i: (0, i, 0)),
        out_shape=jax.ShapeDtypeStruct((S5_SLABS, SEQ * BATCH, LANES), _F32),
        scratch_shapes=[pltpu.VMEM((SB_S5 * BATCH, 2 * S5_NSTATE), _F32),
                        pltpu.VMEM((SB_S5 * BATCH, 2 * S5_NSTATE), _F32),
                        pltpu.VMEM((SB_S5 * BATCH, 2 * S5_NSTATE), _BF),
                        pltpu.VMEM((SB_S5 * BATCH, 2 * S5_NSTATE), _BF),
                        pltpu.VMEM((BATCH, 2 * S5_NSTATE), _F32)],
        compiler_params=arb1,
        name="s5_scan",
    )(u_tm, s5_init, bblk, cblk, ar, ai, _row(s5_d[0]), s5_w_glu[0].astype(_BF), _row(s5_b_glu[0]))

    n_r = SEQ // TR_RET
    ret_spec = pl.BlockSpec((TR_RET, RET_WIDTH), lambda b, c: (b * n_r + c, 0))
    tab = lambda n: _const_spec((RET_HEADS, n, hd))
    y_ret = pl.pallas_call(
        _ret_kernel,
        grid=(BATCH, n_r),
        in_specs=[ret_spec, ret_spec, ret_spec, ret_spec, tab(hd), tab(CHUNK), tab(CHUNK),
                  tab(CHUNK), tab(1), _const_spec((1, RET_WIDTH)), _const_spec((1, RET_WIDTH))],
        out_specs=ret_spec,
        out_shape=jax.ShapeDtypeStruct((nrows, RET_WIDTH), _BF),
        scratch_shapes=[pltpu.VMEM((RET_HEADS, hd, hd), _F32)],
        compiler_params=arb2,
        name="retention",
    )(q, k, v, sg, s0, dmat, xi_b, zeta_b, gc_b, _row(ret_gn_g[0]), _row(ret_gn_b[0]))

    n_f = SEQ // TM_FFN
    frow = lambda w: pl.BlockSpec((BPS_FFN, TM_FFN, w), lambda i, p: (p, i, 0))
    out = pl.pallas_call(
        _ffn_kernel,
        grid=(n_f, BATCH // BPS_FFN),
        in_specs=[frow(D_MODEL),
                  pl.BlockSpec((S5_SLABS, TM_FFN * BATCH, LANES), lambda i, p: (0, i, 0)),
                  frow(RET_WIDTH),
                  _const_spec((1, D_MODEL)), _const_spec((1, D_MODEL)),
                  _const_spec((D_MODEL, D_MODEL)),
                  _const_spec((1, D_MODEL)), _const_spec((1, D_MODEL)),
                  _const_spec((D_MODEL, D_FF)), _const_spec((D_FF, D_MODEL)),
                  _const_spec((1, D_MODEL)), _const_spec((1, D_MODEL))],
        out_specs=frow(D_MODEL),
        out_shape=jax.ShapeDtypeStruct((BATCH, SEQ, D_MODEL), _F32),
        compiler_params=arb2,
        name="out_ffn",
    )(x, ys5_tm, y_ret.reshape(BATCH, SEQ, RET_WIDTH), lig, lib, w_out_b,
      _row(ln1_g[0]), _row(ln1_b[0]), w_up_b, w_dn_b, _row(ln2_g[0]), _row(ln2_b[0]))

    return out
```

```python
import functools
import math

import jax
import jax.numpy as jnp
import numpy as np
from jax import lax
from jax.experimental import pallas as pl
from jax.experimental.pallas import tpu as pltpu

D_MODEL = 1024
BATCH = 8
SEQ = 4096
N_META = 16
S5_GROUP_CH = 16
S5_STATE = 64
S5_WIDTH = 256
S5_GROUPS = S5_WIDTH // S5_GROUP_CH
S5_NSTATE = S5_GROUPS * S5_STATE
RET_HEAD_DIM = 128
RET_WIDTH = 768
RET_HEADS = RET_WIDTH // RET_HEAD_DIM
CHUNK = 128
ROPE_BASE = 10000.0
D_FF = 4 * D_MODEL
LANES = 128
S5_SLABS = S5_WIDTH // LANES
LN_EPS = 1e-5
GN_EPS = 1e-5
IN_PROJ_WIDTH = S5_WIDTH + 4 * RET_WIDTH
DEEPNORM_ALPHA = 2.0 ** 0.25

_OFF_Q = S5_WIDTH
_OFF_K = _OFF_Q + RET_WIDTH
_OFF_V = _OFF_K + RET_WIDTH
_OFF_G = _OFF_V + RET_WIDTH

TM_PROJ = 512
BPS_PROJ = 2
SUB_PROJ = 256
MXU_TILE = 256
TT_S5 = 1024
SB_S5 = 32
TR_RET = 512
TM_FFN = 512
BPS_FFN = 2
SUB_FFN = 256
FF_CHUNK = 1024
VMEM_LIMIT = 56 * 1024 * 1024

_BF = jnp.bfloat16
_F32 = jnp.float32


def _const_spec(shape):
    nd = len(shape)
    return pl.BlockSpec(shape, lambda *_: (0,) * nd, pipeline_mode=pl.Buffered(1))


def _layer_norm(x, g, b):
    mu = jnp.mean(x, axis=-1, keepdims=True)
    xc = x - mu
    var = jnp.mean(xc * xc, axis=-1, keepdims=True)
    return xc * lax.rsqrt(var + LN_EPS) * g + b


def _dot(a, b):
    return jnp.dot(a, b, preferred_element_type=_F32)


def _zero_after(values):
    if not values:
        return 0.0
    tok = values[0][0:1, 0:1]
    for v in values[1:]:
        tok = tok + v[0:1, 0:1]
    bits = lax.shift_right_logical(lax.shift_right_logical(tok.astype(jnp.int32), 16), 16)
    return bits.astype(_F32)


def _rope_head(t, cos2, sin2):
    return t * cos2 + pltpu.roll(t, RET_HEAD_DIM // 2, 1) * sin2


def _meta_kernel(meta_ref, g_ref, b_ref, w_ref, bblk_ref, ar_ref, ai_ref, cos_ref, sin_ref,
                 zmeta_ref, s5_ref, s0_ref):
    hm = _layer_norm(meta_ref[...], g_ref[...], b_ref[...]).astype(_BF)
    u = _dot(hm, w_ref[:, 0:S5_WIDTH])
    bu = _dot(u.astype(_BF), bblk_ref[...])
    ar = ar_ref[...]
    ai = ai_ref[...]
    xr = jnp.zeros((1, S5_NSTATE), _F32)
    xi = jnp.zeros((1, S5_NSTATE), _F32)
    for t in range(N_META):
        br = bu[t:t + 1, 0:S5_NSTATE]
        bi = bu[t:t + 1, S5_NSTATE:2 * S5_NSTATE]
        xr, xi = ar * xr - ai * xi + br, ar * xi + ai * xr + bi
    s5_ref[:, 0:S5_NSTATE] = xr
    s5_ref[:, S5_NSTATE:2 * S5_NSTATE] = xi

    k = _dot(hm, w_ref[:, _OFF_K:_OFF_V])
    v = _dot(hm, w_ref[:, _OFF_V:_OFF_G]).astype(_BF)
    cos2 = cos_ref[...]
    sin2 = sin_ref[...]
    for h in range(RET_HEADS):
        sl = slice(h * RET_HEAD_DIM, (h + 1) * RET_HEAD_DIM)
        kh = _rope_head(k[:, sl], cos2, sin2) * (RET_HEAD_DIM ** -0.5)
        kz = (kh * zmeta_ref[h]).astype(_BF)
        s0_ref[h] = lax.dot_general(kz, v[:, sl], (((0,), (0,)), ((), ())),
                                    preferred_element_type=_F32)


def _in_proj_kernel(x_ref, g_ref, b_ref, w_ref, cos_ref, sin_ref,
                    u_ref, q_ref, k_ref, v_ref, sg_ref):
    pair = pl.program_id(1)
    halves = TM_PROJ // SUB_PROJ

    def where(s):
        return s // halves, (s % halves) * SUB_PROJ

    def normed(s):
        bb, r0 = where(s)
        return _layer_norm(x_ref[bb, r0:r0 + SUB_PROJ, :], g_ref[...], b_ref[...]).astype(_BF)

    def project(s, hn, hn_next):
        bb, r0 = where(s)
        rows = slice(r0, r0 + SUB_PROJ)
        cos2 = cos_ref[rows, :]
        sin2 = sin_ref[rows, :]
        u = _dot(hn, w_ref[:, 0:S5_WIDTH])
        start = pair * BPS_PROJ + bb + r0 * BATCH
        for j in range(S5_SLABS):
            u_ref[j, pl.ds(start, SUB_PROJ, stride=BATCH), :] = u[:, j * LANES:(j + 1) * LANES]
        q = _dot(hn, w_ref[:, _OFF_Q:_OFF_K])
        nxt = None if hn_next is None else hn_next()
        for h in range(RET_HEADS):
            sl = slice(h * RET_HEAD_DIM, (h + 1) * RET_HEAD_DIM)
            q_ref[bb, rows, sl] = _rope_head(q[:, sl], cos2, sin2).astype(_BF)
        k = _dot(hn, w_ref[:, _OFF_K:_OFF_V])
        for h in range(RET_HEADS):
            sl = slice(h * RET_HEAD_DIM, (h + 1) * RET_HEAD_DIM)
            k_ref[bb, rows, sl] = (_rope_head(k[:, sl], cos2, sin2)
                                   * (RET_HEAD_DIM ** -0.5)).astype(_BF)
        g = _dot(hn, w_ref[:, _OFF_G:IN_PROJ_WIDTH])
        sg_ref[bb, rows, :] = (g * jax.nn.sigmoid(g)).astype(_BF)
        v_ref[bb, rows, :] = _dot(hn, w_ref[:, _OFF_V:_OFF_G]).astype(_BF)
        return nxt

    n_sub = BPS_PROJ * halves
    hn = normed(0)
    for s in range(n_sub):
        nxt = (lambda s=s: normed(s + 1)) if s + 1 < n_sub else None
        hn = project(s, hn, nxt)


def _s5_kernel(u_ref, init_ref, bblk_ref, cblk_ref, ar_ref, ai_ref, d_ref, wglu_ref, bglu_ref,
               y_ref, bu0_scr, bu1_scr, xb0_scr, xb1_scr, st_scr):
    @pl.when(pl.program_id(0) == 0)
    def _():
        st_scr[...] = jnp.broadcast_to(init_ref[...], (BATCH, 2 * S5_NSTATE))

    rows_sb = SB_S5 * BATCH
    n_sb = TT_S5 // SB_S5
    n_piece = 2 * S5_NSTATE // MXU_TILE
    steps_piece = SB_S5 // n_piece
    re = slice(0, S5_NSTATE)
    im = slice(S5_NSTATE, 2 * S5_NSTATE)
    ar = jnp.broadcast_to(ar_ref[...], (BATCH, S5_NSTATE))
    ai = jnp.broadcast_to(ai_ref[...], (BATCH, S5_NSTATE))

    def rows_of(j):
        if isinstance(j, int):
            return pl.ds(j * rows_sb, rows_sb)
        return pl.ds(pl.multiple_of(j * rows_sb, rows_sb), rows_sb)

    def load_u(j):
        return jnp.concatenate([u_ref[s, rows_of(j), :] for s in range(S5_SLABS)], axis=1)

    bu_scr = (bu0_scr, bu1_scr)
    xb_scr = (xb0_scr, xb1_scr)

    def a_piece(ub, slot, c):
        cols = slice(c * MXU_TILE, (c + 1) * MXU_TILE)
        bu_scr[slot][:, cols] = _dot(ub, bblk_ref[:, cols])

    def b_piece(slot, c, xr, xi):
        for t in range(c * steps_piece, (c + 1) * steps_piece, 2):
            out_r, out_i = [], []
            for tt in (t, t + 1):
                rows = slice(tt * BATCH, (tt + 1) * BATCH)
                br = bu_scr[slot][rows, re]
                bi = bu_scr[slot][rows, im]
                xr, xi = ar * xr - ai * xi + br, ar * xi + ai * xr + bi
                out_r.append(xr)
                out_i.append(xi)
            rows2 = slice(t * BATCH, (t + 2) * BATCH)
            xb_scr[slot][rows2, re] = jnp.concatenate(out_r, axis=0).astype(_BF)
            xb_scr[slot][rows2, im] = jnp.concatenate(out_i, axis=0).astype(_BF)
        return xr, xi

    def c_piece(slot, c, acc):
        cols = slice(c * MXU_TILE, (c + 1) * MXU_TILE)
        part = _dot(xb_scr[slot][:, cols], cblk_ref[cols, :])
        return part if acc is None else acc + part

    def c_finish(j, acc):
        y = jax.nn.gelu(acc + d_ref[...] * load_u(j))
        gate = jax.nn.sigmoid(_dot(y.astype(_BF), wglu_ref[...]) + bglu_ref[...])
        y = y * gate
        for s in range(S5_SLABS):
            y_ref[s, rows_of(j), :] = y[:, s * LANES:(s + 1) * LANES]

    def iteration(j, slot, xr, xi, do_a, do_c):
        other = 1 - slot
        ub = load_u(j + 1).astype(_BF) if do_a else None
        acc = None
        for c in range(n_piece):
            if do_a:
                a_piece(ub, other, c)
            xr, xi = b_piece(slot, c, xr, xi)
            if do_c:
                acc = c_piece(other, c, acc)
        if do_c:
            c_finish(j - 1, acc)
        return xr, xi

    ub0 = load_u(0).astype(_BF)
    for c in range(n_piece):
        a_piece(ub0, 0, c)
    xr, xi = iteration(0, 0, st_scr[:, re], st_scr[:, im], True, False)

    def pair(p, carry):
        j = 2 * p + 1
        xr, xi = iteration(j, 1, carry[0], carry[1], True, True)
        return iteration(j + 1, 0, xr, xi, True, True)

    xr, xi = lax.fori_loop(0, (n_sb - 2) // 2, pair, (xr, xi))
    xr, xi = iteration(n_sb - 1, 1, xr, xi, False, True)
    st_scr[:, re] = xr
    st_scr[:, im] = xi
    acc = None
    for c in range(n_piece):
        acc = c_piece(1, c, acc)
    c_finish(n_sb - 1, acc)


def _ret_kernel(q_ref, k_ref, v_ref, sg_ref, s0_ref, dmat_ref, xi_ref, zeta_ref, gc_ref,
                gng_ref, gnb_ref, y_ref, s_scr):
    @pl.when(pl.program_id(1) == 0)
    def _():
        s_scr[...] = s0_ref[...]

    heads = [slice(h * RET_HEAD_DIM, (h + 1) * RET_HEAD_DIM) for h in range(RET_HEADS)]
    for c in range(TR_RET // CHUNK):
        rows = slice(c * CHUNK, (c + 1) * CHUNK)
        scores, cross = [], []
        for h, cols in enumerate(heads):
            qh = q_ref[rows, cols]
            kh = k_ref[rows, cols]
            state = s_scr[h]
            scores.append(lax.dot_general(qh, kh, (((1,), (1,)), ((), ())),
                                          preferred_element_type=_F32))
            cross.append(_dot(qh, state.astype(_BF)))
            kz = (kh.astype(_F32) * zeta_ref[h]).astype(_BF)
            s_scr[h] = gc_ref[h] * state + lax.dot_general(
                kz, v_ref[rows, cols], (((0,), (0,)), ((), ())), preferred_element_type=_F32)
        outs = []
        for h, cols in enumerate(heads):
            p = (scores[h] * dmat_ref[h]).astype(_BF)
            outs.append(_dot(p, v_ref[rows, cols]) + cross[h] * xi_ref[h])
        for h, cols in enumerate(heads):
            o = outs[h]
            mu = jnp.mean(o, axis=-1, keepdims=True)
            oc = o - mu
            var = jnp.mean(oc * oc, axis=-1, keepdims=True)
            on = oc * lax.rsqrt(var + GN_EPS) * gng_ref[:, cols] + gnb_ref[:, cols]
            y_ref[rows, cols] = (sg_ref[rows, cols].astype(_F32) * on).astype(_BF)


def _ffn_kernel(x_ref, ys5_ref, yret_ref, lig_ref, lib_ref, wo_ref, l1g_ref, l1b_ref,
                wup_ref, wdn_ref, l2g_ref, l2b_ref, o_ref):
    pair = pl.program_id(1)
    n_ch = D_FF // FF_CHUNK
    piece = SUB_FFN // (2 * n_ch)
    halves = TM_FFN // SUB_FFN
    n_sub = BPS_FFN * halves

    def where(t):
        return t // halves, (t % halves) * SUB_FFN

    def mixed_of(t):
        bb, r0 = where(t)
        start = pair * BPS_FFN + bb + r0 * BATCH
        ys5 = jnp.concatenate(
            [ys5_ref[j, pl.ds(start, SUB_FFN, stride=BATCH), :] for j in range(S5_SLABS)],
            axis=1).astype(_BF)
        return (_dot(ys5, wo_ref[0:S5_WIDTH, :])
                + _dot(yret_ref[bb, r0:r0 + SUB_FFN, :], wo_ref[S5_WIDTH:D_MODEL, :]))

    def prep_rows(t, mixed, lo, hi):
        bb, r0 = where(t)
        h = _layer_norm(x_ref[bb, r0 + lo:r0 + hi, :], lig_ref[...], lib_ref[...])
        return _layer_norm(DEEPNORM_ALPHA * h + mixed[lo:hi], l1g_ref[...], l1b_ref[...])

    def ffn_up(h1b, c, floor):
        up = jnp.maximum(_dot(h1b, wup_ref[:, c * FF_CHUNK:(c + 1) * FF_CHUNK]), floor)
        return (up * up).astype(_BF)

    def ffn_down(act, c):
        return _dot(act, wdn_ref[c * FF_CHUNK:(c + 1) * FF_CHUNK, :])

    def finish_rows(t, pre, lo, hi):
        bb, r0 = where(t)
        out = _layer_norm(pre[lo:hi], l2g_ref[...], l2b_ref[...])
        o_ref[bb, r0 + lo:r0 + hi, :] = out
        return out

    h1 = prep_rows(0, mixed_of(0), 0, SUB_FFN)
    pre_prev = None
    floor = 0.0
    for t in range(n_sub):
        h1b = h1.astype(_BF)
        pre = DEEPNORM_ALPHA * h1
        mixed_next = mixed_of(t + 1) if t + 1 < n_sub else None
        next_parts = []

        def side_work(k):
            lo, hi = k * piece, (k + 1) * piece
            done = []
            if mixed_next is not None:
                next_parts.append(prep_rows(t + 1, mixed_next, lo, hi))
                done.append(next_parts[-1])
            if pre_prev is not None:
                done.append(finish_rows(t - 1, pre_prev, lo, hi))
            return done

        for c in range(n_ch):
            act = ffn_up(h1b, c, floor)
            done = side_work(2 * c)
            pre = pre + ffn_down(act, c)
            done += side_work(2 * c + 1)
            floor = _zero_after(done)
        if next_parts:
            h1 = jnp.concatenate(next_parts, axis=0)
        pre_prev = pre
    finish_rows(n_sub - 1, pre_prev, 0, SUB_FFN)


def _row(v):
    return v.reshape(1, -1).astype(_F32)


@functools.lru_cache(maxsize=None)
def _position_tables():
    f32 = np.float32

    def fn(f, a):
        return f(a.astype(np.float64)).astype(f32)

    pos = np.arange(N_META + SEQ, dtype=f32)
    expo = np.arange(0, RET_HEAD_DIM, 2, dtype=f32) / f32(RET_HEAD_DIM)
    inv_freq = (f32(1.0) / fn(lambda e: np.power(ROPE_BASE, e), expo)).astype(f32)
    ang = pos[:, None] * inv_freq[None, :]
    cos, sin = fn(np.cos, ang), fn(np.sin, ang)
    cos2 = np.concatenate([cos, cos], axis=1)
    sin2 = np.concatenate([-sin, sin], axis=1)

    heads = np.arange(RET_HEADS, dtype=f32)
    log_gamma = fn(np.log1p, -fn(np.exp2, f32(-5.0) - heads))
    idx = np.arange(CHUNK, dtype=f32)
    diff = idx[:, None] - idx[None, :]
    dmat = np.where(diff[None] >= 0,
                    fn(np.exp, np.maximum(diff, f32(0.0))[None] * log_gamma[:, None, None]),
                    f32(0.0)).astype(f32)
    zeta = fn(np.exp, (f32(CHUNK - 1.0) - idx)[None] * log_gamma[:, None])
    xi = fn(np.exp, (idx + f32(1.0))[None] * log_gamma[:, None])
    gamma_chunk = fn(np.exp, f32(CHUNK) * log_gamma)
    hd = RET_HEAD_DIM
    zeta_b = np.ascontiguousarray(np.broadcast_to(zeta[:, :, None], (RET_HEADS, CHUNK, hd)))
    xi_b = np.ascontiguousarray(np.broadcast_to(xi[:, :, None], (RET_HEADS, CHUNK, hd)))
    gc_b = np.ascontiguousarray(np.broadcast_to(gamma_chunk[:, None, None], (RET_HEADS, 1, hd)))
    return cos2, sin2, dmat, zeta_b, xi_b, gc_b


def kernel(x, meta_tokens, ln_in_g, ln_in_b, w_in, s5_lambda_re, s5_lambda_im, s5_log_dt, s5_b_re, s5_b_im, s5_c_re, s5_c_im, s5_d, s5_w_glu, s5_b_glu, ret_gn_g, ret_gn_b, w_out, ln1_g, ln1_b, w_up, w_down, ln2_g, ln2_b):
    assert x.shape == (BATCH, SEQ, D_MODEL) and w_in.shape[0] == 1
    G, P, H = S5_GROUPS, S5_STATE, S5_GROUP_CH
    nrows = BATCH * SEQ
    arb2 = pltpu.CompilerParams(dimension_semantics=("arbitrary", "arbitrary"),
                                vmem_limit_bytes=VMEM_LIMIT)
    arb1 = pltpu.CompilerParams(dimension_semantics=("arbitrary",), vmem_limit_bytes=VMEM_LIMIT)

    lam_re, lam_im = s5_lambda_re[0], s5_lambda_im[0]
    dt = jnp.exp(s5_log_dt[0])[:, None]
    mag = jnp.exp(lam_re * dt)
    lbr = mag * jnp.cos(lam_im * dt)
    lbi = mag * jnp.sin(lam_im * dt)
    den = lam_re * lam_re + lam_im * lam_im
    nr = lbr - 1.0
    qr = (nr * lam_re + lbi * lam_im) / den
    qi = (lbi * lam_re - nr * lam_im) / den
    bbr = qr[..., None] * s5_b_re[0] - qi[..., None] * s5_b_im[0]
    bbi = qr[..., None] * s5_b_im[0] + qi[..., None] * s5_b_re[0]
    eye = jnp.eye(G, dtype=_F32)

    def blk_in(m):
        return (eye[:, None, :, None] * m.transpose(0, 2, 1)[:, :, None, :]).reshape(G * H, G * P)

    def blk_out(m):
        return (eye[:, None, :, None] * m.transpose(0, 2, 1)[:, :, None, :]).reshape(G * P, G * H)

    bblk = jnp.concatenate([blk_in(bbr), blk_in(bbi)], axis=1).astype(_BF)
    cblk = jnp.concatenate([blk_out(s5_c_re[0]), -blk_out(s5_c_im[0])], axis=0).astype(_BF)
    ar = lbr.reshape(1, S5_NSTATE)
    ai = lbi.reshape(1, S5_NSTATE)

    hd = RET_HEAD_DIM
    cos2, sin2, dmat, zeta_b, xi_b, gc_b = _position_tables()
    zmeta_b = zeta_b[:, CHUNK - N_META:, :]

    w_in_b = w_in[0].astype(_BF)
    w_out_b = w_out[0].astype(_BF)
    w_up_b = w_up[0].astype(_BF)
    w_dn_b = w_down[0].astype(_BF)
    lig, lib = _row(ln_in_g), _row(ln_in_b)

    s5_init, s0 = pl.pallas_call(
        _meta_kernel,
        out_shape=(jax.ShapeDtypeStruct((1, 2 * S5_NSTATE), _F32),
                   jax.ShapeDtypeStruct((RET_HEADS, hd, hd), _F32)),
        compiler_params=pltpu.CompilerParams(vmem_limit_bytes=VMEM_LIMIT),
        name="meta_prologue",
    )(meta_tokens.astype(_F32), lig, lib, w_in_b, bblk, ar, ai,
      cos2[:N_META], sin2[:N_META], zmeta_b)

    n_t = SEQ // TM_PROJ
    rows_spec = lambda w: pl.BlockSpec((BPS_PROJ, TM_PROJ, w), lambda i, p: (p, i, 0))
    ret_shape = jax.ShapeDtypeStruct((BATCH, SEQ, RET_WIDTH), _BF)
    u_tm, q, k, v, sg = pl.pallas_call(
        _in_proj_kernel,
        grid=(n_t, BATCH // BPS_PROJ),
        in_specs=[rows_spec(D_MODEL), _const_spec((1, D_MODEL)), _const_spec((1, D_MODEL)),
                  _const_spec((D_MODEL, IN_PROJ_WIDTH)),
                  pl.BlockSpec((TM_PROJ, hd), lambda i, p: (i, 0)),
                  pl.BlockSpec((TM_PROJ, hd), lambda i, p: (i, 0))],
        out_specs=[pl.BlockSpec((S5_SLABS, TM_PROJ * BATCH, LANES), lambda i, p: (0, i, 0)),
                   rows_spec(RET_WIDTH), rows_spec(RET_WIDTH), rows_spec(RET_WIDTH),
                   rows_spec(RET_WIDTH)],
        out_shape=(jax.ShapeDtypeStruct((S5_SLABS, SEQ * BATCH, LANES), _F32),
                   ret_shape, ret_shape, ret_shape, ret_shape),
        compiler_params=arb2,
        name="in_proj",
    )(x, lig, lib, w_in_b, cos2[N_META:], sin2[N_META:])
    q, k, v, sg = (t.reshape(nrows, RET_WIDTH) for t in (q, k, v, sg))

    rows_s5 = TT_S5 * BATCH
    ys5_tm = pl.pallas_call(
        _s5_kernel,
        grid=(SEQ // TT_S5,),
        in_specs=[pl.BlockSpec((S5_SLABS, rows_s5, LANES), lambda i: (0, i, 0)),
                  _const_spec((1, 2 * S5_NSTATE)),
                  _const_spec((S5_WIDTH, 2 * S5_NSTATE)), _const_spec((2 * S5_NSTATE, S5_WIDTH)),
                  _const_spec((1, S5_NSTATE)), _const_spec((1, S5_NSTATE)),
                  _const_spec((1, S5_WIDTH)), _const_spec((S5_WIDTH, S5_WIDTH)),
                  _const_spec((1, S5_WIDTH))],
        out_specs=pl.BlockSpec((S5_SLABS, rows_s5, LANES), lambda i: (0, i, 0)),
        out_shape=jax.ShapeDtypeStruct((S5_SLABS, SEQ * BATCH, LANES), _F32),
        scratch_shapes=[pltpu.VMEM((SB_S5 * BATCH, 2 * S5_NSTATE), _F32),
                        pltpu.VMEM((SB_S5 * BATCH, 2 * S5_NSTATE), _F32),
                        pltpu.VMEM((SB_S5 * BATCH, 2 * S5_NSTATE), _BF),
                        pltpu.VMEM((SB_S5 * BATCH, 2 * S5_NSTATE), _BF),
                        pltpu.VMEM((BATCH, 2 * S5_NSTATE), _F32)],
        compiler_params=arb1,
        name="s5_scan",
    )(u_tm, s5_init, bblk, cblk, ar, ai, _row(s5_d[0]), s5_w_glu[0].astype(_BF), _row(s5_b_glu[0]))

    n_r = SEQ // TR_RET
    ret_spec = pl.BlockSpec((TR_RET, RET_WIDTH), lambda b, c: (b * n_r + c, 0))
    tab = lambda n: _const_spec((RET_HEADS, n, hd))
    y_ret = pl.pallas_call(
        _ret_kernel,
        grid=(BATCH, n_r),
        in_specs=[ret_spec, ret_spec, ret_spec, ret_spec, tab(hd), tab(CHUNK), tab(CHUNK),
                  tab(CHUNK), tab(1), _const_spec((1, RET_WIDTH)), _const_spec((1, RET_WIDTH))],
        out_specs=ret_spec,
        out_shape=jax.ShapeDtypeStruct((nrows, RET_WIDTH), _BF),
        scratch_shapes=[pltpu.VMEM((RET_HEADS, hd, hd), _F32)],
        compiler_params=arb2,
        name="retention",
    )(q, k, v, sg, s0, dmat, xi_b, zeta_b, gc_b, _row(ret_gn_g[0]), _row(ret_gn_b[0]))

    n_f = SEQ // TM_FFN
    frow = lambda w: pl.BlockSpec((BPS_FFN, TM_FFN, w), lambda i, p: (p, i, 0))
    out = pl.pallas_call(
        _ffn_kernel,
        grid=(n_f, BATCH // BPS_FFN),
        in_specs=[frow(D_MODEL),
                  pl.BlockSpec((S5_SLABS, TM_FFN * BATCH, LANES), lambda i, p: (0, i, 0)),
                  frow(RET_WIDTH),
                  _const_spec((1, D_MODEL)), _const_spec((1, D_MODEL)),
                  _const_spec((D_MODEL, D_MODEL)),
                  _const_spec((1, D_MODEL)), _const_spec((1, D_MODEL)),
                  _const_spec((D_MODEL, D_FF)), _const_spec((D_FF, D_MODEL)),
                  _const_spec((1, D_MODEL)), _const_spec((1, D_MODEL))],
        out_specs=frow(D_MODEL),
        out_shape=jax.ShapeDtypeStruct((BATCH, SEQ, D_MODEL), _F32),
        compiler_params=arb2,
        name="out_ffn",
    )(x, ys5_tm, y_ret.reshape(BATCH, SEQ, RET_WIDTH), lig, lib, w_out_b,
      _row(ln1_g[0]), _row(ln1_b[0]), w_up_b, w_dn_b, _row(ln2_g[0]), _row(ln2_b[0]))

    return out
```

```python
import functools
import math

import jax
import jax.numpy as jnp
import numpy as np
from jax import lax
from jax.experimental import pallas as pl
from jax.experimental.pallas import tpu as pltpu

D_MODEL = 1024
BATCH = 8
SEQ = 4096
N_META = 16
S5_GROUP_CH = 16
S5_STATE = 64
S5_WIDTH = 256
S5_GROUPS = S5_WIDTH // S5_GROUP_CH
S5_NSTATE = S5_GROUPS * S5_STATE
RET_HEAD_DIM = 128
RET_WIDTH = 768
RET_HEADS = RET_WIDTH // RET_HEAD_DIM
CHUNK = 128
ROPE_BASE = 10000.0
D_FF = 4 * D_MODEL
LANES = 128
S5_SLABS = S5_WIDTH // LANES
LN_EPS = 1e-5
GN_EPS = 1e-5
IN_PROJ_WIDTH = S5_WIDTH + 4 * RET_WIDTH
DEEPNORM_ALPHA = 2.0 ** 0.25

_OFF_Q = S5_WIDTH
_OFF_K = _OFF_Q + RET_WIDTH
_OFF_V = _OFF_K + RET_WIDTH
_OFF_G = _OFF_V + RET_WIDTH

TM_PROJ = 512
BPS_PROJ = 2
SUB_PROJ = 256
MXU_TILE = 256
TT_S5 = 1024
SB_S5 = 32
TR_RET = 512
TM_FFN = 512
BPS_FFN = 2
SUB_FFN = 256
FF_CHUNK = 1024
VMEM_LIMIT = 56 * 1024 * 1024

_BF = jnp.bfloat16
_F32 = jnp.float32


def _const_spec(shape):
    nd = len(shape)
    return pl.BlockSpec(shape, lambda *_: (0,) * nd, pipeline_mode=pl.Buffered(1))


def _layer_norm(x, g, b):
    mu = jnp.mean(x, axis=-1, keepdims=True)
    xc = x - mu
    var = jnp.mean(xc * xc, axis=-1, keepdims=True)
    return xc * lax.rsqrt(var + LN_EPS) * g + b


def _dot(a, b):
    return jnp.dot(a, b, preferred_element_type=_F32)


def _zero_after(values):
    if not values:
        return 0.0
    tok = values[0][0:1, 0:1]
    for v in values[1:]:
        tok = tok + v[0:1, 0:1]
    bits = lax.shift_right_logical(lax.shift_right_logical(tok.astype(jnp.int32), 16), 16)
    return bits.astype(_F32)


def _rope_head(t, cos2, sin2):
    return t * cos2 + pltpu.roll(t, RET_HEAD_DIM // 2, 1) * sin2


def _meta_kernel(meta_ref, g_ref, b_ref, w32_ref, bblk_ref, ar_ref, ai_ref, cos_ref, sin_ref,
                 zmeta_ref, s5_ref, s0_ref, w_ref):
    slab = D_MODEL // 8
    for r in range(0, D_MODEL, slab):
        w_ref[r:r + slab, :] = w32_ref[r:r + slab, :].astype(_BF)
    hm = _layer_norm(meta_ref[...], g_ref[...], b_ref[...]).astype(_BF)
    u = _dot(hm, w_ref[:, 0:S5_WIDTH])
    bu = _dot(u.astype(_BF), bblk_ref[...])
    ar = ar_ref[...]
    ai = ai_ref[...]
    xr = jnp.zeros((1, S5_NSTATE), _F32)
    xi = jnp.zeros((1, S5_NSTATE), _F32)
    for t in range(N_META):
        br = bu[t:t + 1, 0:S5_NSTATE]
        bi = bu[t:t + 1, S5_NSTATE:2 * S5_NSTATE]
        xr, xi = ar * xr - ai * xi + br, ar * xi + ai * xr + bi
    s5_ref[:, 0:S5_NSTATE] = xr
    s5_ref[:, S5_NSTATE:2 * S5_NSTATE] = xi

    k = _dot(hm, w_ref[:, _OFF_K:_OFF_V])
    v = _dot(hm, w_ref[:, _OFF_V:_OFF_G]).astype(_BF)
    cos2 = cos_ref[...]
    sin2 = sin_ref[...]
    for h in range(RET_HEADS):
        sl = slice(h * RET_HEAD_DIM, (h + 1) * RET_HEAD_DIM)
        kh = _rope_head(k[:, sl], cos2, sin2) * (RET_HEAD_DIM ** -0.5)
        kz = (kh * zmeta_ref[h]).astype(_BF)
        s0_ref[h] = lax.dot_general(kz, v[:, sl], (((0,), (0,)), ((), ())),
                                    preferred_element_type=_F32)


def _in_proj_kernel(x_ref, g_ref, b_ref, w_ref, cos_ref, sin_ref, wo32_ref, wup32_ref, wdn32_ref,
                    u_ref, q_ref, k_ref, v_ref, sg_ref, wo16_ref, wup16_ref, wdn16_ref):
    wo16_ref[...] = wo32_ref[...].astype(_BF)
    wup16_ref[...] = wup32_ref[...].astype(_BF)
    wdn16_ref[...] = wdn32_ref[...].astype(_BF)

    pair = pl.program_id(1)
    halves = TM_PROJ // SUB_PROJ

    def where(s):
        return s // halves, (s % halves) * SUB_PROJ

    def normed(s):
        bb, r0 = where(s)
        return _layer_norm(x_ref[bb, r0:r0 + SUB_PROJ, :], g_ref[...], b_ref[...]).astype(_BF)

    def project(s, hn, hn_next):
        bb, r0 = where(s)
        rows = slice(r0, r0 + SUB_PROJ)
        cos2 = cos_ref[rows, :]
        sin2 = sin_ref[rows, :]
        u = _dot(hn, w_ref[:, 0:S5_WIDTH])
        start = pair * BPS_PROJ + bb + r0 * BATCH
        for j in range(S5_SLABS):
            u_ref[j, pl.ds(start, SUB_PROJ, stride=BATCH), :] = u[:, j * LANES:(j + 1) * LANES]
        q = _dot(hn, w_ref[:, _OFF_Q:_OFF_K])
        nxt = None if hn_next is None else hn_next()
        for h in range(RET_HEADS):
            sl = slice(h * RET_HEAD_DIM, (h + 1) * RET_HEAD_DIM)
            q_ref[bb, rows, sl] = _rope_head(q[:, sl], cos2, sin2).astype(_BF)
        k = _dot(hn, w_ref[:, _OFF_K:_OFF_V])
        for h in range(RET_HEADS):
            sl = slice(h * RET_HEAD_DIM, (h + 1) * RET_HEAD_DIM)
            k_ref[bb, rows, sl] = (_rope_head(k[:, sl], cos2, sin2)
                                   * (RET_HEAD_DIM ** -0.5)).astype(_BF)
        g = _dot(hn, w_ref[:, _OFF_G:IN_PROJ_WIDTH])
        sg_ref[bb, rows, :] = (g * jax.nn.sigmoid(g)).astype(_BF)
        v_ref[bb, rows, :] = _dot(hn, w_ref[:, _OFF_V:_OFF_G]).astype(_BF)
        return nxt

    n_sub = BPS_PROJ * halves
    hn = normed(0)
    for s in range(n_sub):
        nxt = (lambda s=s: normed(s + 1)) if s + 1 < n_sub else None
        hn = project(s, hn, nxt)


def _s5_kernel(u_ref, init_ref, bblk_ref, cblk_ref, ar_ref, ai_ref, d_ref, wglu_ref, bglu_ref,
               y_ref, bu0_scr, bu1_scr, xb0_scr, xb1_scr, st_scr):
    @pl.when(pl.program_id(0) == 0)
    def _():
        st_scr[...] = jnp.broadcast_to(init_ref[...], (BATCH, 2 * S5_NSTATE))

    rows_sb = SB_S5 * BATCH
    n_sb = TT_S5 // SB_S5
    n_piece = 2 * S5_NSTATE // MXU_TILE
    steps_piece = SB_S5 // n_piece
    re = slice(0, S5_NSTATE)
    im = slice(S5_NSTATE, 2 * S5_NSTATE)
    ar = jnp.broadcast_to(ar_ref[...], (BATCH, S5_NSTATE))
    ai = jnp.broadcast_to(ai_ref[...], (BATCH, S5_NSTATE))

    def rows_of(j):
        if isinstance(j, int):
            return pl.ds(j * rows_sb, rows_sb)
        return pl.ds(pl.multiple_of(j * rows_sb, rows_sb), rows_sb)

    def load_u(j):
        return jnp.concatenate([u_ref[s, rows_of(j), :] for s in range(S5_SLABS)], axis=1)

    bu_scr = (bu0_scr, bu1_scr)
    xb_scr = (xb0_scr, xb1_scr)

    def a_piece(ub, slot, c):
        cols = slice(c * MXU_TILE, (c + 1) * MXU_TILE)
        bu_scr[slot][:, cols] = _dot(ub, bblk_ref[:, cols])

    def b_piece(slot, c, xr, xi):
        for t in range(c * steps_piece, (c + 1) * steps_piece, 2):
            out_r, out_i = [], []
            for tt in (t, t + 1):
                rows = slice(tt * BATCH, (tt + 1) * BATCH)
                br = bu_scr[slot][rows, re]
                bi = bu_scr[slot][rows, im]
                xr, xi = ar * xr - ai * xi + br, ar * xi + ai * xr + bi
                out_r.append(xr)
                out_i.append(xi)
            rows2 = slice(t * BATCH, (t + 2) * BATCH)
            xb_scr[slot][rows2, re] = jnp.concatenate(out_r, axis=0).astype(_BF)
            xb_scr[slot][rows2, im] = jnp.concatenate(out_i, axis=0).astype(_BF)
        return xr, xi

    def c_piece(slot, c, acc):
        cols = slice(c * MXU_TILE, (c + 1) * MXU_TILE)
        part = _dot(xb_scr[slot][:, cols], cblk_ref[cols, :])
        return part if acc is None else acc + part

    def c_finish(j, acc):
        y = jax.nn.gelu(acc + d_ref[...] * load_u(j))
        gate = jax.nn.sigmoid(_dot(y.astype(_BF), wglu_ref[...]) + bglu_ref[...])
        y = y * gate
        for s in range(S5_SLABS):
            y_ref[s, rows_of(j), :] = y[:, s * LANES:(s + 1) * LANES]

    def iteration(j, slot, xr, xi, do_a, do_c):
        other = 1 - slot
        ub = load_u(j + 1).astype(_BF) if do_a else None
        acc = None
        for c in range(n_piece):
            if do_a:
                a_piece(ub, other, c)
            xr, xi = b_piece(slot, c, xr, xi)
            if do_c:
                acc = c_piece(other, c, acc)
        if do_c:
            c_finish(j - 1, acc)
        return xr, xi

    ub0 = load_u(0).astype(_BF)
    for c in range(n_piece):
        a_piece(ub0, 0, c)
    xr, xi = iteration(0, 0, st_scr[:, re], st_scr[:, im], True, False)

    def pair(p, carry):
        j = 2 * p + 1
        xr, xi = iteration(j, 1, carry[0], carry[1], True, True)
        return iteration(j + 1, 0, xr, xi, True, True)

    xr, xi = lax.fori_loop(0, (n_sb - 2) // 2, pair, (xr, xi))
    xr, xi = iteration(n_sb - 1, 1, xr, xi, False, True)
    st_scr[:, re] = xr
    st_scr[:, im] = xi
    acc = None
    for c in range(n_piece):
        acc = c_piece(1, c, acc)
    c_finish(n_sb - 1, acc)


def _ret_kernel(q_ref, k_ref, v_ref, sg_ref, s0_ref, dmat_ref, xi_ref, zeta_ref, gc_ref,
                gng_ref, gnb_ref, y_ref, s_scr):
    @pl.when(pl.program_id(1) == 0)
    def _():
        s_scr[...] = s0_ref[...]

    heads = [slice(h * RET_HEAD_DIM, (h + 1) * RET_HEAD_DIM) for h in range(RET_HEADS)]
    for c in range(TR_RET // CHUNK):
        rows = slice(c * CHUNK, (c + 1) * CHUNK)
        scores, cross = [], []
        for h, cols in enumerate(heads):
            qh = q_ref[rows, cols]
            kh = k_ref[rows, cols]
            state = s_scr[h]
            scores.append(lax.dot_general(qh, kh, (((1,), (1,)), ((), ())),
                                          preferred_element_type=_F32))
            cross.append(_dot(qh, state.astype(_BF)))
            kz = (kh.astype(_F32) * zeta_ref[h]).astype(_BF)
            s_scr[h] = gc_ref[h] * state + lax.dot_general(
                kz, v_ref[rows, cols], (((0,), (0,)), ((), ())), preferred_element_type=_F32)
        outs = []
        for h, cols in enumerate(heads):
            p = (scores[h] * dmat_ref[h]).astype(_BF)
            outs.append(_dot(p, v_ref[rows, cols]) + cross[h] * xi_ref[h])
        for h, cols in enumerate(heads):
            o = outs[h]
            mu = jnp.mean(o, axis=-1, keepdims=True)
            oc = o - mu
            var = jnp.mean(oc * oc, axis=-1, keepdims=True)
            on = oc * lax.rsqrt(var + GN_EPS) * gng_ref[:, cols] + gnb_ref[:, cols]
            y_ref[rows, cols] = (sg_ref[rows, cols].astype(_F32) * on).astype(_BF)


def _ffn_kernel(x_ref, ys5_ref, yret_ref, lig_ref, lib_ref, wo_ref, l1g_ref, l1b_ref,
                wup_ref, wdn_ref, l2g_ref, l2b_ref, o_ref):
    pair = pl.program_id(1)
    n_ch = D_FF // FF_CHUNK
    piece = SUB_FFN // (2 * n_ch)
    halves = TM_FFN // SUB_FFN
    n_sub = BPS_FFN * halves

    def where(t):
        return t // halves, (t % halves) * SUB_FFN

    def mixed_of(t):
        bb, r0 = where(t)
        start = pair * BPS_FFN + bb + r0 * BATCH
        ys5 = jnp.concatenate(
            [ys5_ref[j, pl.ds(start, SUB_FFN, stride=BATCH), :] for j in range(S5_SLABS)],
            axis=1).astype(_BF)
        return (_dot(ys5, wo_ref[0:S5_WIDTH, :])
                + _dot(yret_ref[bb, r0:r0 + SUB_FFN, :], wo_ref[S5_WIDTH:D_MODEL, :]))

    def prep_rows(t, mixed, lo, hi):
        bb, r0 = where(t)
        h = _layer_norm(x_ref[bb, r0 + lo:r0 + hi, :], lig_ref[...], lib_ref[...])
        return _layer_norm(DEEPNORM_ALPHA * h + mixed[lo:hi], l1g_ref[...], l1b_ref[...])

    def ffn_up(h1b, c, floor):
        up = jnp.maximum(_dot(h1b, wup_ref[:, c * FF_CHUNK:(c + 1) * FF_CHUNK]), floor)
        return (up * up).astype(_BF)

    def ffn_down(act, c):
        return _dot(act, wdn_ref[c * FF_CHUNK:(c + 1) * FF_CHUNK, :])

    def finish_rows(t, pre, lo, hi):
        bb, r0 = where(t)
        out = _layer_norm(pre[lo:hi], l2g_ref[...], l2b_ref[...])
        o_ref[bb, r0 + lo:r0 + hi, :] = out
        return out

    h1 = prep_rows(0, mixed_of(0), 0, SUB_FFN)
    pre_prev = None
    floor = 0.0
    for t in range(n_sub):
        h1b = h1.astype(_BF)
        pre = DEEPNORM_ALPHA * h1
        mixed_next = mixed_of(t + 1) if t + 1 < n_sub else None
        next_parts = []

        def side_work(k):
            lo, hi = k * piece, (k + 1) * piece
            done = []
            if mixed_next is not None:
                next_parts.append(prep_rows(t + 1, mixed_next, lo, hi))
                done.append(next_parts[-1])
            if pre_prev is not None:
                done.append(finish_rows(t - 1, pre_prev, lo, hi))
            return done

        for c in range(n_ch):
            act = ffn_up(h1b, c, floor)
            done = side_work(2 * c)
            pre = pre + ffn_down(act, c)
            done += side_work(2 * c + 1)
            floor = _zero_after(done)
        if next_parts:
            h1 = jnp.concatenate(next_parts, axis=0)
        pre_prev = pre
    finish_rows(n_sub - 1, pre_prev, 0, SUB_FFN)


def _row(v):
    return v.reshape(1, -1).astype(_F32)


@functools.lru_cache(maxsize=None)
def _position_tables():
    f32 = np.float32

    def fn(f, a):
        return f(a.astype(np.float64)).astype(f32)

    pos = np.arange(N_META + SEQ, dtype=f32)
    expo = np.arange(0, RET_HEAD_DIM, 2, dtype=f32) / f32(RET_HEAD_DIM)
    inv_freq = (f32(1.0) / fn(lambda e: np.power(ROPE_BASE, e), expo)).astype(f32)
    ang = pos[:, None] * inv_freq[None, :]
    cos, sin = fn(np.cos, ang), fn(np.sin, ang)
    cos2 = np.concatenate([cos, cos], axis=1)
    sin2 = np.concatenate([-sin, sin], axis=1)

    heads = np.arange(RET_HEADS, dtype=f32)
    log_gamma = fn(np.log1p, -fn(np.exp2, f32(-5.0) - heads))
    idx = np.arange(CHUNK, dtype=f32)
    diff = idx[:, None] - idx[None, :]
    dmat = np.where(diff[None] >= 0,
                    fn(np.exp, np.maximum(diff, f32(0.0))[None] * log_gamma[:, None, None]),
                    f32(0.0)).astype(f32)
    zeta = fn(np.exp, (f32(CHUNK - 1.0) - idx)[None] * log_gamma[:, None])
    xi = fn(np.exp, (idx + f32(1.0))[None] * log_gamma[:, None])
    gamma_chunk = fn(np.exp, f32(CHUNK) * log_gamma)
    hd = RET_HEAD_DIM
    zeta_b = np.ascontiguousarray(np.broadcast_to(zeta[:, :, None], (RET_HEADS, CHUNK, hd)))
    xi_b = np.ascontiguousarray(np.broadcast_to(xi[:, :, None], (RET_HEADS, CHUNK, hd)))
    gc_b = np.ascontiguousarray(np.broadcast_to(gamma_chunk[:, None, None], (RET_HEADS, 1, hd)))
    return cos2, sin2, dmat, zeta_b, xi_b, gc_b


def kernel(x, meta_tokens, ln_in_g, ln_in_b, w_in, s5_lambda_re, s5_lambda_im, s5_log_dt, s5_b_re, s5_b_im, s5_c_re, s5_c_im, s5_d, s5_w_glu, s5_b_glu, ret_gn_g, ret_gn_b, w_out, ln1_g, ln1_b, w_up, w_down, ln2_g, ln2_b):
    assert x.shape == (BATCH, SEQ, D_MODEL) and w_in.shape[0] == 1
    G, P, H = S5_GROUPS, S5_STATE, S5_GROUP_CH
    nrows = BATCH * SEQ
    arb2 = pltpu.CompilerParams(dimension_semantics=("arbitrary", "arbitrary"),
                                vmem_limit_bytes=VMEM_LIMIT)
    arb1 = pltpu.CompilerParams(dimension_semantics=("arbitrary",), vmem_limit_bytes=VMEM_LIMIT)

    lam_re, lam_im = s5_lambda_re[0], s5_lambda_im[0]
    dt = jnp.exp(s5_log_dt[0])[:, None]
    mag = jnp.exp(lam_re * dt)
    lbr = mag * jnp.cos(lam_im * dt)
    lbi = mag * jnp.sin(lam_im * dt)
    den = lam_re * lam_re + lam_im * lam_im
    nr = lbr - 1.0
    qr = (nr * lam_re + lbi * lam_im) / den
    qi = (lbi * lam_re - nr * lam_im) / den
    bbr = qr[..., None] * s5_b_re[0] - qi[..., None] * s5_b_im[0]
    bbi = qr[..., None] * s5_b_im[0] + qi[..., None] * s5_b_re[0]
    eye = jnp.eye(G, dtype=_F32)

    def blk_in(m):
        return (eye[:, None, :, None] * m.transpose(0, 2, 1)[:, :, None, :]).reshape(G * H, G * P)

    def blk_out(m):
        return (eye[:, None, :, None] * m.transpose(0, 2, 1)[:, :, None, :]).reshape(G * P, G * H)

    bblk = jnp.concatenate([blk_in(bbr), blk_in(bbi)], axis=1).astype(_BF)
    cblk = jnp.concatenate([blk_out(s5_c_re[0]), -blk_out(s5_c_im[0])], axis=0).astype(_BF)
    ar = lbr.reshape(1, S5_NSTATE)
    ai = lbi.reshape(1, S5_NSTATE)

    hd = RET_HEAD_DIM
    cos2, sin2, dmat, zeta_b, xi_b, gc_b = _position_tables()
    zmeta_b = zeta_b[:, CHUNK - N_META:, :]

    lig, lib = _row(ln_in_g), _row(ln_in_b)

    s5_init, s0, w_in_b = pl.pallas_call(
        _meta_kernel,
        out_shape=(jax.ShapeDtypeStruct((1, 2 * S5_NSTATE), _F32),
                   jax.ShapeDtypeStruct((RET_HEADS, hd, hd), _F32),
                   jax.ShapeDtypeStruct((D_MODEL, IN_PROJ_WIDTH), _BF)),
        compiler_params=pltpu.CompilerParams(vmem_limit_bytes=VMEM_LIMIT),
        name="meta_prologue",
    )(meta_tokens.astype(_F32), lig, lib, w_in[0], bblk, ar, ai,
      cos2[:N_META], sin2[:N_META], zmeta_b)

    n_t = SEQ // TM_PROJ
    rows_spec = lambda w: pl.BlockSpec((BPS_PROJ, TM_PROJ, w), lambda i, p: (p, i, 0))
    ret_shape = jax.ShapeDtypeStruct((BATCH, SEQ, RET_WIDTH), _BF)
    n_pairs = BATCH // BPS_PROJ
    n_steps = n_t * n_pairs

    def slab_spec(shape):
        return pl.BlockSpec((shape[0] // n_steps, shape[1]), lambda i, p: (i * n_pairs + p, 0))

    w_shapes = [(D_MODEL, D_MODEL), (D_MODEL, D_FF), (D_FF, D_MODEL)]
    u_tm, q, k, v, sg, w_out_b, w_up_b, w_dn_b = pl.pallas_call(
        _in_proj_kernel,
        grid=(n_t, n_pairs),
        in_specs=[rows_spec(D_MODEL), _const_spec((1, D_MODEL)), _const_spec((1, D_MODEL)),
                  _const_spec((D_MODEL, IN_PROJ_WIDTH)),
                  pl.BlockSpec((TM_PROJ, hd), lambda i, p: (i, 0)),
                  pl.BlockSpec((TM_PROJ, hd), lambda i, p: (i, 0))]
                 + [slab_spec(s) for s in w_shapes],
        out_specs=[pl.BlockSpec((S5_SLABS, TM_PROJ * BATCH, LANES), lambda i, p: (0, i, 0)),
                   rows_spec(RET_WIDTH), rows_spec(RET_WIDTH), rows_spec(RET_WIDTH),
                   rows_spec(RET_WIDTH)] + [slab_spec(s) for s in w_shapes],
        out_shape=(jax.ShapeDtypeStruct((S5_SLABS, SEQ * BATCH, LANES), _F32),
                   ret_shape, ret_shape, ret_shape, ret_shape)
                  + tuple(jax.ShapeDtypeStruct(s, _BF) for s in w_shapes),
        compiler_params=arb2,
        name="in_proj",
    )(x, lig, lib, w_in_b, cos2[N_META:], sin2[N_META:], w_out[0], w_up[0], w_down[0])
    q, k, v, sg = (t.reshape(nrows, RET_WIDTH) for t in (q, k, v, sg))

    rows_s5 = TT_S5 * BATCH
    ys5_tm = pl.pallas_call(
        _s5_kernel,
        grid=(SEQ // TT_S5,),
        in_specs=[pl.BlockSpec((S5_SLABS, rows_s5, LANES), lambda i: (0, i, 0)),
                  _const_spec((1, 2 * S5_NSTATE)),
                  _const_spec((S5_WIDTH, 2 * S5_NSTATE)), _const_spec((2 * S5_NSTATE, S5_WIDTH)),
                  _const_spec((1, S5_NSTATE)), _const_spec((1, S5_NSTATE)),
                  _const_spec((1, S5_WIDTH)), _const_spec((S5_WIDTH, S5_WIDTH)),
                  _const_spec((1, S5_WIDTH))],
        out_specs=pl.BlockSpec((S5_SLABS, rows_s5, LANES), lambda i: (0, i, 0)),
        out_shape=jax.ShapeDtypeStruct((S5_SLABS, SEQ * BATCH, LANES), _F32),
        scratch_shapes=[pltpu.VMEM((SB_S5 * BATCH, 2 * S5_NSTATE), _F32),
                        pltpu.VMEM((SB_S5 * BATCH, 2 * S5_NSTATE), _F32),
                        pltpu.VMEM((SB_S5 * BATCH, 2 * S5_NSTATE), _BF),
                        pltpu.VMEM((SB_S5 * BATCH, 2 * S5_NSTATE), _BF),
                        pltpu.VMEM((BATCH, 2 * S5_NSTATE), _F32)],
        compiler_params=arb1,
        name="s5_scan",
    )(u_tm, s5_init, bblk, cblk, ar, ai, _row(s5_d[0]), s5_w_glu[0].astype(_BF), _row(s5_b_glu[0]))

    n_r = SEQ // TR_RET
    ret_spec = pl.BlockSpec((TR_RET, RET_WIDTH), lambda b, c: (b * n_r + c, 0))
    tab = lambda n: _const_spec((RET_HEADS, n, hd))
    y_ret = pl.pallas_call(
        _ret_kernel,
        grid=(BATCH, n_r),
        in_specs=[ret_spec, ret_spec, ret_spec, ret_spec, tab(hd), tab(CHUNK), tab(CHUNK),
                  tab(CHUNK), tab(1), _const_spec((1, RET_WIDTH)), _const_spec((1, RET_WIDTH))],
        out_specs=ret_spec,
        out_shape=jax.ShapeDtypeStruct((nrows, RET_WIDTH), _BF),
        scratch_shapes=[pltpu.VMEM((RET_HEADS, hd, hd), _F32)],
        compiler_params=arb2,
        name="retention",
    )(q, k, v, sg, s0, dmat, xi_b, zeta_b, gc_b, _row(ret_gn_g[0]), _row(ret_gn_b[0]))

    n_f = SEQ // TM_FFN
    frow = lambda w: pl.BlockSpec((BPS_FFN, TM_FFN, w), lambda i, p: (p, i, 0))
    out = pl.pallas_call(
        _ffn_kernel,
        grid=(n_f, BATCH // BPS_FFN),
        in_specs=[frow(D_MODEL),
                  pl.BlockSpec((S5_SLABS, TM_FFN * BATCH, LANES), lambda i, p: (0, i, 0)),
                  frow(RET_WIDTH),
                  _const_spec((1, D_MODEL)), _const_spec((1, D_MODEL)),
                  _const_spec((D_MODEL, D_MODEL)),
                  _const_spec((1, D_MODEL)), _const_spec((1, D_MODEL)),
                  _const_spec((D_MODEL, D_FF)), _const_spec((D_FF, D_MODEL)),
                  _const_spec((1, D_MODEL)), _const_spec((1, D_MODEL))],
        out_specs=frow(D_MODEL),
        out_shape=jax.ShapeDtypeStruct((BATCH, SEQ, D_MODEL), _F32),
        compiler_params=arb2,
        name="out_ffn",
    )(x, ys5_tm, y_ret.reshape(BATCH, SEQ, RET_WIDTH), lig, lib, w_out_b,
      _row(ln1_g[0]), _row(ln1_b[0]), w_up_b, w_dn_b, _row(ln2_g[0]), _row(ln2_b[0]))

    return out
```

```python
import functools
import math

import jax
import jax.numpy as jnp
import numpy as np
from jax import lax
from jax.experimental import pallas as pl
from jax.experimental.pallas import tpu as pltpu

D_MODEL = 1024
BATCH = 8
SEQ = 4096
N_META = 16
S5_GROUP_CH = 16
S5_STATE = 64
S5_WIDTH = 256
S5_GROUPS = S5_WIDTH // S5_GROUP_CH
S5_NSTATE = S5_GROUPS * S5_STATE
RET_HEAD_DIM = 128
RET_WIDTH = 768
RET_HEADS = RET_WIDTH // RET_HEAD_DIM
CHUNK = 128
ROPE_BASE = 10000.0
D_FF = 4 * D_MODEL
LANES = 128
S5_SLABS = S5_WIDTH // LANES
LN_EPS = 1e-5
GN_EPS = 1e-5
IN_PROJ_WIDTH = S5_WIDTH + 4 * RET_WIDTH
DEEPNORM_ALPHA = 2.0 ** 0.25

_OFF_Q = S5_WIDTH
_OFF_K = _OFF_Q + RET_WIDTH
_OFF_V = _OFF_K + RET_WIDTH
_OFF_G = _OFF_V + RET_WIDTH

TM_PROJ = 512
BPS_PROJ = 2
SUB_PROJ = 256
MXU_TILE = 256
TT_S5 = 1024
SB_S5 = 32
TR_RET = 512
TM_FFN = 512
BPS_FFN = 2
SUB_FFN = 256
FF_CHUNK = 1024
VMEM_LIMIT = 56 * 1024 * 1024

_BF = jnp.bfloat16
_F32 = jnp.float32


def _const_spec(shape):
    nd = len(shape)
    return pl.BlockSpec(shape, lambda *_: (0,) * nd, pipeline_mode=pl.Buffered(1))


def _layer_norm(x, g, b, after=()):
    mu = jnp.mean(x, axis=-1, keepdims=True)
    if after:
        mu = mu + _zero_after(list(after))
    xc = x - mu
    var = jnp.mean(xc * xc, axis=-1, keepdims=True)
    return xc * lax.rsqrt(var + LN_EPS) * g + b


def _dot(a, b):
    return jnp.dot(a, b, preferred_element_type=_F32)


def _zero_after(values):
    if not values:
        return 0.0
    tok = values[0][0:1, 0:1]
    for v in values[1:]:
        tok = tok + v[0:1, 0:1]
    bits = lax.shift_right_logical(lax.shift_right_logical(tok.astype(jnp.int32), 16), 16)
    return bits.astype(_F32)


def _rope_head(t, cos2, sin2):
    return t * cos2 + pltpu.roll(t, RET_HEAD_DIM // 2, 1) * sin2


def _meta_kernel(meta_ref, g_ref, b_ref, w32_ref, bblk_ref, ar_ref, ai_ref, cos_ref, sin_ref,
                 zmeta_ref, s5_ref, s0_ref, w_ref):
    slab = D_MODEL // 8
    for r in range(0, D_MODEL, slab):
        w_ref[r:r + slab, :] = w32_ref[r:r + slab, :].astype(_BF)
    hm = _layer_norm(meta_ref[...], g_ref[...], b_ref[...]).astype(_BF)
    u = _dot(hm, w_ref[:, 0:S5_WIDTH])
    bu = _dot(u.astype(_BF), bblk_ref[...])
    ar = ar_ref[...]
    ai = ai_ref[...]
    xr = jnp.zeros((1, S5_NSTATE), _F32)
    xi = jnp.zeros((1, S5_NSTATE), _F32)
    for t in range(N_META):
        br = bu[t:t + 1, 0:S5_NSTATE]
        bi = bu[t:t + 1, S5_NSTATE:2 * S5_NSTATE]
        xr, xi = ar * xr - ai * xi + br, ar * xi + ai * xr + bi
    s5_ref[:, 0:S5_NSTATE] = xr
    s5_ref[:, S5_NSTATE:2 * S5_NSTATE] = xi

    k = _dot(hm, w_ref[:, _OFF_K:_OFF_V])
    v = _dot(hm, w_ref[:, _OFF_V:_OFF_G]).astype(_BF)
    cos2 = cos_ref[...]
    sin2 = sin_ref[...]
    for h in range(RET_HEADS):
        sl = slice(h * RET_HEAD_DIM, (h + 1) * RET_HEAD_DIM)
        kh = _rope_head(k[:, sl], cos2, sin2) * (RET_HEAD_DIM ** -0.5)
        kz = (kh * zmeta_ref[h]).astype(_BF)
        s0_ref[h] = lax.dot_general(kz, v[:, sl], (((0,), (0,)), ((), ())),
                                    preferred_element_type=_F32)


def _in_proj_kernel(x_ref, g_ref, b_ref, w_ref, cos_ref, sin_ref, wo32_ref, wup32_ref, wdn32_ref,
                    u_ref, q_ref, k_ref, v_ref, sg_ref, wo16_ref, wup16_ref, wdn16_ref):
    wo16_ref[...] = wo32_ref[...].astype(_BF)
    wup16_ref[...] = wup32_ref[...].astype(_BF)
    wdn16_ref[...] = wdn32_ref[...].astype(_BF)

    pair = pl.program_id(1)
    halves = TM_PROJ // SUB_PROJ

    def where(s):
        return s // halves, (s % halves) * SUB_PROJ

    def normed(s):
        bb, r0 = where(s)
        return _layer_norm(x_ref[bb, r0:r0 + SUB_PROJ, :], g_ref[...], b_ref[...]).astype(_BF)

    def project(s, hn, hn_next):
        bb, r0 = where(s)
        rows = slice(r0, r0 + SUB_PROJ)
        cos2 = cos_ref[rows, :]
        sin2 = sin_ref[rows, :]
        u = _dot(hn, w_ref[:, 0:S5_WIDTH])
        start = pair * BPS_PROJ + bb + r0 * BATCH
        for j in range(S5_SLABS):
            u_ref[j, pl.ds(start, SUB_PROJ, stride=BATCH), :] = u[:, j * LANES:(j + 1) * LANES]
        q = _dot(hn, w_ref[:, _OFF_Q:_OFF_K])
        nxt = None if hn_next is None else hn_next()
        for h in range(RET_HEADS):
            sl = slice(h * RET_HEAD_DIM, (h + 1) * RET_HEAD_DIM)
            q_ref[bb, rows, sl] = _rope_head(q[:, sl], cos2, sin2).astype(_BF)
        k = _dot(hn, w_ref[:, _OFF_K:_OFF_V])
        for h in range(RET_HEADS):
            sl = slice(h * RET_HEAD_DIM, (h + 1) * RET_HEAD_DIM)
            k_ref[bb, rows, sl] = (_rope_head(k[:, sl], cos2, sin2)
                                   * (RET_HEAD_DIM ** -0.5)).astype(_BF)
        g = _dot(hn, w_ref[:, _OFF_G:IN_PROJ_WIDTH])
        sg_ref[bb, rows, :] = (g * jax.nn.sigmoid(g)).astype(_BF)
        v_ref[bb, rows, :] = _dot(hn, w_ref[:, _OFF_V:_OFF_G]).astype(_BF)
        return nxt

    n_sub = BPS_PROJ * halves
    hn = normed(0)
    for s in range(n_sub):
        nxt = (lambda s=s: normed(s + 1)) if s + 1 < n_sub else None
        hn = project(s, hn, nxt)


def _s5_kernel(u_ref, init_ref, bblk_ref, cblk_ref, ar_ref, ai_ref, d_ref, wglu_ref, bglu_ref,
               y_ref, bu0_scr, bu1_scr, xb0_scr, xb1_scr, acc0_scr, acc1_scr, st_scr):
    @pl.when(pl.program_id(0) == 0)
    def _():
        st_scr[...] = jnp.broadcast_to(init_ref[...], (BATCH, 2 * S5_NSTATE))

    rows_sb = SB_S5 * BATCH
    n_sb = TT_S5 // SB_S5
    n_piece = 2 * S5_NSTATE // MXU_TILE
    steps_piece = SB_S5 // n_piece
    re = slice(0, S5_NSTATE)
    im = slice(S5_NSTATE, 2 * S5_NSTATE)
    ar = jnp.broadcast_to(ar_ref[...], (BATCH, S5_NSTATE))
    ai = jnp.broadcast_to(ai_ref[...], (BATCH, S5_NSTATE))

    def rows_of(j):
        if isinstance(j, int):
            return pl.ds(j * rows_sb, rows_sb)
        return pl.ds(pl.multiple_of(j * rows_sb, rows_sb), rows_sb)

    def load_u(j):
        return jnp.concatenate([u_ref[s, rows_of(j), :] for s in range(S5_SLABS)], axis=1)

    bu_scr = (bu0_scr, bu1_scr)
    xb_scr = (xb0_scr, xb1_scr)
    acc_scr = (acc0_scr, acc1_scr)

    def a_piece(ub, slot, c):
        cols = slice(c * MXU_TILE, (c + 1) * MXU_TILE)
        bu_scr[slot][:, cols] = _dot(ub, bblk_ref[:, cols])

    def b_piece(slot, c, xr, xi):
        for t in range(c * steps_piece, (c + 1) * steps_piece, 2):
            out_r, out_i = [], []
            for tt in (t, t + 1):
                rows = slice(tt * BATCH, (tt + 1) * BATCH)
                br = bu_scr[slot][rows, re]
                bi = bu_scr[slot][rows, im]
                xr, xi = ar * xr - ai * xi + br, ar * xi + ai * xr + bi
                out_r.append(xr)
                out_i.append(xi)
            rows2 = slice(t * BATCH, (t + 2) * BATCH)
            xb_scr[slot][rows2, re] = jnp.concatenate(out_r, axis=0).astype(_BF)
            xb_scr[slot][rows2, im] = jnp.concatenate(out_i, axis=0).astype(_BF)
        return xr, xi

    def c_piece(slot, c, acc):
        cols = slice(c * MXU_TILE, (c + 1) * MXU_TILE)
        part = _dot(xb_scr[slot][:, cols], cblk_ref[cols, :])
        return part if acc is None else acc + part

    def d_piece(j, slot, half):
        lo = half * (rows_sb // 2)
        if isinstance(j, int):
            rows = pl.ds(j * rows_sb + lo, rows_sb // 2)
        else:
            rows = pl.ds(pl.multiple_of(j * rows_sb + lo, rows_sb // 2), rows_sb // 2)
        u = jnp.concatenate([u_ref[s, rows, :] for s in range(S5_SLABS)], axis=1)
        y = jax.nn.gelu(acc_scr[slot][lo:lo + rows_sb // 2, :] + d_ref[...] * u)
        gate = jax.nn.sigmoid(_dot(y.astype(_BF), wglu_ref[...]) + bglu_ref[...])
        y = y * gate
        for s in range(S5_SLABS):
            y_ref[s, rows, :] = y[:, s * LANES:(s + 1) * LANES]

    def iteration(j, slot, xr, xi, do_a, do_b, do_c, do_d):
        other = 1 - slot
        ub = load_u(j + 1).astype(_BF) if do_a else None
        acc = None
        for c in range(n_piece):
            if do_a:
                a_piece(ub, other, c)
            if do_b:
                xr, xi = b_piece(slot, c, xr, xi)
            if do_c:
                acc = c_piece(other, c, acc)
            if do_d and c % (n_piece // 2) == 1:
                d_piece(j - 2, slot, c // (n_piece // 2))
        if do_c:
            acc_scr[other][...] = acc
        return xr, xi

    ub0 = load_u(0).astype(_BF)
    for c in range(n_piece):
        a_piece(ub0, 0, c)
    xr, xi = iteration(0, 0, st_scr[:, re], st_scr[:, im], True, True, False, False)
    xr, xi = iteration(1, 1, xr, xi, True, True, True, False)

    def pair(p, carry):
        j = 2 * p + 2
        xr, xi = iteration(j, 0, carry[0], carry[1], True, True, True, True)
        return iteration(j + 1, 1, xr, xi, True, True, True, True)

    xr, xi = lax.fori_loop(0, (n_sb - 4) // 2, pair, (xr, xi))
    xr, xi = iteration(n_sb - 2, 0, xr, xi, True, True, True, True)
    xr, xi = iteration(n_sb - 1, 1, xr, xi, False, True, True, True)
    st_scr[:, re] = xr
    st_scr[:, im] = xi
    iteration(n_sb, 0, xr, xi, False, False, True, True)
    iteration(n_sb + 1, 1, xr, xi, False, False, False, True)


def _ret_kernel(q_ref, k_ref, v_ref, sg_ref, s0_ref, dmat_ref, xi_ref, zeta_ref, gc_ref,
                gng_ref, gnb_ref, y_ref, s_scr):
    @pl.when(pl.program_id(1) == 0)
    def _():
        s_scr[...] = s0_ref[...]

    heads = [slice(h * RET_HEAD_DIM, (h + 1) * RET_HEAD_DIM) for h in range(RET_HEADS)]
    for c in range(TR_RET // CHUNK):
        rows = slice(c * CHUNK, (c + 1) * CHUNK)
        scores, cross = [], []
        for h, cols in enumerate(heads):
            qh = q_ref[rows, cols]
            kh = k_ref[rows, cols]
            state = s_scr[h]
            scores.append(lax.dot_general(qh, kh, (((1,), (1,)), ((), ())),
                                          preferred_element_type=_F32))
            cross.append(_dot(qh, state.astype(_BF)))
            kz = (kh.astype(_F32) * zeta_ref[h]).astype(_BF)
            s_scr[h] = gc_ref[h] * state + lax.dot_general(
                kz, v_ref[rows, cols], (((0,), (0,)), ((), ())), preferred_element_type=_F32)
        outs = []
        for h, cols in enumerate(heads):
            p = (scores[h] * dmat_ref[h]).astype(_BF)
            outs.append(_dot(p, v_ref[rows, cols]) + cross[h] * xi_ref[h])
        for h, cols in enumerate(heads):
            o = outs[h]
            mu = jnp.mean(o, axis=-1, keepdims=True)
            oc = o - mu
            var = jnp.mean(oc * oc, axis=-1, keepdims=True)
            on = oc * lax.rsqrt(var + GN_EPS) * gng_ref[:, cols] + gnb_ref[:, cols]
            y_ref[rows, cols] = (sg_ref[rows, cols].astype(_F32) * on).astype(_BF)


def _ffn_kernel(x_ref, ys5_ref, yret_ref, lig_ref, lib_ref, wo_ref, l1g_ref, l1b_ref,
                wup_ref, wdn_ref, l2g_ref, l2b_ref, o_ref):
    pair = pl.program_id(1)
    n_ch = D_FF // FF_CHUNK
    piece = SUB_FFN // (2 * n_ch)
    halves = TM_FFN // SUB_FFN
    n_sub = BPS_FFN * halves

    def where(t):
        return t // halves, (t % halves) * SUB_FFN

    def mixed_of(t):
        bb, r0 = where(t)
        start = pair * BPS_FFN + bb + r0 * BATCH
        ys5 = jnp.concatenate(
            [ys5_ref[j, pl.ds(start, SUB_FFN, stride=BATCH), :] for j in range(S5_SLABS)],
            axis=1).astype(_BF)
        return (_dot(ys5, wo_ref[0:S5_WIDTH, :])
                + _dot(yret_ref[bb, r0:r0 + SUB_FFN, :], wo_ref[S5_WIDTH:D_MODEL, :]))

    def prep_rows(t, mixed, lo, hi):
        bb, r0 = where(t)
        h = _layer_norm(x_ref[bb, r0 + lo:r0 + hi, :], lig_ref[...], lib_ref[...])
        return _layer_norm(DEEPNORM_ALPHA * h + mixed[lo:hi], l1g_ref[...], l1b_ref[...])

    def ffn_up(h1b, c, floor):
        up = jnp.maximum(_dot(h1b, wup_ref[:, c * FF_CHUNK:(c + 1) * FF_CHUNK]), floor)
        return (up * up).astype(_BF)

    def ffn_down(act, c):
        return _dot(act, wdn_ref[c * FF_CHUNK:(c + 1) * FF_CHUNK, :])

    def finish_rows(t, pre, lo, hi):
        bb, r0 = where(t)
        out = _layer_norm(pre[lo:hi], l2g_ref[...], l2b_ref[...])
        o_ref[bb, r0 + lo:r0 + hi, :] = out
        return out

    h1 = prep_rows(0, mixed_of(0), 0, SUB_FFN)
    pre_prev = None
    floor = 0.0
    for t in range(n_sub):
        h1b = h1.astype(_BF)
        pre = DEEPNORM_ALPHA * h1
        mixed_next = mixed_of(t + 1) if t + 1 < n_sub else None
        next_parts = []

        def side_work(k):
            lo, hi = k * piece, (k + 1) * piece
            done = []
            if mixed_next is not None:
                next_parts.append(prep_rows(t + 1, mixed_next, lo, hi))
                done.append(next_parts[-1])
            if pre_prev is not None:
                done.append(finish_rows(t - 1, pre_prev, lo, hi))
            return done

        for c in range(n_ch):
            act = ffn_up(h1b, c, floor)
            done = side_work(2 * c)
            pre = pre + ffn_down(act, c)
            done += side_work(2 * c + 1)
            floor = _zero_after(done)
        if next_parts:
            h1 = jnp.concatenate(next_parts, axis=0)
        pre_prev = pre
    finish_rows(n_sub - 1, pre_prev, 0, SUB_FFN)


def _row(v):
    return v.reshape(1, -1).astype(_F32)


@functools.lru_cache(maxsize=None)
def _position_tables():
    f32 = np.float32

    def fn(f, a):
        return f(a.astype(np.float64)).astype(f32)

    pos = np.arange(N_META + SEQ, dtype=f32)
    expo = np.arange(0, RET_HEAD_DIM, 2, dtype=f32) / f32(RET_HEAD_DIM)
    inv_freq = (f32(1.0) / fn(lambda e: np.power(ROPE_BASE, e), expo)).astype(f32)
    ang = pos[:, None] * inv_freq[None, :]
    cos, sin = fn(np.cos, ang), fn(np.sin, ang)
    cos2 = np.concatenate([cos, cos], axis=1)
    sin2 = np.concatenate([-sin, sin], axis=1)

    heads = np.arange(RET_HEADS, dtype=f32)
    log_gamma = fn(np.log1p, -fn(np.exp2, f32(-5.0) - heads))
    idx = np.arange(CHUNK, dtype=f32)
    diff = idx[:, None] - idx[None, :]
    dmat = np.where(diff[None] >= 0,
                    fn(np.exp, np.maximum(diff, f32(0.0))[None] * log_gamma[:, None, None]),
                    f32(0.0)).astype(f32)
    zeta = fn(np.exp, (f32(CHUNK - 1.0) - idx)[None] * log_gamma[:, None])
    xi = fn(np.exp, (idx + f32(1.0))[None] * log_gamma[:, None])
    gamma_chunk = fn(np.exp, f32(CHUNK) * log_gamma)
    hd = RET_HEAD_DIM
    zeta_b = np.ascontiguousarray(np.broadcast_to(zeta[:, :, None], (RET_HEADS, CHUNK, hd)))
    xi_b = np.ascontiguousarray(np.broadcast_to(xi[:, :, None], (RET_HEADS, CHUNK, hd)))
    gc_b = np.ascontiguousarray(np.broadcast_to(gamma_chunk[:, None, None], (RET_HEADS, 1, hd)))
    return cos2, sin2, dmat, zeta_b, xi_b, gc_b


def kernel(x, meta_tokens, ln_in_g, ln_in_b, w_in, s5_lambda_re, s5_lambda_im, s5_log_dt, s5_b_re, s5_b_im, s5_c_re, s5_c_im, s5_d, s5_w_glu, s5_b_glu, ret_gn_g, ret_gn_b, w_out, ln1_g, ln1_b, w_up, w_down, ln2_g, ln2_b):
    assert x.shape == (BATCH, SEQ, D_MODEL) and w_in.shape[0] == 1
    G, P, H = S5_GROUPS, S5_STATE, S5_GROUP_CH
    nrows = BATCH * SEQ
    arb2 = pltpu.CompilerParams(dimension_semantics=("arbitrary", "arbitrary"),
                                vmem_limit_bytes=VMEM_LIMIT)
    arb1 = pltpu.CompilerParams(dimension_semantics=("arbitrary",), vmem_limit_bytes=VMEM_LIMIT)

    lam_re, lam_im = s5_lambda_re[0], s5_lambda_im[0]
    dt = jnp.exp(s5_log_dt[0])[:, None]
    mag = jnp.exp(lam_re * dt)
    lbr = mag * jnp.cos(lam_im * dt)
    lbi = mag * jnp.sin(lam_im * dt)
    den = lam_re * lam_re + lam_im * lam_im
    nr = lbr - 1.0
    qr = (nr * lam_re + lbi * lam_im) / den
    qi = (lbi * lam_re - nr * lam_im) / den
    bbr = qr[..., None] * s5_b_re[0] - qi[..., None] * s5_b_im[0]
    bbi = qr[..., None] * s5_b_im[0] + qi[..., None] * s5_b_re[0]
    eye = jnp.eye(G, dtype=_F32)

    def blk_in(m):
        return (eye[:, None, :, None] * m.transpose(0, 2, 1)[:, :, None, :]).reshape(G * H, G * P)

    def blk_out(m):
        return (eye[:, None, :, None] * m.transpose(0, 2, 1)[:, :, None, :]).reshape(G * P, G * H)

    bblk = jnp.concatenate([blk_in(bbr), blk_in(bbi)], axis=1).astype(_BF)
    cblk = jnp.concatenate([blk_out(s5_c_re[0]), -blk_out(s5_c_im[0])], axis=0).astype(_BF)
    ar = lbr.reshape(1, S5_NSTATE)
    ai = lbi.reshape(1, S5_NSTATE)

    hd = RET_HEAD_DIM
    cos2, sin2, dmat, zeta_b, xi_b, gc_b = _position_tables()
    zmeta_b = zeta_b[:, CHUNK - N_META:, :]

    lig, lib = _row(ln_in_g), _row(ln_in_b)

    s5_init, s0, w_in_b = pl.pallas_call(
        _meta_kernel,
        out_shape=(jax.ShapeDtypeStruct((1, 2 * S5_NSTATE), _F32),
                   jax.ShapeDtypeStruct((RET_HEADS, hd, hd), _F32),
                   jax.ShapeDtypeStruct((D_MODEL, IN_PROJ_WIDTH), _BF)),
        compiler_params=pltpu.CompilerParams(vmem_limit_bytes=VMEM_LIMIT),
        name="meta_prologue",
    )(meta_tokens.astype(_F32), lig, lib, w_in[0], bblk, ar, ai,
      cos2[:N_META], sin2[:N_META], zmeta_b)

    n_t = SEQ // TM_PROJ
    rows_spec = lambda w: pl.BlockSpec((BPS_PROJ, TM_PROJ, w), lambda i, p: (p, i, 0))
    ret_shape = jax.ShapeDtypeStruct((BATCH, SEQ, RET_WIDTH), _BF)
    n_pairs = BATCH // BPS_PROJ
    n_steps = n_t * n_pairs

    def slab_spec(shape):
        return pl.BlockSpec((shape[0] // n_steps, shape[1]), lambda i, p: (i * n_pairs + p, 0))

    w_shapes = [(D_MODEL, D_MODEL), (D_MODEL, D_FF), (D_FF, D_MODEL)]
    u_tm, q, k, v, sg, w_out_b, w_up_b, w_dn_b = pl.pallas_call(
        _in_proj_kernel,
        grid=(n_t, n_pairs),
        in_specs=[rows_spec(D_MODEL), _const_spec((1, D_MODEL)), _const_spec((1, D_MODEL)),
                  _const_spec((D_MODEL, IN_PROJ_WIDTH)),
                  pl.BlockSpec((TM_PROJ, hd), lambda i, p: (i, 0)),
                  pl.BlockSpec((TM_PROJ, hd), lambda i, p: (i, 0))]
                 + [slab_spec(s) for s in w_shapes],
        out_specs=[pl.BlockSpec((S5_SLABS, TM_PROJ * BATCH, LANES), lambda i, p: (0, i, 0)),
                   rows_spec(RET_WIDTH), rows_spec(RET_WIDTH), rows_spec(RET_WIDTH),
                   rows_spec(RET_WIDTH)] + [slab_spec(s) for s in w_shapes],
        out_shape=(jax.ShapeDtypeStruct((S5_SLABS, SEQ * BATCH, LANES), _F32),
                   ret_shape, ret_shape, ret_shape, ret_shape)
                  + tuple(jax.ShapeDtypeStruct(s, _BF) for s in w_shapes),
        compiler_params=arb2,
        name="in_proj",
    )(x, lig, lib, w_in_b, cos2[N_META:], sin2[N_META:], w_out[0], w_up[0], w_down[0])
    q, k, v, sg = (t.reshape(nrows, RET_WIDTH) for t in (q, k, v, sg))

    rows_s5 = TT_S5 * BATCH
    ys5_tm = pl.pallas_call(
        _s5_kernel,
        grid=(SEQ // TT_S5,),
        in_specs=[pl.BlockSpec((S5_SLABS, rows_s5, LANES), lambda i: (0, i, 0)),
                  _const_spec((1, 2 * S5_NSTATE)),
                  _const_spec((S5_WIDTH, 2 * S5_NSTATE)), _const_spec((2 * S5_NSTATE, S5_WIDTH)),
                  _const_spec((1, S5_NSTATE)), _const_spec((1, S5_NSTATE)),
                  _const_spec((1, S5_WIDTH)), _const_spec((S5_WIDTH, S5_WIDTH)),
                  _const_spec((1, S5_WIDTH))],
        out_specs=pl.BlockSpec((S5_SLABS, rows_s5, LANES), lambda i: (0, i, 0)),
        out_shape=jax.ShapeDtypeStruct((S5_SLABS, SEQ * BATCH, LANES), _F32),
        scratch_shapes=[pltpu.VMEM((SB_S5 * BATCH, 2 * S5_NSTATE), _F32),
                        pltpu.VMEM((SB_S5 * BATCH, 2 * S5_NSTATE), _F32),
                        pltpu.VMEM((SB_S5 * BATCH, 2 * S5_NSTATE), _BF),
                        pltpu.VMEM((SB_S5 * BATCH, 2 * S5_NSTATE), _BF),
                        pltpu.VMEM((SB_S5 * BATCH, S5_WIDTH), _F32),
                        pltpu.VMEM((SB_S5 * BATCH, S5_WIDTH), _F32),
                        pltpu.VMEM((BATCH, 2 * S5_NSTATE), _F32)],
        compiler_params=arb1,
        name="s5_scan",
    )(u_tm, s5_init, bblk, cblk, ar, ai, _row(s5_d[0]), s5_w_glu[0].astype(_BF), _row(s5_b_glu[0]))

    n_r = SEQ // TR_RET
    ret_spec = pl.BlockSpec((TR_RET, RET_WIDTH), lambda b, c: (b * n_r + c, 0))
    tab = lambda n: _const_spec((RET_HEADS, n, hd))
    y_ret = pl.pallas_call(
        _ret_kernel,
        grid=(BATCH, n_r),
        in_specs=[ret_spec, ret_spec, ret_spec, ret_spec, tab(hd), tab(CHUNK), tab(CHUNK),
                  tab(CHUNK), tab(1), _const_spec((1, RET_WIDTH)), _const_spec((1, RET_WIDTH))],
        out_specs=ret_spec,
        out_shape=jax.ShapeDtypeStruct((nrows, RET_WIDTH), _BF),
        scratch_shapes=[pltpu.VMEM((RET_HEADS, hd, hd), _F32)],
        compiler_params=arb2,
        name="retention",
    )(q, k, v, sg, s0, dmat, xi_b, zeta_b, gc_b, _row(ret_gn_g[0]), _row(ret_gn_b[0]))

    n_f = SEQ // TM_FFN
    frow = lambda w: pl.BlockSpec((BPS_FFN, TM_FFN, w), lambda i, p: (p, i, 0))
    out = pl.pallas_call(
        _ffn_kernel,
        grid=(n_f, BATCH // BPS_FFN),
        in_specs=[frow(D_MODEL),
                  pl.BlockSpec((S5_SLABS, TM_FFN * BATCH, LANES), lambda i, p: (0, i, 0)),
                  frow(RET_WIDTH),
                  _const_spec((1, D_MODEL)), _const_spec((1, D_MODEL)),
                  _const_spec((D_MODEL, D_MODEL)),
                  _const_spec((1, D_MODEL)), _const_spec((1, D_MODEL)),
                  _const_spec((D_MODEL, D_FF)), _const_spec((D_FF, D_MODEL)),
                  _const_spec((1, D_MODEL)), _const_spec((1, D_MODEL))],
        out_specs=frow(D_MODEL),
        out_shape=jax.ShapeDtypeStruct((BATCH, SEQ, D_MODEL), _F32),
        compiler_params=arb2,
        name="out_ffn",
    )(x, ys5_tm, y_ret.reshape(BATCH, SEQ, RET_WIDTH), lig, lib, w_out_b,
      _row(ln1_g[0]), _row(ln1_b[0]), w_up_b, w_dn_b, _row(ln2_g[0]), _row(ln2_b[0]))

    return out
```

```python
import functools
import math

import jax
import jax.numpy as jnp
import numpy as np
from jax import lax
from jax.experimental import pallas as pl
from jax.experimental.pallas import tpu as pltpu

D_MODEL = 1024
BATCH = 8
SEQ = 4096
N_META = 16
S5_GROUP_CH = 16
S5_STATE = 64
S5_WIDTH = 256
S5_GROUPS = S5_WIDTH // S5_GROUP_CH
S5_NSTATE = S5_GROUPS * S5_STATE
RET_HEAD_DIM = 128
RET_WIDTH = 768
RET_HEADS = RET_WIDTH // RET_HEAD_DIM
CHUNK = 128
ROPE_BASE = 10000.0
D_FF = 4 * D_MODEL
LANES = 128
S5_SLABS = S5_WIDTH // LANES
LN_EPS = 1e-5
GN_EPS = 1e-5
IN_PROJ_WIDTH = S5_WIDTH + 4 * RET_WIDTH
DEEPNORM_ALPHA = 2.0 ** 0.25

_OFF_Q = S5_WIDTH
_OFF_K = _OFF_Q + RET_WIDTH
_OFF_V = _OFF_K + RET_WIDTH
_OFF_G = _OFF_V + RET_WIDTH

TM_PROJ = 512
BPS_PROJ = 2
SUB_PROJ = 256
MXU_TILE = 256
TT_S5 = 1024
SB_S5 = 32
BPS_RET = 2
TR_RET = 512
TM_FFN = 512
BPS_FFN = 2
SUB_FFN = 256
FF_CHUNK = 1024
VMEM_LIMIT = 56 * 1024 * 1024

_BF = jnp.bfloat16
_F32 = jnp.float32


def _const_spec(shape):
    nd = len(shape)
    return pl.BlockSpec(shape, lambda *_: (0,) * nd, pipeline_mode=pl.Buffered(1))


def _layer_norm(x, g, b, after=()):
    mu = jnp.mean(x, axis=-1, keepdims=True)
    if after:
        mu = mu + _zero_after(list(after))
    xc = x - mu
    var = jnp.mean(xc * xc, axis=-1, keepdims=True)
    return xc * lax.rsqrt(var + LN_EPS) * g + b


def _dot(a, b):
    return jnp.dot(a, b, preferred_element_type=_F32)


def _zero_after(values):
    if not values:
        return 0.0
    tok = values[0][0:1, 0:1]
    for v in values[1:]:
        tok = tok + v[0:1, 0:1]
    bits = lax.shift_right_logical(lax.shift_right_logical(tok.astype(jnp.int32), 16), 16)
    return bits.astype(_F32)


def _rope_head(t, cos2, sin2):
    return t * cos2 + pltpu.roll(t, RET_HEAD_DIM // 2, 1) * sin2


def _meta_kernel(meta_ref, g_ref, b_ref, w32_ref, bblk_ref, ar_ref, ai_ref, cos_ref, sin_ref,
                 zmeta_ref, s5_ref, s0_ref, w_ref):
    slab = D_MODEL // 8
    for r in range(0, D_MODEL, slab):
        w_ref[r:r + slab, :] = w32_ref[r:r + slab, :].astype(_BF)
    hm = _layer_norm(meta_ref[...], g_ref[...], b_ref[...]).astype(_BF)
    u = _dot(hm, w_ref[:, 0:S5_WIDTH])
    bu = _dot(u.astype(_BF), bblk_ref[...])
    ar = ar_ref[...]
    ai = ai_ref[...]
    xr = jnp.zeros((1, S5_NSTATE), _F32)
    xi = jnp.zeros((1, S5_NSTATE), _F32)
    for t in range(N_META):
        br = bu[t:t + 1, 0:S5_NSTATE]
        bi = bu[t:t + 1, S5_NSTATE:2 * S5_NSTATE]
        xr, xi = ar * xr - ai * xi + br, ar * xi + ai * xr + bi
    s5_ref[:, 0:S5_NSTATE] = xr
    s5_ref[:, S5_NSTATE:2 * S5_NSTATE] = xi

    k = _dot(hm, w_ref[:, _OFF_K:_OFF_V])
    v = _dot(hm, w_ref[:, _OFF_V:_OFF_G]).astype(_BF)
    cos2 = cos_ref[...]
    sin2 = sin_ref[...]
    for h in range(RET_HEADS):
        sl = slice(h * RET_HEAD_DIM, (h + 1) * RET_HEAD_DIM)
        kh = _rope_head(k[:, sl], cos2, sin2) * (RET_HEAD_DIM ** -0.5)
        kz = (kh * zmeta_ref[h]).astype(_BF)
        s0_ref[h] = lax.dot_general(kz, v[:, sl], (((0,), (0,)), ((), ())),
                                    preferred_element_type=_F32)


def _in_proj_kernel(x_ref, g_ref, b_ref, w_ref, cos_ref, sin_ref, wo32_ref, wup32_ref, wdn32_ref,
                    u_ref, q_ref, k_ref, v_ref, sg_ref, wo16_ref, wup16_ref, wdn16_ref):
    wo16_ref[...] = wo32_ref[...].astype(_BF)
    wup16_ref[...] = wup32_ref[...].astype(_BF)
    wdn16_ref[...] = wdn32_ref[...].astype(_BF)

    pair = pl.program_id(1)
    halves = TM_PROJ // SUB_PROJ

    def where(s):
        return s // halves, (s % halves) * SUB_PROJ

    def normed(s):
        bb, r0 = where(s)
        return _layer_norm(x_ref[bb, r0:r0 + SUB_PROJ, :], g_ref[...], b_ref[...]).astype(_BF)

    def project(s, hn, hn_next):
        bb, r0 = where(s)
        rows = slice(r0, r0 + SUB_PROJ)
        cos2 = cos_ref[rows, :]
        sin2 = sin_ref[rows, :]
        u = _dot(hn, w_ref[:, 0:S5_WIDTH])
        start = pair * BPS_PROJ + bb + r0 * BATCH
        for j in range(S5_SLABS):
            u_ref[j, pl.ds(start, SUB_PROJ, stride=BATCH), :] = u[:, j * LANES:(j + 1) * LANES]
        q = _dot(hn, w_ref[:, _OFF_Q:_OFF_K])
        nxt = None if hn_next is None else hn_next()
        for h in range(RET_HEADS):
            sl = slice(h * RET_HEAD_DIM, (h + 1) * RET_HEAD_DIM)
            q_ref[bb, rows, sl] = _rope_head(q[:, sl], cos2, sin2).astype(_BF)
        k = _dot(hn, w_ref[:, _OFF_K:_OFF_V])
        for h in range(RET_HEADS):
            sl = slice(h * RET_HEAD_DIM, (h + 1) * RET_HEAD_DIM)
            k_ref[bb, rows, sl] = (_rope_head(k[:, sl], cos2, sin2)
                                   * (RET_HEAD_DIM ** -0.5)).astype(_BF)
        g = _dot(hn, w_ref[:, _OFF_G:IN_PROJ_WIDTH])
        sg_ref[bb, rows, :] = (g * jax.nn.sigmoid(g)).astype(_BF)
        v_ref[bb, rows, :] = _dot(hn, w_ref[:, _OFF_V:_OFF_G]).astype(_BF)
        return nxt

    n_sub = BPS_PROJ * halves
    hn = normed(0)
    for s in range(n_sub):
        nxt = (lambda s=s: normed(s + 1)) if s + 1 < n_sub else None
        hn = project(s, hn, nxt)


def _s5_kernel(u_ref, init_ref, bblk_ref, cblk_ref, ar_ref, ai_ref, d_ref, wglu_ref, bglu_ref,
               y_ref, bu0_scr, bu1_scr, xb0_scr, xb1_scr, acc0_scr, acc1_scr, st_scr):
    @pl.when(pl.program_id(0) == 0)
    def _():
        st_scr[...] = jnp.broadcast_to(init_ref[...], (BATCH, 2 * S5_NSTATE))

    rows_sb = SB_S5 * BATCH
    n_sb = TT_S5 // SB_S5
    n_piece = 2 * S5_NSTATE // MXU_TILE
    steps_piece = SB_S5 // n_piece
    re = slice(0, S5_NSTATE)
    im = slice(S5_NSTATE, 2 * S5_NSTATE)
    ar = jnp.broadcast_to(ar_ref[...], (BATCH, S5_NSTATE))
    ai = jnp.broadcast_to(ai_ref[...], (BATCH, S5_NSTATE))

    def rows_of(j):
        if isinstance(j, int):
            return pl.ds(j * rows_sb, rows_sb)
        return pl.ds(pl.multiple_of(j * rows_sb, rows_sb), rows_sb)

    def load_u(j):
        return jnp.concatenate([u_ref[s, rows_of(j), :] for s in range(S5_SLABS)], axis=1)

    bu_scr = (bu0_scr, bu1_scr)
    xb_scr = (xb0_scr, xb1_scr)
    acc_scr = (acc0_scr, acc1_scr)

    def a_piece(ub, slot, c):
        cols = slice(c * MXU_TILE, (c + 1) * MXU_TILE)
        bu_scr[slot][:, cols] = _dot(ub, bblk_ref[:, cols])

    def b_piece(slot, c, xr, xi):
        for t in range(c * steps_piece, (c + 1) * steps_piece, 2):
            out_r, out_i = [], []
            for tt in (t, t + 1):
                rows = slice(tt * BATCH, (tt + 1) * BATCH)
                br = bu_scr[slot][rows, re]
                bi = bu_scr[slot][rows, im]
                xr, xi = ar * xr - ai * xi + br, ar * xi + ai * xr + bi
                out_r.append(xr)
                out_i.append(xi)
            rows2 = slice(t * BATCH, (t + 2) * BATCH)
            xb_scr[slot][rows2, re] = jnp.concatenate(out_r, axis=0).astype(_BF)
            xb_scr[slot][rows2, im] = jnp.concatenate(out_i, axis=0).astype(_BF)
        return xr, xi

    def c_piece(slot, c, acc):
        cols = slice(c * MXU_TILE, (c + 1) * MXU_TILE)
        part = _dot(xb_scr[slot][:, cols], cblk_ref[cols, :])
        return part if acc is None else acc + part

    def d_piece(j, slot, half):
        lo = half * (rows_sb // 2)
        if isinstance(j, int):
            rows = pl.ds(j * rows_sb + lo, rows_sb // 2)
        else:
            rows = pl.ds(pl.multiple_of(j * rows_sb + lo, rows_sb // 2), rows_sb // 2)
        u = jnp.concatenate([u_ref[s, rows, :] for s in range(S5_SLABS)], axis=1)
        y = jax.nn.gelu(acc_scr[slot][lo:lo + rows_sb // 2, :] + d_ref[...] * u)
        gate = jax.nn.sigmoid(_dot(y.astype(_BF), wglu_ref[...]) + bglu_ref[...])
        y = y * gate
        for s in range(S5_SLABS):
            y_ref[s, rows, :] = y[:, s * LANES:(s + 1) * LANES]

    def iteration(j, slot, xr, xi, do_a, do_b, do_c, do_d):
        other = 1 - slot
        ub = load_u(j + 1).astype(_BF) if do_a else None
        acc = None
        for c in range(n_piece):
            if do_a:
                a_piece(ub, other, c)
            if do_b:
                xr, xi = b_piece(slot, c, xr, xi)
            if do_c:
                acc = c_piece(other, c, acc)
            if do_d and c % (n_piece // 2) == 1:
                d_piece(j - 2, slot, c // (n_piece // 2))
        if do_c:
            acc_scr[other][...] = acc
        return xr, xi

    ub0 = load_u(0).astype(_BF)
    for c in range(n_piece):
        a_piece(ub0, 0, c)
    xr, xi = iteration(0, 0, st_scr[:, re], st_scr[:, im], True, True, False, False)
    xr, xi = iteration(1, 1, xr, xi, True, True, True, False)

    def pair(p, carry):
        j = 2 * p + 2
        xr, xi = iteration(j, 0, carry[0], carry[1], True, True, True, True)
        return iteration(j + 1, 1, xr, xi, True, True, True, True)

    xr, xi = lax.fori_loop(0, (n_sb - 4) // 2, pair, (xr, xi))
    xr, xi = iteration(n_sb - 2, 0, xr, xi, True, True, True, True)
    xr, xi = iteration(n_sb - 1, 1, xr, xi, False, True, True, True)
    st_scr[:, re] = xr
    st_scr[:, im] = xi
    iteration(n_sb, 0, xr, xi, False, False, True, True)
    iteration(n_sb + 1, 1, xr, xi, False, False, False, True)


def _ret_kernel(q_ref, k_ref, v_ref, sg_ref, s0_ref, dmat_ref, xi_ref, zeta_ref, gc_ref,
                gng_ref, gnb_ref, y_ref, s_scr):
    @pl.when(pl.program_id(1) == 0)
    def _():
        for bb in range(BPS_RET):
            s_scr[bb] = s0_ref[...]

    units = [(bb, h, slice(h * RET_HEAD_DIM, (h + 1) * RET_HEAD_DIM))
             for bb in range(BPS_RET) for h in range(RET_HEADS)]
    for c in range(TR_RET // CHUNK):
        rows = slice(c * CHUNK, (c + 1) * CHUNK)
        scores, cross = [], []
        for bb, h, cols in units:
            qh = q_ref[bb, rows, cols]
            kh = k_ref[bb, rows, cols]
            state = s_scr[bb, h]
            scores.append(lax.dot_general(qh, kh, (((1,), (1,)), ((), ())),
                                          preferred_element_type=_F32))
            cross.append(_dot(qh, state.astype(_BF)))
            kz = (kh.astype(_F32) * zeta_ref[h]).astype(_BF)
            s_scr[bb, h] = gc_ref[h] * state + lax.dot_general(
                kz, v_ref[bb, rows, cols], (((0,), (0,)), ((), ())),
                preferred_element_type=_F32)
        outs = []
        for i, (bb, h, cols) in enumerate(units):
            p = (scores[i] * dmat_ref[h]).astype(_BF)
            outs.append(_dot(p, v_ref[bb, rows, cols]) + cross[i] * xi_ref[h])
        for i, (bb, h, cols) in enumerate(units):
            o = outs[i]
            mu = jnp.mean(o, axis=-1, keepdims=True)
            oc = o - mu
            var = jnp.mean(oc * oc, axis=-1, keepdims=True)
            on = oc * lax.rsqrt(var + GN_EPS) * gng_ref[:, cols] + gnb_ref[:, cols]
            y_ref[bb, rows, cols] = (sg_ref[bb, rows, cols].astype(_F32) * on).astype(_BF)


def _ffn_kernel(x_ref, ys5_ref, yret_ref, lig_ref, lib_ref, wo_ref, l1g_ref, l1b_ref,
                wup_ref, wdn_ref, l2g_ref, l2b_ref, o_ref):
    pair = pl.program_id(1)
    n_ch = D_FF // FF_CHUNK
    piece = SUB_FFN // (2 * n_ch)
    halves = TM_FFN // SUB_FFN
    n_sub = BPS_FFN * halves

    def where(t):
        return t // halves, (t % halves) * SUB_FFN

    def mixed_of(t):
        bb, r0 = where(t)
        start = pair * BPS_FFN + bb + r0 * BATCH
        ys5 = jnp.concatenate(
            [ys5_ref[j, pl.ds(start, SUB_FFN, stride=BATCH), :] for j in range(S5_SLABS)],
            axis=1).astype(_BF)
        return (_dot(ys5, wo_ref[0:S5_WIDTH, :])
                + _dot(yret_ref[bb, r0:r0 + SUB_FFN, :], wo_ref[S5_WIDTH:D_MODEL, :]))

    def prep_rows(t, mixed, lo, hi):
        bb, r0 = where(t)
        h = _layer_norm(x_ref[bb, r0 + lo:r0 + hi, :], lig_ref[...], lib_ref[...])
        return _layer_norm(DEEPNORM_ALPHA * h + mixed[lo:hi], l1g_ref[...], l1b_ref[...])

    def ffn_up(h1b, c, floor):
        up = jnp.maximum(_dot(h1b, wup_ref[:, c * FF_CHUNK:(c + 1) * FF_CHUNK]), floor)
        return (up * up).astype(_BF)

    def ffn_down(act, c):
        return _dot(act, wdn_ref[c * FF_CHUNK:(c + 1) * FF_CHUNK, :])

    def finish_rows(t, pre, lo, hi):
        bb, r0 = where(t)
        out = _layer_norm(pre[lo:hi], l2g_ref[...], l2b_ref[...])
        o_ref[bb, r0 + lo:r0 + hi, :] = out
        return out

    h1 = prep_rows(0, mixed_of(0), 0, SUB_FFN)
    pre_prev = None
    floor = 0.0
    for t in range(n_sub):
        h1b = h1.astype(_BF)
        pre = DEEPNORM_ALPHA * h1
        mixed_next = mixed_of(t + 1) if t + 1 < n_sub else None
        next_parts = []

        def side_work(k):
            lo, hi = k * piece, (k + 1) * piece
            done = []
            if mixed_next is not None:
                next_parts.append(prep_rows(t + 1, mixed_next, lo, hi))
                done.append(next_parts[-1])
            if pre_prev is not None:
                done.append(finish_rows(t - 1, pre_prev, lo, hi))
            return done

        for c in range(n_ch):
            act = ffn_up(h1b, c, floor)
            done = side_work(2 * c)
            pre = pre + ffn_down(act, c)
            done += side_work(2 * c + 1)
            floor = _zero_after(done)
        if next_parts:
            h1 = jnp.concatenate(next_parts, axis=0)
        pre_prev = pre
    finish_rows(n_sub - 1, pre_prev, 0, SUB_FFN)


def _row(v):
    return v.reshape(1, -1).astype(_F32)


@functools.lru_cache(maxsize=None)
def _position_tables():
    f32 = np.float32

    def fn(f, a):
        return f(a.astype(np.float64)).astype(f32)

    pos = np.arange(N_META + SEQ, dtype=f32)
    expo = np.arange(0, RET_HEAD_DIM, 2, dtype=f32) / f32(RET_HEAD_DIM)
    inv_freq = (f32(1.0) / fn(lambda e: np.power(ROPE_BASE, e), expo)).astype(f32)
    ang = pos[:, None] * inv_freq[None, :]
    cos, sin = fn(np.cos, ang), fn(np.sin, ang)
    cos2 = np.concatenate([cos, cos], axis=1)
    sin2 = np.concatenate([-sin, sin], axis=1)

    heads = np.arange(RET_HEADS, dtype=f32)
    log_gamma = fn(np.log1p, -fn(np.exp2, f32(-5.0) - heads))
    idx = np.arange(CHUNK, dtype=f32)
    diff = idx[:, None] - idx[None, :]
    dmat = np.where(diff[None] >= 0,
                    fn(np.exp, np.maximum(diff, f32(0.0))[None] * log_gamma[:, None, None]),
                    f32(0.0)).astype(f32)
    zeta = fn(np.exp, (f32(CHUNK - 1.0) - idx)[None] * log_gamma[:, None])
    xi = fn(np.exp, (idx + f32(1.0))[None] * log_gamma[:, None])
    gamma_chunk = fn(np.exp, f32(CHUNK) * log_gamma)
    hd = RET_HEAD_DIM
    zeta_b = np.ascontiguousarray(np.broadcast_to(zeta[:, :, None], (RET_HEADS, CHUNK, hd)))
    xi_b = np.ascontiguousarray(np.broadcast_to(xi[:, :, None], (RET_HEADS, CHUNK, hd)))
    gc_b = np.ascontiguousarray(np.broadcast_to(gamma_chunk[:, None, None], (RET_HEADS, 1, hd)))
    return cos2, sin2, dmat, zeta_b, xi_b, gc_b


def kernel(x, meta_tokens, ln_in_g, ln_in_b, w_in, s5_lambda_re, s5_lambda_im, s5_log_dt, s5_b_re, s5_b_im, s5_c_re, s5_c_im, s5_d, s5_w_glu, s5_b_glu, ret_gn_g, ret_gn_b, w_out, ln1_g, ln1_b, w_up, w_down, ln2_g, ln2_b):
    assert x.shape == (BATCH, SEQ, D_MODEL) and w_in.shape[0] == 1
    G, P, H = S5_GROUPS, S5_STATE, S5_GROUP_CH
    nrows = BATCH * SEQ
    arb2 = pltpu.CompilerParams(dimension_semantics=("arbitrary", "arbitrary"),
                                vmem_limit_bytes=VMEM_LIMIT)
    arb1 = pltpu.CompilerParams(dimension_semantics=("arbitrary",), vmem_limit_bytes=VMEM_LIMIT)

    lam_re, lam_im = s5_lambda_re[0], s5_lambda_im[0]
    dt = jnp.exp(s5_log_dt[0])[:, None]
    mag = jnp.exp(lam_re * dt)
    lbr = mag * jnp.cos(lam_im * dt)
    lbi = mag * jnp.sin(lam_im * dt)
    den = lam_re * lam_re + lam_im * lam_im
    nr = lbr - 1.0
    qr = (nr * lam_re + lbi * lam_im) / den
    qi = (lbi * lam_re - nr * lam_im) / den
    bbr = qr[..., None] * s5_b_re[0] - qi[..., None] * s5_b_im[0]
    bbi = qr[..., None] * s5_b_im[0] + qi[..., None] * s5_b_re[0]
    eye = jnp.eye(G, dtype=_F32)

    def blk_in(m):
        return (eye[:, None, :, None] * m.transpose(0, 2, 1)[:, :, None, :]).reshape(G * H, G * P)

    def blk_out(m):
        return (eye[:, None, :, None] * m.transpose(0, 2, 1)[:, :, None, :]).reshape(G * P, G * H)

    bblk = jnp.concatenate([blk_in(bbr), blk_in(bbi)], axis=1).astype(_BF)
    cblk = jnp.concatenate([blk_out(s5_c_re[0]), -blk_out(s5_c_im[0])], axis=0).astype(_BF)
    ar = lbr.reshape(1, S5_NSTATE)
    ai = lbi.reshape(1, S5_NSTATE)

    hd = RET_HEAD_DIM
    cos2, sin2, dmat, zeta_b, xi_b, gc_b = _position_tables()
    zmeta_b = zeta_b[:, CHUNK - N_META:, :]

    lig, lib = _row(ln_in_g), _row(ln_in_b)

    s5_init, s0, w_in_b = pl.pallas_call(
        _meta_kernel,
        out_shape=(jax.ShapeDtypeStruct((1, 2 * S5_NSTATE), _F32),
                   jax.ShapeDtypeStruct((RET_HEADS, hd, hd), _F32),
                   jax.ShapeDtypeStruct((D_MODEL, IN_PROJ_WIDTH), _BF)),
        compiler_params=pltpu.CompilerParams(vmem_limit_bytes=VMEM_LIMIT),
        name="meta_prologue",
    )(meta_tokens.astype(_F32), lig, lib, w_in[0], bblk, ar, ai,
      cos2[:N_META], sin2[:N_META], zmeta_b)

    n_t = SEQ // TM_PROJ
    rows_spec = lambda w: pl.BlockSpec((BPS_PROJ, TM_PROJ, w), lambda i, p: (p, i, 0))
    ret_shape = jax.ShapeDtypeStruct((BATCH, SEQ, RET_WIDTH), _BF)
    n_pairs = BATCH // BPS_PROJ
    n_steps = n_t * n_pairs

    def slab_spec(shape):
        return pl.BlockSpec((shape[0] // n_steps, shape[1]), lambda i, p: (i * n_pairs + p, 0))

    w_shapes = [(D_MODEL, D_MODEL), (D_MODEL, D_FF), (D_FF, D_MODEL)]
    u_tm, q, k, v, sg, w_out_b, w_up_b, w_dn_b = pl.pallas_call(
        _in_proj_kernel,
        grid=(n_t, n_pairs),
        in_specs=[rows_spec(D_MODEL), _const_spec((1, D_MODEL)), _const_spec((1, D_MODEL)),
                  _const_spec((D_MODEL, IN_PROJ_WIDTH)),
                  pl.BlockSpec((TM_PROJ, hd), lambda i, p: (i, 0)),
                  pl.BlockSpec((TM_PROJ, hd), lambda i, p: (i, 0))]
                 + [slab_spec(s) for s in w_shapes],
        out_specs=[pl.BlockSpec((S5_SLABS, TM_PROJ * BATCH, LANES), lambda i, p: (0, i, 0)),
                   rows_spec(RET_WIDTH), rows_spec(RET_WIDTH), rows_spec(RET_WIDTH),
                   rows_spec(RET_WIDTH)] + [slab_spec(s) for s in w_shapes],
        out_shape=(jax.ShapeDtypeStruct((S5_SLABS, SEQ * BATCH, LANES), _F32),
                   ret_shape, ret_shape, ret_shape, ret_shape)
                  + tuple(jax.ShapeDtypeStruct(s, _BF) for s in w_shapes),
        compiler_params=arb2,
        name="in_proj",
    )(x, lig, lib, w_in_b, cos2[N_META:], sin2[N_META:], w_out[0], w_up[0], w_down[0])

    rows_s5 = TT_S5 * BATCH
    ys5_tm = pl.pallas_call(
        _s5_kernel,
        grid=(SEQ // TT_S5,),
        in_specs=[pl.BlockSpec((S5_SLABS, rows_s5, LANES), lambda i: (0, i, 0)),
                  _const_spec((1, 2 * S5_NSTATE)),
                  _const_spec((S5_WIDTH, 2 * S5_NSTATE)), _const_spec((2 * S5_NSTATE, S5_WIDTH)),
                  _const_spec((1, S5_NSTATE)), _const_spec((1, S5_NSTATE)),
                  _const_spec((1, S5_WIDTH)), _const_spec((S5_WIDTH, S5_WIDTH)),
                  _const_spec((1, S5_WIDTH))],
        out_specs=pl.BlockSpec((S5_SLABS, rows_s5, LANES), lambda i: (0, i, 0)),
        out_shape=jax.ShapeDtypeStruct((S5_SLABS, SEQ * BATCH, LANES), _F32),
        scratch_shapes=[pltpu.VMEM((SB_S5 * BATCH, 2 * S5_NSTATE), _F32),
                        pltpu.VMEM((SB_S5 * BATCH, 2 * S5_NSTATE), _F32),
                        pltpu.VMEM((SB_S5 * BATCH, 2 * S5_NSTATE), _BF),
                        pltpu.VMEM((SB_S5 * BATCH, 2 * S5_NSTATE), _BF),
                        pltpu.VMEM((SB_S5 * BATCH, S5_WIDTH), _F32),
                        pltpu.VMEM((SB_S5 * BATCH, S5_WIDTH), _F32),
                        pltpu.VMEM((BATCH, 2 * S5_NSTATE), _F32)],
        compiler_params=arb1,
        name="s5_scan",
    )(u_tm, s5_init, bblk, cblk, ar, ai, _row(s5_d[0]), s5_w_glu[0].astype(_BF), _row(s5_b_glu[0]))

    n_r = SEQ // TR_RET
    ret_spec = pl.BlockSpec((BPS_RET, TR_RET, RET_WIDTH), lambda b, c: (b, c, 0))
    tab = lambda n: _const_spec((RET_HEADS, n, hd))
    y_ret = pl.pallas_call(
        _ret_kernel,
        grid=(BATCH // BPS_RET, n_r),
        in_specs=[ret_spec, ret_spec, ret_spec, ret_spec, tab(hd), tab(CHUNK), tab(CHUNK),
                  tab(CHUNK), tab(1), _const_spec((1, RET_WIDTH)), _const_spec((1, RET_WIDTH))],
        out_specs=ret_spec,
        out_shape=jax.ShapeDtypeStruct((BATCH, SEQ, RET_WIDTH), _BF),
        scratch_shapes=[pltpu.VMEM((BPS_RET, RET_HEADS, hd, hd), _F32)],
        compiler_params=arb2,
        name="retention",
    )(q, k, v, sg, s0, dmat, xi_b, zeta_b, gc_b, _row(ret_gn_g[0]), _row(ret_gn_b[0]))

    n_f = SEQ // TM_FFN
    frow = lambda w: pl.BlockSpec((BPS_FFN, TM_FFN, w), lambda i, p: (p, i, 0))
    out = pl.pallas_call(
        _ffn_kernel,
        grid=(n_f, BATCH // BPS_FFN),
        in_specs=[frow(D_MODEL),
                  pl.BlockSpec((S5_SLABS, TM_FFN * BATCH, LANES), lambda i, p: (0, i, 0)),
                  frow(RET_WIDTH),
                  _const_spec((1, D_MODEL)), _const_spec((1, D_MODEL)),
                  _const_spec((D_MODEL, D_MODEL)),
                  _const_spec((1, D_MODEL)), _const_spec((1, D_MODEL)),
                  _const_spec((D_MODEL, D_FF)), _const_spec((D_FF, D_MODEL)),
                  _const_spec((1, D_MODEL)), _const_spec((1, D_MODEL))],
        out_specs=frow(D_MODEL),
        out_shape=jax.ShapeDtypeStruct((BATCH, SEQ, D_MODEL), _F32),
        compiler_params=arb2,
        name="out_ffn",
    )(x, ys5_tm, y_ret, lig, lib, w_out_b,
      _row(ln1_g[0]), _row(ln1_b[0]), w_up_b, w_dn_b, _row(ln2_g[0]), _row(ln2_b[0]))

    return out
```

```python
import functools

import jax
import jax.numpy as jnp
import numpy as np
from jax import lax
from jax.experimental import pallas as pl
from jax.experimental.pallas import tpu as pltpu

D_MODEL = 1024
BATCH = 8
SEQ = 4096
N_META = 16
S5_GROUP_CH = 16
S5_STATE = 64
S5_WIDTH = 256
S5_GROUPS = S5_WIDTH // S5_GROUP_CH
S5_NSTATE = S5_GROUPS * S5_STATE
RET_HEAD_DIM = 128
RET_WIDTH = 768
RET_HEADS = RET_WIDTH // RET_HEAD_DIM
CHUNK = 128
ROPE_BASE = 10000.0
D_FF = 4 * D_MODEL
LANES = 128
S5_SLABS = S5_WIDTH // LANES
LN_EPS = 1e-5
GN_EPS = 1e-5
IN_PROJ_WIDTH = S5_WIDTH + 4 * RET_WIDTH
DEEPNORM_ALPHA = 2.0 ** 0.25

_OFF_Q = S5_WIDTH
_OFF_K = _OFF_Q + RET_WIDTH
_OFF_V = _OFF_K + RET_WIDTH
_OFF_G = _OFF_V + RET_WIDTH

TM_PROJ = 512
BPS_PROJ = 2
SUB_PROJ = 256
MXU_TILE = 256
TT_S5 = 1024
SB_S5 = 32
BPS_RET = 2
TR_RET = 512
TM_FFN = 512
BPS_FFN = 2
SUB_FFN = 256
FF_CHUNK = 1024
VMEM_LIMIT = 56 * 1024 * 1024

_BF = jnp.bfloat16
_F32 = jnp.float32


def _const_spec(shape):
    nd = len(shape)
    return pl.BlockSpec(shape, lambda *_: (0,) * nd, pipeline_mode=pl.Buffered(1))


def _layer_norm(x, g, b):
    mu = jnp.mean(x, axis=-1, keepdims=True)
    xc = x - mu
    var = jnp.mean(xc * xc, axis=-1, keepdims=True)
    return xc * lax.rsqrt(var + LN_EPS) * g + b


def _dot(a, b):
    return jnp.dot(a, b, preferred_element_type=_F32)


def _zero_after(values):
    if not values:
        return 0.0
    tok = values[0][0:1, 0:1]
    for v in values[1:]:
        tok = tok + v[0:1, 0:1]
    bits = lax.shift_right_logical(lax.shift_right_logical(tok.astype(jnp.int32), 16), 16)
    return bits.astype(_F32)


def _rope_head(t, cos2, sin2):
    return t * cos2 + pltpu.roll(t, RET_HEAD_DIM // 2, 1) * sin2


def _meta_kernel(meta_ref, g_ref, b_ref, w32_ref, bblk_ref, ar_ref, ai_ref, cos_ref, sin_ref,
                 zmeta_ref, s5_ref, s0_ref, w_ref):
    slab = D_MODEL // 8
    for r in range(0, D_MODEL, slab):
        w_ref[r:r + slab, :] = w32_ref[r:r + slab, :].astype(_BF)
    hm = _layer_norm(meta_ref[...], g_ref[...], b_ref[...]).astype(_BF)
    u = _dot(hm, w_ref[:, 0:S5_WIDTH])
    bu = _dot(u.astype(_BF), bblk_ref[...])
    ar = ar_ref[...]
    ai = ai_ref[...]
    xr = jnp.zeros((1, S5_NSTATE), _F32)
    xi = jnp.zeros((1, S5_NSTATE), _F32)
    for t in range(N_META):
        br = bu[t:t + 1, 0:S5_NSTATE]
        bi = bu[t:t + 1, S5_NSTATE:2 * S5_NSTATE]
        xr, xi = ar * xr - ai * xi + br, ar * xi + ai * xr + bi
    s5_ref[:, 0:S5_NSTATE] = xr
    s5_ref[:, S5_NSTATE:2 * S5_NSTATE] = xi

    k = _dot(hm, w_ref[:, _OFF_K:_OFF_V])
    v = _dot(hm, w_ref[:, _OFF_V:_OFF_G]).astype(_BF)
    cos2 = cos_ref[...]
    sin2 = sin_ref[...]
    for h in range(RET_HEADS):
        sl = slice(h * RET_HEAD_DIM, (h + 1) * RET_HEAD_DIM)
        kh = _rope_head(k[:, sl], cos2, sin2) * (RET_HEAD_DIM ** -0.5)
        kz = (kh * zmeta_ref[h]).astype(_BF)
        s0_ref[h] = lax.dot_general(kz, v[:, sl], (((0,), (0,)), ((), ())),
                                    preferred_element_type=_F32)


def _in_proj_kernel(x_ref, g_ref, b_ref, w_ref, cos_ref, sin_ref, wo32_ref, wup32_ref, wdn32_ref,
                    u_ref, q_ref, k_ref, v_ref, sg_ref, wo16_ref, wup16_ref, wdn16_ref):
    wo16_ref[...] = wo32_ref[...].astype(_BF)
    wup16_ref[...] = wup32_ref[...].astype(_BF)
    wdn16_ref[...] = wdn32_ref[...].astype(_BF)

    pair = pl.program_id(1)
    halves = TM_PROJ // SUB_PROJ

    def where(s):
        return s // halves, (s % halves) * SUB_PROJ

    def normed(s):
        bb, r0 = where(s)
        return _layer_norm(x_ref[bb, r0:r0 + SUB_PROJ, :], g_ref[...], b_ref[...]).astype(_BF)

    def project(s, hn, hn_next):
        bb, r0 = where(s)
        rows = slice(r0, r0 + SUB_PROJ)
        cos2 = cos_ref[rows, :]
        sin2 = sin_ref[rows, :]
        u = _dot(hn, w_ref[:, 0:S5_WIDTH])
        start = pair * BPS_PROJ + bb + r0 * BATCH
        for j in range(S5_SLABS):
            u_ref[j, pl.ds(start, SUB_PROJ, stride=BATCH), :] = u[:, j * LANES:(j + 1) * LANES]
        q = _dot(hn, w_ref[:, _OFF_Q:_OFF_K])
        nxt = None if hn_next is None else hn_next()
        for h in range(RET_HEADS):
            sl = slice(h * RET_HEAD_DIM, (h + 1) * RET_HEAD_DIM)
            q_ref[bb, rows, sl] = _rope_head(q[:, sl], cos2, sin2).astype(_BF)
        k = _dot(hn, w_ref[:, _OFF_K:_OFF_V])
        for h in range(RET_HEADS):
            sl = slice(h * RET_HEAD_DIM, (h + 1) * RET_HEAD_DIM)
            k_ref[bb, rows, sl] = (_rope_head(k[:, sl], cos2, sin2)
                                   * (RET_HEAD_DIM ** -0.5)).astype(_BF)
        g = _dot(hn, w_ref[:, _OFF_G:IN_PROJ_WIDTH])
        sg_ref[bb, rows, :] = (g * jax.nn.sigmoid(g)).astype(_BF)
        v_ref[bb, rows, :] = _dot(hn, w_ref[:, _OFF_V:_OFF_G]).astype(_BF)
        return nxt

    n_sub = BPS_PROJ * halves
    hn = normed(0)
    for s in range(n_sub):
        nxt = (lambda s=s: normed(s + 1)) if s + 1 < n_sub else None
        hn = project(s, hn, nxt)


def _s5_kernel(u_ref, init_ref, bblk_ref, cblk_ref, ar_ref, ai_ref, d_ref, wglu_ref, bglu_ref,
               y_ref, bu0_scr, bu1_scr, xb0_scr, xb1_scr, acc0_scr, acc1_scr, st_scr):
    @pl.when(pl.program_id(0) == 0)
    def _():
        st_scr[...] = jnp.broadcast_to(init_ref[...], (BATCH, 2 * S5_NSTATE))

    rows_sb = SB_S5 * BATCH
    n_sb = TT_S5 // SB_S5
    n_piece = 2 * S5_NSTATE // MXU_TILE
    steps_piece = SB_S5 // n_piece
    re = slice(0, S5_NSTATE)
    im = slice(S5_NSTATE, 2 * S5_NSTATE)
    ar = jnp.broadcast_to(ar_ref[...], (BATCH, S5_NSTATE))
    ai = jnp.broadcast_to(ai_ref[...], (BATCH, S5_NSTATE))

    def rows_of(j):
        if isinstance(j, int):
            return pl.ds(j * rows_sb, rows_sb)
        return pl.ds(pl.multiple_of(j * rows_sb, rows_sb), rows_sb)

    def load_u(j):
        return jnp.concatenate([u_ref[s, rows_of(j), :] for s in range(S5_SLABS)], axis=1)

    bu_scr = (bu0_scr, bu1_scr)
    xb_scr = (xb0_scr, xb1_scr)
    acc_scr = (acc0_scr, acc1_scr)

    def a_piece(ub, slot, c):
        cols = slice(c * MXU_TILE, (c + 1) * MXU_TILE)
        bu_scr[slot][:, cols] = _dot(ub, bblk_ref[:, cols])

    def b_piece(slot, c, xr, xi):
        for t in range(c * steps_piece, (c + 1) * steps_piece, 2):
            out_r, out_i = [], []
            for tt in (t, t + 1):
                rows = slice(tt * BATCH, (tt + 1) * BATCH)
                br = bu_scr[slot][rows, re]
                bi = bu_scr[slot][rows, im]
                xr, xi = ar * xr - ai * xi + br, ar * xi + ai * xr + bi
                out_r.append(xr)
                out_i.append(xi)
            rows2 = slice(t * BATCH, (t + 2) * BATCH)
            xb_scr[slot][rows2, re] = jnp.concatenate(out_r, axis=0).astype(_BF)
            xb_scr[slot][rows2, im] = jnp.concatenate(out_i, axis=0).astype(_BF)
        return xr, xi

    def c_piece(slot, c, acc):
        cols = slice(c * MXU_TILE, (c + 1) * MXU_TILE)
        part = _dot(xb_scr[slot][:, cols], cblk_ref[cols, :])
        return part if acc is None else acc + part

    def d_piece(j, slot, half):
        lo = half * (rows_sb // 2)
        if isinstance(j, int):
            rows = pl.ds(j * rows_sb + lo, rows_sb // 2)
        else:
            rows = pl.ds(pl.multiple_of(j * rows_sb + lo, rows_sb // 2), rows_sb // 2)
        u = jnp.concatenate([u_ref[s, rows, :] for s in range(S5_SLABS)], axis=1)
        y = jax.nn.gelu(acc_scr[slot][lo:lo + rows_sb // 2, :] + d_ref[...] * u)
        gate = jax.nn.sigmoid(_dot(y.astype(_BF), wglu_ref[...]) + bglu_ref[...])
        y = y * gate
        for s in range(S5_SLABS):
            y_ref[s, rows, :] = y[:, s * LANES:(s + 1) * LANES]

    def iteration(j, slot, xr, xi, do_a, do_b, do_c, do_d):
        other = 1 - slot
        ub = load_u(j + 1).astype(_BF) if do_a else None
        acc = None
        for c in range(n_piece):
            if do_a:
                a_piece(ub, other, c)
            if do_b:
                xr, xi = b_piece(slot, c, xr, xi)
            if do_c:
                acc = c_piece(other, c, acc)
            if do_d and c % (n_piece // 2) == 1:
                d_piece(j - 2, slot, c // (n_piece // 2))
        if do_c:
            acc_scr[other][...] = acc
        return xr, xi

    ub0 = load_u(0).astype(_BF)
    for c in range(n_piece):
        a_piece(ub0, 0, c)
    xr, xi = iteration(0, 0, st_scr[:, re], st_scr[:, im], True, True, False, False)
    xr, xi = iteration(1, 1, xr, xi, True, True, True, False)

    def pair(p, carry):
        j = 2 * p + 2
        xr, xi = iteration(j, 0, carry[0], carry[1], True, True, True, True)
        return iteration(j + 1, 1, xr, xi, True, True, True, True)

    xr, xi = lax.fori_loop(0, (n_sb - 4) // 2, pair, (xr, xi))
    xr, xi = iteration(n_sb - 2, 0, xr, xi, True, True, True, True)
    xr, xi = iteration(n_sb - 1, 1, xr, xi, False, True, True, True)
    st_scr[:, re] = xr
    st_scr[:, im] = xi
    iteration(n_sb, 0, xr, xi, False, False, True, True)
    iteration(n_sb + 1, 1, xr, xi, False, False, False, True)


def _ret_kernel(q_ref, k_ref, v_ref, sg_ref, s0_ref, dmat_ref, xi_ref, zeta_ref, gc_ref,
                gng_ref, gnb_ref, y_ref, s_scr):
    @pl.when(pl.program_id(1) == 0)
    def _():
        for bb in range(BPS_RET):
            s_scr[bb] = s0_ref[...]

    heads = [(h, slice(h * RET_HEAD_DIM, (h + 1) * RET_HEAD_DIM)) for h in range(RET_HEADS)]

    def wave_dots(bb, rows):
        scores, cross = [], []
        for h, cols in heads:
            qh = q_ref[bb, rows, cols]
            kh = k_ref[bb, rows, cols]
            state = s_scr[bb, h]
            scores.append(lax.dot_general(qh, kh, (((1,), (1,)), ((), ())),
                                          preferred_element_type=_F32))
            cross.append(_dot(qh, state.astype(_BF)))
            kz = (kh.astype(_F32) * zeta_ref[h]).astype(_BF)
            s_scr[bb, h] = gc_ref[h] * state + lax.dot_general(
                kz, v_ref[bb, rows, cols], (((0,), (0,)), ((), ())),
                preferred_element_type=_F32)
        return scores, cross

    def wave_mix(bb, rows, scores, cross):
        return [_dot((scores[h] * dmat_ref[h]).astype(_BF), v_ref[bb, rows, cols])
                + cross[h] * xi_ref[h] for h, cols in heads]

    def wave_norm(bb, rows, outs):
        for h, cols in heads:
            o = outs[h]
            mu = jnp.mean(o, axis=-1, keepdims=True)
            oc = o - mu
            var = jnp.mean(oc * oc, axis=-1, keepdims=True)
            on = oc * lax.rsqrt(var + GN_EPS) * gng_ref[:, cols] + gnb_ref[:, cols]
            y_ref[bb, rows, cols] = (sg_ref[bb, rows, cols].astype(_F32) * on).astype(_BF)

    assert BPS_RET == 2
    pending = None
    for c in range(TR_RET // CHUNK):
        rows = slice(c * CHUNK, (c + 1) * CHUNK)
        dots0 = wave_dots(0, rows)
        if pending is not None:
            wave_norm(1, *pending)
        outs0 = wave_mix(0, rows, *dots0)
        dots1 = wave_dots(1, rows)
        wave_norm(0, rows, outs0)
        pending = (rows, wave_mix(1, rows, *dots1))
    wave_norm(1, *pending)


def _ffn_kernel(x_ref, ys5_ref, yret_ref, lig_ref, lib_ref, wo_ref, l1g_ref, l1b_ref,
                wup_ref, wdn_ref, l2g_ref, l2b_ref, o_ref):
    pair = pl.program_id(1)
    n_ch = D_FF // FF_CHUNK
    piece = SUB_FFN // (2 * n_ch)
    halves = TM_FFN // SUB_FFN
    n_sub = BPS_FFN * halves

    def where(t):
        return t // halves, (t % halves) * SUB_FFN

    def mixed_of(t):
        bb, r0 = where(t)
        start = pair * BPS_FFN + bb + r0 * BATCH
        ys5 = jnp.concatenate(
            [ys5_ref[j, pl.ds(start, SUB_FFN, stride=BATCH), :] for j in range(S5_SLABS)],
            axis=1).astype(_BF)
        return (_dot(ys5, wo_ref[0:S5_WIDTH, :])
                + _dot(yret_ref[bb, r0:r0 + SUB_FFN, :], wo_ref[S5_WIDTH:D_MODEL, :]))

    def prep_rows(t, mixed, lo, hi):
        bb, r0 = where(t)
        h = _layer_norm(x_ref[bb, r0 + lo:r0 + hi, :], lig_ref[...], lib_ref[...])
        return _layer_norm(DEEPNORM_ALPHA * h + mixed[lo:hi], l1g_ref[...], l1b_ref[...])

    def ffn_up(h1b, c, floor):
        up = jnp.maximum(_dot(h1b, wup_ref[:, c * FF_CHUNK:(c + 1) * FF_CHUNK]), floor)
        return (up * up).astype(_BF)

    def ffn_down(act, c):
        return _dot(act, wdn_ref[c * FF_CHUNK:(c + 1) * FF_CHUNK, :])

    def finish_rows(t, pre, lo, hi):
        bb, r0 = where(t)
        out = _layer_norm(pre[lo:hi], l2g_ref[...], l2b_ref[...])
        o_ref[bb, r0 + lo:r0 + hi, :] = out
        return out

    h1 = prep_rows(0, mixed_of(0), 0, SUB_FFN)
    pre_prev = None
    floor = 0.0
    for t in range(n_sub):
        h1b = h1.astype(_BF)
        pre = DEEPNORM_ALPHA * h1
        mixed_next = mixed_of(t + 1) if t + 1 < n_sub else None
        next_parts = []

        def side_work(k):
            lo, hi = k * piece, (k + 1) * piece
            done = []
            if mixed_next is not None:
                next_parts.append(prep_rows(t + 1, mixed_next, lo, hi))
                done.append(next_parts[-1])
            if pre_prev is not None:
                done.append(finish_rows(t - 1, pre_prev, lo, hi))
            return done

        for c in range(n_ch):
            act = ffn_up(h1b, c, floor)
            done = side_work(2 * c)
            pre = pre + ffn_down(act, c)
            done += side_work(2 * c + 1)
            floor = _zero_after(done)
        if next_parts:
            h1 = jnp.concatenate(next_parts, axis=0)
        pre_prev = pre
    finish_rows(n_sub - 1, pre_prev, 0, SUB_FFN)


def _row(v):
    return v.reshape(1, -1).astype(_F32)


@functools.lru_cache(maxsize=None)
def _position_tables():
    f32 = np.float32

    def fn(f, a):
        return f(a.astype(np.float64)).astype(f32)

    pos = np.arange(N_META + SEQ, dtype=f32)
    expo = np.arange(0, RET_HEAD_DIM, 2, dtype=f32) / f32(RET_HEAD_DIM)
    inv_freq = (f32(1.0) / fn(lambda e: np.power(ROPE_BASE, e), expo)).astype(f32)
    ang = pos[:, None] * inv_freq[None, :]
    cos, sin = fn(np.cos, ang), fn(np.sin, ang)
    cos2 = np.concatenate([cos, cos], axis=1)
    sin2 = np.concatenate([-sin, sin], axis=1)

    heads = np.arange(RET_HEADS, dtype=f32)
    log_gamma = fn(np.log1p, -fn(np.exp2, f32(-5.0) - heads))
    idx = np.arange(CHUNK, dtype=f32)
    diff = idx[:, None] - idx[None, :]
    dmat = np.where(diff[None] >= 0,
                    fn(np.exp, np.maximum(diff, f32(0.0))[None] * log_gamma[:, None, None]),
                    f32(0.0)).astype(f32)
    zeta = fn(np.exp, (f32(CHUNK - 1.0) - idx)[None] * log_gamma[:, None])
    xi = fn(np.exp, (idx + f32(1.0))[None] * log_gamma[:, None])
    gamma_chunk = fn(np.exp, f32(CHUNK) * log_gamma)
    hd = RET_HEAD_DIM
    zeta_b = np.ascontiguousarray(np.broadcast_to(zeta[:, :, None], (RET_HEADS, CHUNK, hd)))
    xi_b = np.ascontiguousarray(np.broadcast_to(xi[:, :, None], (RET_HEADS, CHUNK, hd)))
    gc_b = np.ascontiguousarray(np.broadcast_to(gamma_chunk[:, None, None], (RET_HEADS, 1, hd)))
    return cos2, sin2, dmat, zeta_b, xi_b, gc_b


def kernel(x, meta_tokens, ln_in_g, ln_in_b, w_in, s5_lambda_re, s5_lambda_im, s5_log_dt, s5_b_re, s5_b_im, s5_c_re, s5_c_im, s5_d, s5_w_glu, s5_b_glu, ret_gn_g, ret_gn_b, w_out, ln1_g, ln1_b, w_up, w_down, ln2_g, ln2_b):
    assert x.shape == (BATCH, SEQ, D_MODEL) and w_in.shape[0] == 1
    G, P, H = S5_GROUPS, S5_STATE, S5_GROUP_CH
    arb2 = pltpu.CompilerParams(dimension_semantics=("arbitrary", "arbitrary"),
                                vmem_limit_bytes=VMEM_LIMIT)
    arb1 = pltpu.CompilerParams(dimension_semantics=("arbitrary",), vmem_limit_bytes=VMEM_LIMIT)

    lam_re, lam_im = s5_lambda_re[0], s5_lambda_im[0]
    dt = jnp.exp(s5_log_dt[0])[:, None]
    mag = jnp.exp(lam_re * dt)
    lbr = mag * jnp.cos(lam_im * dt)
    lbi = mag * jnp.sin(lam_im * dt)
    den = lam_re * lam_re + lam_im * lam_im
    nr = lbr - 1.0
    qr = (nr * lam_re + lbi * lam_im) / den
    qi = (lbi * lam_re - nr * lam_im) / den
    bbr = qr[..., None] * s5_b_re[0] - qi[..., None] * s5_b_im[0]
    bbi = qr[..., None] * s5_b_im[0] + qi[..., None] * s5_b_re[0]
    eye = jnp.eye(G, dtype=_F32)

    def blk_in(m):
        return (eye[:, None, :, None] * m.transpose(0, 2, 1)[:, :, None, :]).reshape(G * H, G * P)

    def blk_out(m):
        return (eye[:, None, :, None] * m.transpose(0, 2, 1)[:, :, None, :]).reshape(G * P, G * H)

    bblk = jnp.concatenate([blk_in(bbr), blk_in(bbi)], axis=1).astype(_BF)
    cblk = jnp.concatenate([blk_out(s5_c_re[0]), -blk_out(s5_c_im[0])], axis=0).astype(_BF)
    ar = lbr.reshape(1, S5_NSTATE)
    ai = lbi.reshape(1, S5_NSTATE)

    hd = RET_HEAD_DIM
    cos2, sin2, dmat, zeta_b, xi_b, gc_b = _position_tables()
    zmeta_b = zeta_b[:, CHUNK - N_META:, :]

    lig, lib = _row(ln_in_g), _row(ln_in_b)

    s5_init, s0, w_in_b = pl.pallas_call(
        _meta_kernel,
        out_shape=(jax.ShapeDtypeStruct((1, 2 * S5_NSTATE), _F32),
                   jax.ShapeDtypeStruct((RET_HEADS, hd, hd), _F32),
                   jax.ShapeDtypeStruct((D_MODEL, IN_PROJ_WIDTH), _BF)),
        compiler_params=pltpu.CompilerParams(vmem_limit_bytes=VMEM_LIMIT),
        name="meta_prologue",
    )(meta_tokens.astype(_F32), lig, lib, w_in[0], bblk, ar, ai,
      cos2[:N_META], sin2[:N_META], zmeta_b)

    n_t = SEQ // TM_PROJ
    rows_spec = lambda w: pl.BlockSpec((BPS_PROJ, TM_PROJ, w), lambda i, p: (p, i, 0))
    ret_shape = jax.ShapeDtypeStruct((BATCH, SEQ, RET_WIDTH), _BF)
    n_pairs = BATCH // BPS_PROJ
    n_steps = n_t * n_pairs

    def slab_spec(shape):
        return pl.BlockSpec((shape[0] // n_steps, shape[1]), lambda i, p: (i * n_pairs + p, 0))

    w_shapes = [(D_MODEL, D_MODEL), (D_MODEL, D_FF), (D_FF, D_MODEL)]
    u_tm, q, k, v, sg, w_out_b, w_up_b, w_dn_b = pl.pallas_call(
        _in_proj_kernel,
        grid=(n_t, n_pairs),
        in_specs=[rows_spec(D_MODEL), _const_spec((1, D_MODEL)), _const_spec((1, D_MODEL)),
                  _const_spec((D_MODEL, IN_PROJ_WIDTH)),
                  pl.BlockSpec((TM_PROJ, hd), lambda i, p: (i, 0)),
                  pl.BlockSpec((TM_PROJ, hd), lambda i, p: (i, 0))]
                 + [slab_spec(s) for s in w_shapes],
        out_specs=[pl.BlockSpec((S5_SLABS, TM_PROJ * BATCH, LANES), lambda i, p: (0, i, 0)),
                   rows_spec(RET_WIDTH), rows_spec(RET_WIDTH), rows_spec(RET_WIDTH),
                   rows_spec(RET_WIDTH)] + [slab_spec(s) for s in w_shapes],
        out_shape=(jax.ShapeDtypeStruct((S5_SLABS, SEQ * BATCH, LANES), _F32),
                   ret_shape, ret_shape, ret_shape, ret_shape)
                  + tuple(jax.ShapeDtypeStruct(s, _BF) for s in w_shapes),
        compiler_params=arb2,
        name="in_proj",
    )(x, lig, lib, w_in_b, cos2[N_META:], sin2[N_META:], w_out[0], w_up[0], w_down[0])

    rows_s5 = TT_S5 * BATCH
    ys5_tm = pl.pallas_call(
        _s5_kernel,
        grid=(SEQ // TT_S5,),
        in_specs=[pl.BlockSpec((S5_SLABS, rows_s5, LANES), lambda i: (0, i, 0)),
                  _const_spec((1, 2 * S5_NSTATE)),
                  _const_spec((S5_WIDTH, 2 * S5_NSTATE)), _const_spec((2 * S5_NSTATE, S5_WIDTH)),
                  _const_spec((1, S5_NSTATE)), _const_spec((1, S5_NSTATE)),
                  _const_spec((1, S5_WIDTH)), _const_spec((S5_WIDTH, S5_WIDTH)),
                  _const_spec((1, S5_WIDTH))],
        out_specs=pl.BlockSpec((S5_SLABS, rows_s5, LANES), lambda i: (0, i, 0)),
        out_shape=jax.ShapeDtypeStruct((S5_SLABS, SEQ * BATCH, LANES), _F32),
        scratch_shapes=[pltpu.VMEM((SB_S5 * BATCH, 2 * S5_NSTATE), _F32),
                        pltpu.VMEM((SB_S5 * BATCH, 2 * S5_NSTATE), _F32),
                        pltpu.VMEM((SB_S5 * BATCH, 2 * S5_NSTATE), _BF),
                        pltpu.VMEM((SB_S5 * BATCH, 2 * S5_NSTATE), _BF),
                        pltpu.VMEM((SB_S5 * BATCH, S5_WIDTH), _F32),
                        pltpu.VMEM((SB_S5 * BATCH, S5_WIDTH), _F32),
                        pltpu.VMEM((BATCH, 2 * S5_NSTATE), _F32)],
        compiler_params=arb1,
        name="s5_scan",
    )(u_tm, s5_init, bblk, cblk, ar, ai, _row(s5_d[0]), s5_w_glu[0].astype(_BF), _row(s5_b_glu[0]))

    n_r = SEQ // TR_RET
    ret_spec = pl.BlockSpec((BPS_RET, TR_RET, RET_WIDTH), lambda b, c: (b, c, 0))
    tab = lambda n: _const_spec((RET_HEADS, n, hd))
    y_ret = pl.pallas_call(
        _ret_kernel,
        grid=(BATCH // BPS_RET, n_r),
        in_specs=[ret_spec, ret_spec, ret_spec, ret_spec, tab(hd), tab(CHUNK), tab(CHUNK),
                  tab(CHUNK), tab(1), _const_spec((1, RET_WIDTH)), _const_spec((1, RET_WIDTH))],
        out_specs=ret_spec,
        out_shape=jax.ShapeDtypeStruct((BATCH, SEQ, RET_WIDTH), _BF),
        scratch_shapes=[pltpu.VMEM((BPS_RET, RET_HEADS, hd, hd), _F32)],
        compiler_params=arb2,
        name="retention",
    )(q, k, v, sg, s0, dmat, xi_b, zeta_b, gc_b, _row(ret_gn_g[0]), _row(ret_gn_b[0]))

    n_f = SEQ // TM_FFN
    frow = lambda w: pl.BlockSpec((BPS_FFN, TM_FFN, w), lambda i, p: (p, i, 0))
    out = pl.pallas_call(
        _ffn_kernel,
        grid=(n_f, BATCH // BPS_FFN),
        in_specs=[frow(D_MODEL),
                  pl.BlockSpec((S5_SLABS, TM_FFN * BATCH, LANES), lambda i, p: (0, i, 0)),
                  frow(RET_WIDTH),
                  _const_spec((1, D_MODEL)), _const_spec((1, D_MODEL)),
                  _const_spec((D_MODEL, D_MODEL)),
                  _const_spec((1, D_MODEL)), _const_spec((1, D_MODEL)),
                  _const_spec((D_MODEL, D_FF)), _const_spec((D_FF, D_MODEL)),
                  _const_spec((1, D_MODEL)), _const_spec((1, D_MODEL))],
        out_specs=frow(D_MODEL),
        out_shape=jax.ShapeDtypeStruct((BATCH, SEQ, D_MODEL), _F32),
        compiler_params=arb2,
        name="out_ffn",
    )(x, ys5_tm, y_ret, lig, lib, w_out_b,
      _row(ln1_g[0]), _row(ln1_b[0]), w_up_b, w_dn_b, _row(ln2_g[0]), _row(ln2_b[0]))

    return out
```

```python
import functools

import jax
import jax.numpy as jnp
import numpy as np
from jax import lax
from jax.experimental import pallas as pl
from jax.experimental.pallas import tpu as pltpu

D_MODEL = 1024
BATCH = 8
SEQ = 4096
N_META = 16
S5_GROUP_CH = 16
S5_STATE = 64
S5_WIDTH = 256
S5_GROUPS = S5_WIDTH // S5_GROUP_CH
S5_NSTATE = S5_GROUPS * S5_STATE
RET_HEAD_DIM = 128
RET_WIDTH = 768
RET_HEADS = RET_WIDTH // RET_HEAD_DIM
CHUNK = 128
ROPE_BASE = 10000.0
D_FF = 4 * D_MODEL
LANES = 128
S5_SLABS = S5_WIDTH // LANES
LN_EPS = 1e-5
GN_EPS = 1e-5
IN_PROJ_WIDTH = S5_WIDTH + 4 * RET_WIDTH
DEEPNORM_ALPHA = 2.0 ** 0.25

_OFF_Q = S5_WIDTH
_OFF_K = _OFF_Q + RET_WIDTH
_OFF_V = _OFF_K + RET_WIDTH
_OFF_G = _OFF_V + RET_WIDTH

TM_PROJ = 512
BPS_PROJ = 2
SUB_PROJ = 256
MXU_TILE = 256
TT_S5 = 1024
SB_S5 = 32
BPS_RET = 2
TR_RET = 512
TM_FFN = 512
BPS_FFN = 2
SUB_FFN = 256
FF_CHUNK = 1024
VMEM_LIMIT = 56 * 1024 * 1024

_BF = jnp.bfloat16
_F32 = jnp.float32


def _const_spec(shape):
    nd = len(shape)
    return pl.BlockSpec(shape, lambda *_: (0,) * nd, pipeline_mode=pl.Buffered(1))


def _layer_norm(x, g, b):
    mu = jnp.mean(x, axis=-1, keepdims=True)
    xc = x - mu
    var = jnp.mean(xc * xc, axis=-1, keepdims=True)
    return xc * lax.rsqrt(var + LN_EPS) * g + b


def _dot(a, b):
    return jnp.dot(a, b, preferred_element_type=_F32)


def _zero_after(values):
    if not values:
        return 0.0
    tok = values[0][0:1, 0:1]
    for v in values[1:]:
        tok = tok + v[0:1, 0:1]
    bits = lax.shift_right_logical(lax.shift_right_logical(tok.astype(jnp.int32), 16), 16)
    return bits.astype(_F32)


def _rope_head(t, cos2, sin2):
    return t * cos2 + pltpu.roll(t, RET_HEAD_DIM // 2, 1) * sin2


def _meta_kernel(meta_ref, g_ref, b_ref, w32_ref, bblk_ref, ar_ref, ai_ref, cos_ref, sin_ref,
                 zmeta_ref, s5_ref, s0_ref, w_ref):
    slab = D_MODEL // 8
    for r in range(0, D_MODEL, slab):
        w_ref[r:r + slab, :] = w32_ref[r:r + slab, :].astype(_BF)
    hm = _layer_norm(meta_ref[...], g_ref[...], b_ref[...]).astype(_BF)
    u = _dot(hm, w_ref[:, 0:S5_WIDTH])
    bu = _dot(u.astype(_BF), bblk_ref[...])
    ar = ar_ref[...]
    ai = ai_ref[...]
    xr = jnp.zeros((1, S5_NSTATE), _F32)
    xi = jnp.zeros((1, S5_NSTATE), _F32)
    for t in range(N_META):
        br = bu[t:t + 1, 0:S5_NSTATE]
        bi = bu[t:t + 1, S5_NSTATE:2 * S5_NSTATE]
        xr, xi = ar * xr - ai * xi + br, ar * xi + ai * xr + bi
    s5_ref[:, 0:S5_NSTATE] = xr
    s5_ref[:, S5_NSTATE:2 * S5_NSTATE] = xi

    k = _dot(hm, w_ref[:, _OFF_K:_OFF_V])
    v = _dot(hm, w_ref[:, _OFF_V:_OFF_G]).astype(_BF)
    cos2 = cos_ref[...]
    sin2 = sin_ref[...]
    for h in range(RET_HEADS):
        sl = slice(h * RET_HEAD_DIM, (h + 1) * RET_HEAD_DIM)
        kh = _rope_head(k[:, sl], cos2, sin2) * (RET_HEAD_DIM ** -0.5)
        kz = (kh * zmeta_ref[h]).astype(_BF)
        s0_ref[h] = lax.dot_general(kz, v[:, sl], (((0,), (0,)), ((), ())),
                                    preferred_element_type=_F32)


def _in_proj_kernel(x_ref, g_ref, b_ref, w_ref, cos_ref, sin_ref, wo32_ref, wup32_ref, wdn32_ref,
                    u_ref, q_ref, k_ref, v_ref, sg_ref, wo16_ref, wup16_ref, wdn16_ref):
    wo16_ref[...] = wo32_ref[...].astype(_BF)
    wup16_ref[...] = wup32_ref[...].astype(_BF)
    wdn16_ref[...] = wdn32_ref[...].astype(_BF)

    pair = pl.program_id(1)
    halves = TM_PROJ // SUB_PROJ

    def where(s):
        return s // halves, (s % halves) * SUB_PROJ

    def normed(s):
        bb, r0 = where(s)
        return _layer_norm(x_ref[bb, r0:r0 + SUB_PROJ, :], g_ref[...], b_ref[...]).astype(_BF)

    def project(s, hn, hn_next):
        bb, r0 = where(s)
        rows = slice(r0, r0 + SUB_PROJ)
        cos2 = cos_ref[rows, :]
        sin2 = sin_ref[rows, :]
        u = _dot(hn, w_ref[:, 0:S5_WIDTH])
        start = pair * BPS_PROJ + bb + r0 * BATCH
        for j in range(S5_SLABS):
            u_ref[j, pl.ds(start, SUB_PROJ, stride=BATCH), :] = u[:, j * LANES:(j + 1) * LANES]
        q = _dot(hn, w_ref[:, _OFF_Q:_OFF_K])
        nxt = None if hn_next is None else hn_next()
        for h in range(RET_HEADS):
            sl = slice(h * RET_HEAD_DIM, (h + 1) * RET_HEAD_DIM)
            q_ref[bb, rows, sl] = _rope_head(q[:, sl], cos2, sin2).astype(_BF)
        k = _dot(hn, w_ref[:, _OFF_K:_OFF_V])
        for h in range(RET_HEADS):
            sl = slice(h * RET_HEAD_DIM, (h + 1) * RET_HEAD_DIM)
            k_ref[bb, rows, sl] = (_rope_head(k[:, sl], cos2, sin2)
                                   * (RET_HEAD_DIM ** -0.5)).astype(_BF)
        g = _dot(hn, w_ref[:, _OFF_G:IN_PROJ_WIDTH])
        sg_ref[bb, rows, :] = (g * jax.nn.sigmoid(g)).astype(_BF)
        v_ref[bb, rows, :] = _dot(hn, w_ref[:, _OFF_V:_OFF_G]).astype(_BF)
        return nxt

    n_sub = BPS_PROJ * halves
    hn = normed(0)
    for s in range(n_sub):
        nxt = (lambda s=s: normed(s + 1)) if s + 1 < n_sub else None
        hn = project(s, hn, nxt)


def _s5_kernel(u_ref, init_ref, bblk_ref, cblk_ref, ar_ref, ai_ref, d_ref, wglu_ref, bglu_ref,
               y_ref, bu0_scr, bu1_scr, xb0_scr, xb1_scr, acc0_scr, acc1_scr, st_scr):
    @pl.when(pl.program_id(0) == 0)
    def _():
        st_scr[...] = jnp.broadcast_to(init_ref[...], (BATCH, 2 * S5_NSTATE))

    rows_sb = SB_S5 * BATCH
    n_sb = TT_S5 // SB_S5
    n_piece = 2 * S5_NSTATE // MXU_TILE
    steps_piece = SB_S5 // n_piece
    re = slice(0, S5_NSTATE)
    im = slice(S5_NSTATE, 2 * S5_NSTATE)
    ar = jnp.broadcast_to(ar_ref[...], (BATCH, S5_NSTATE))
    ai = jnp.broadcast_to(ai_ref[...], (BATCH, S5_NSTATE))

    def rows_of(j):
        if isinstance(j, int):
            return pl.ds(j * rows_sb, rows_sb)
        return pl.ds(pl.multiple_of(j * rows_sb, rows_sb), rows_sb)

    def load_u(j):
        return jnp.concatenate([u_ref[s, rows_of(j), :] for s in range(S5_SLABS)], axis=1)

    bu_scr = (bu0_scr, bu1_scr)
    xb_scr = (xb0_scr, xb1_scr)
    acc_scr = (acc0_scr, acc1_scr)

    def a_piece(ub, slot, c):
        cols = slice(c * MXU_TILE, (c + 1) * MXU_TILE)
        bu_scr[slot][:, cols] = _dot(ub, bblk_ref[:, cols])

    def b_piece(slot, c, xr, xi):
        for t in range(c * steps_piece, (c + 1) * steps_piece, 2):
            out_r, out_i = [], []
            for tt in (t, t + 1):
                rows = slice(tt * BATCH, (tt + 1) * BATCH)
                br = bu_scr[slot][rows, re]
                bi = bu_scr[slot][rows, im]
                xr, xi = ar * xr - ai * xi + br, ar * xi + ai * xr + bi
                out_r.append(xr)
                out_i.append(xi)
            rows2 = slice(t * BATCH, (t + 2) * BATCH)
            xb_scr[slot][rows2, re] = jnp.concatenate(out_r, axis=0).astype(_BF)
            xb_scr[slot][rows2, im] = jnp.concatenate(out_i, axis=0).astype(_BF)
        return xr, xi

    def c_piece(slot, c, acc):
        cols = slice(c * MXU_TILE, (c + 1) * MXU_TILE)
        part = _dot(xb_scr[slot][:, cols], cblk_ref[cols, :])
        return part if acc is None else acc + part

    def d_piece(j, slot, half):
        lo = half * (rows_sb // 2)
        if isinstance(j, int):
            rows = pl.ds(j * rows_sb + lo, rows_sb // 2)
        else:
            rows = pl.ds(pl.multiple_of(j * rows_sb + lo, rows_sb // 2), rows_sb // 2)
        u = jnp.concatenate([u_ref[s, rows, :] for s in range(S5_SLABS)], axis=1)
        y = jax.nn.gelu(acc_scr[slot][lo:lo + rows_sb // 2, :] + d_ref[...] * u)
        gate = jax.nn.sigmoid(_dot(y.astype(_BF), wglu_ref[...]) + bglu_ref[...])
        y = y * gate
        for s in range(S5_SLABS):
            y_ref[s, rows, :] = y[:, s * LANES:(s + 1) * LANES]

    def iteration(j, slot, xr, xi, do_a, do_b, do_c, do_d):
        other = 1 - slot
        ub = load_u(j + 1).astype(_BF) if do_a else None
        acc = None
        for c in range(n_piece):
            if do_a:
                a_piece(ub, other, c)
            if do_b:
                xr, xi = b_piece(slot, c, xr, xi)
            if do_c:
                acc = c_piece(other, c, acc)
            if do_d and c % (n_piece // 2) == 1:
                d_piece(j - 2, slot, c // (n_piece // 2))
        if do_c:
            acc_scr[other][...] = acc
        return xr, xi

    ub0 = load_u(0).astype(_BF)
    for c in range(n_piece):
        a_piece(ub0, 0, c)
    xr, xi = iteration(0, 0, st_scr[:, re], st_scr[:, im], True, True, False, False)
    xr, xi = iteration(1, 1, xr, xi, True, True, True, False)

    def pair(p, carry):
        j = 2 * p + 2
        xr, xi = iteration(j, 0, carry[0], carry[1], True, True, True, True)
        return iteration(j + 1, 1, xr, xi, True, True, True, True)

    xr, xi = lax.fori_loop(0, (n_sb - 4) // 2, pair, (xr, xi))
    xr, xi = iteration(n_sb - 2, 0, xr, xi, True, True, True, True)
    xr, xi = iteration(n_sb - 1, 1, xr, xi, False, True, True, True)
    st_scr[:, re] = xr
    st_scr[:, im] = xi
    iteration(n_sb, 0, xr, xi, False, False, True, True)
    iteration(n_sb + 1, 1, xr, xi, False, False, False, True)


def _ret_kernel(q_ref, k_ref, v_ref, sg_ref, s0_ref, dmat_ref, xi_ref, zeta_ref, gc_ref,
                gng_ref, gnb_ref, y_ref, s_scr):
    @pl.when(pl.program_id(1) == 0)
    def _():
        for bb in range(BPS_RET):
            s_scr[bb] = s0_ref[...]

    heads = [(h, slice(h * RET_HEAD_DIM, (h + 1) * RET_HEAD_DIM)) for h in range(RET_HEADS)]

    def wave_dots(bb, rows):
        scores, cross = [], []
        for h, cols in heads:
            qh = q_ref[bb, rows, cols]
            kh = k_ref[bb, rows, cols]
            state = s_scr[bb, h]
            scores.append(lax.dot_general(qh, kh, (((1,), (1,)), ((), ())),
                                          preferred_element_type=_F32))
            cross.append(_dot(qh, state.astype(_BF)))
            kz = (kh.astype(_F32) * zeta_ref[h]).astype(_BF)
            s_scr[bb, h] = gc_ref[h] * state + lax.dot_general(
                kz, v_ref[bb, rows, cols], (((0,), (0,)), ((), ())),
                preferred_element_type=_F32)
        return scores, cross

    def wave_mix(bb, rows, scores, cross):
        return [_dot((scores[h] * dmat_ref[h]).astype(_BF), v_ref[bb, rows, cols])
                + cross[h] * xi_ref[h] for h, cols in heads]

    def wave_norm(bb, rows, outs):
        for h, cols in heads:
            o = outs[h]
            mu = jnp.mean(o, axis=-1, keepdims=True)
            oc = o - mu
            var = jnp.mean(oc * oc, axis=-1, keepdims=True)
            on = oc * lax.rsqrt(var + GN_EPS) * gng_ref[:, cols] + gnb_ref[:, cols]
            y_ref[bb, rows, cols] = (sg_ref[bb, rows, cols].astype(_F32) * on).astype(_BF)

    for c in range(TR_RET // CHUNK):
        rows = slice(c * CHUNK, (c + 1) * CHUNK)
        dots = [wave_dots(bb, rows) for bb in range(BPS_RET)]
        outs = [wave_mix(bb, rows, *dots[bb]) for bb in range(BPS_RET)]
        for bb in range(BPS_RET):
            wave_norm(bb, rows, outs[bb])


def _ffn_kernel(x_ref, ys5_ref, yret_ref, lig_ref, lib_ref, wo_ref, l1g_ref, l1b_ref,
                wup_ref, wdn_ref, l2g_ref, l2b_ref, o_ref):
    pair = pl.program_id(1)
    n_ch = D_FF // FF_CHUNK
    piece = SUB_FFN // (2 * n_ch)
    halves = TM_FFN // SUB_FFN
    n_sub = BPS_FFN * halves

    def where(t):
        return t // halves, (t % halves) * SUB_FFN

    def mixed_of(t):
        bb, r0 = where(t)
        start = pair * BPS_FFN + bb + r0 * BATCH
        ys5 = jnp.concatenate(
            [ys5_ref[j, pl.ds(start, SUB_FFN, stride=BATCH), :] for j in range(S5_SLABS)],
            axis=1).astype(_BF)
        return (_dot(ys5, wo_ref[0:S5_WIDTH, :])
                + _dot(yret_ref[bb, r0:r0 + SUB_FFN, :], wo_ref[S5_WIDTH:D_MODEL, :]))

    def prep_rows(t, mixed, lo, hi):
        bb, r0 = where(t)
        h = _layer_norm(x_ref[bb, r0 + lo:r0 + hi, :], lig_ref[...], lib_ref[...])
        return _layer_norm(DEEPNORM_ALPHA * h + mixed[lo:hi], l1g_ref[...], l1b_ref[...])

    def ffn_up(h1b, c, floor):
        up = jnp.maximum(_dot(h1b, wup_ref[:, c * FF_CHUNK:(c + 1) * FF_CHUNK]), floor)
        return (up * up).astype(_BF)

    def ffn_down(act, c):
        return _dot(act, wdn_ref[c * FF_CHUNK:(c + 1) * FF_CHUNK, :])

    def finish_rows(t, pre, lo, hi):
        bb, r0 = where(t)
        out = _layer_norm(pre[lo:hi], l2g_ref[...], l2b_ref[...])
        o_ref[bb, r0 + lo:r0 + hi, :] = out
        return out

    h1 = prep_rows(0, mixed_of(0), 0, SUB_FFN)
    pre_prev = None
    floor = 0.0
    for t in range(n_sub):
        h1b = h1.astype(_BF)
        pre = DEEPNORM_ALPHA * h1
        mixed_next = mixed_of(t + 1) if t + 1 < n_sub else None
        next_parts = []

        def side_work(k):
            lo, hi = k * piece, (k + 1) * piece
            done = []
            if mixed_next is not None:
                next_parts.append(prep_rows(t + 1, mixed_next, lo, hi))
                done.append(next_parts[-1])
            if pre_prev is not None:
                done.append(finish_rows(t - 1, pre_prev, lo, hi))
            return done

        for c in range(n_ch):
            act = ffn_up(h1b, c, floor)
            done = side_work(2 * c)
            pre = pre + ffn_down(act, c)
            done += side_work(2 * c + 1)
            floor = _zero_after(done)
        if next_parts:
            h1 = jnp.concatenate(next_parts, axis=0)
        pre_prev = pre
    finish_rows(n_sub - 1, pre_prev, 0, SUB_FFN)


def _row(v):
    return v.reshape(1, -1).astype(_F32)


@functools.lru_cache(maxsize=None)
def _position_tables():
    f32 = np.float32

    def fn(f, a):
        return f(a.astype(np.float64)).astype(f32)

    pos = np.arange(N_META + SEQ, dtype=f32)
    expo = np.arange(0, RET_HEAD_DIM, 2, dtype=f32) / f32(RET_HEAD_DIM)
    inv_freq = (f32(1.0) / fn(lambda e: np.power(ROPE_BASE, e), expo)).astype(f32)
    ang = pos[:, None] * inv_freq[None, :]
    cos, sin = fn(np.cos, ang), fn(np.sin, ang)
    cos2 = np.concatenate([cos, cos], axis=1)
    sin2 = np.concatenate([-sin, sin], axis=1)

    heads = np.arange(RET_HEADS, dtype=f32)
    log_gamma = fn(np.log1p, -fn(np.exp2, f32(-5.0) - heads))
    idx = np.arange(CHUNK, dtype=f32)
    diff = idx[:, None] - idx[None, :]
    dmat = np.where(diff[None] >= 0,
                    fn(np.exp, np.maximum(diff, f32(0.0))[None] * log_gamma[:, None, None]),
                    f32(0.0)).astype(f32)
    zeta = fn(np.exp, (f32(CHUNK - 1.0) - idx)[None] * log_gamma[:, None])
    xi = fn(np.exp, (idx + f32(1.0))[None] * log_gamma[:, None])
    gamma_chunk = fn(np.exp, f32(CHUNK) * log_gamma)
    hd = RET_HEAD_DIM
    zeta_b = np.ascontiguousarray(np.broadcast_to(zeta[:, :, None], (RET_HEADS, CHUNK, hd)))
    xi_b = np.ascontiguousarray(np.broadcast_to(xi[:, :, None], (RET_HEADS, CHUNK, hd)))
    gc_b = np.ascontiguousarray(np.broadcast_to(gamma_chunk[:, None, None], (RET_HEADS, 1, hd)))
    return cos2, sin2, dmat, zeta_b, xi_b, gc_b


def kernel(x, meta_tokens, ln_in_g, ln_in_b, w_in, s5_lambda_re, s5_lambda_im, s5_log_dt, s5_b_re, s5_b_im, s5_c_re, s5_c_im, s5_d, s5_w_glu, s5_b_glu, ret_gn_g, ret_gn_b, w_out, ln1_g, ln1_b, w_up, w_down, ln2_g, ln2_b):
    assert x.shape == (BATCH, SEQ, D_MODEL) and w_in.shape[0] == 1
    G, P, H = S5_GROUPS, S5_STATE, S5_GROUP_CH
    arb2 = pltpu.CompilerParams(dimension_semantics=("arbitrary", "arbitrary"),
                                vmem_limit_bytes=VMEM_LIMIT)
    arb1 = pltpu.CompilerParams(dimension_semantics=("arbitrary",), vmem_limit_bytes=VMEM_LIMIT)

    lam_re, lam_im = s5_lambda_re[0], s5_lambda_im[0]
    dt = jnp.exp(s5_log_dt[0])[:, None]
    mag = jnp.exp(lam_re * dt)
    lbr = mag * jnp.cos(lam_im * dt)
    lbi = mag * jnp.sin(lam_im * dt)
    den = lam_re * lam_re + lam_im * lam_im
    nr = lbr - 1.0
    qr = (nr * lam_re + lbi * lam_im) / den
    qi = (lbi * lam_re - nr * lam_im) / den
    bbr = qr[..., None] * s5_b_re[0] - qi[..., None] * s5_b_im[0]
    bbi = qr[..., None] * s5_b_im[0] + qi[..., None] * s5_b_re[0]
    eye = jnp.eye(G, dtype=_F32)

    def blk_in(m):
        return (eye[:, None, :, None] * m.transpose(0, 2, 1)[:, :, None, :]).reshape(G * H, G * P)

    def blk_out(m):
        return (eye[:, None, :, None] * m.transpose(0, 2, 1)[:, :, None, :]).reshape(G * P, G * H)

    bblk = jnp.concatenate([blk_in(bbr), blk_in(bbi)], axis=1).astype(_BF)
    cblk = jnp.concatenate([blk_out(s5_c_re[0]), -blk_out(s5_c_im[0])], axis=0).astype(_BF)
    ar = lbr.reshape(1, S5_NSTATE)
    ai = lbi.reshape(1, S5_NSTATE)

    hd = RET_HEAD_DIM
    cos2, sin2, dmat, zeta_b, xi_b, gc_b = _position_tables()
    zmeta_b = zeta_b[:, CHUNK - N_META:, :]

    lig, lib = _row(ln_in_g), _row(ln_in_b)

    s5_init, s0, w_in_b = pl.pallas_call(
        _meta_kernel,
        out_shape=(jax.ShapeDtypeStruct((1, 2 * S5_NSTATE), _F32),
                   jax.ShapeDtypeStruct((RET_HEADS, hd, hd), _F32),
                   jax.ShapeDtypeStruct((D_MODEL, IN_PROJ_WIDTH), _BF)),
        compiler_params=pltpu.CompilerParams(vmem_limit_bytes=VMEM_LIMIT),
        name="meta_prologue",
    )(meta_tokens.astype(_F32), lig, lib, w_in[0], bblk, ar, ai,
      cos2[:N_META], sin2[:N_META], zmeta_b)

    n_t = SEQ // TM_PROJ
    rows_spec = lambda w: pl.BlockSpec((BPS_PROJ, TM_PROJ, w), lambda i, p: (p, i, 0))
    ret_shape = jax.ShapeDtypeStruct((BATCH, SEQ, RET_WIDTH), _BF)
    n_pairs = BATCH // BPS_PROJ
    n_steps = n_t * n_pairs

    def slab_spec(shape):
        return pl.BlockSpec((shape[0] // n_steps, shape[1]), lambda i, p: (i * n_pairs + p, 0))

    w_shapes = [(D_MODEL, D_MODEL), (D_MODEL, D_FF), (D_FF, D_MODEL)]
    u_tm, q, k, v, sg, w_out_b, w_up_b, w_dn_b = pl.pallas_call(
        _in_proj_kernel,
        grid=(n_t, n_pairs),
        in_specs=[rows_spec(D_MODEL), _const_spec((1, D_MODEL)), _const_spec((1, D_MODEL)),
                  _const_spec((D_MODEL, IN_PROJ_WIDTH)),
                  pl.BlockSpec((TM_PROJ, hd), lambda i, p: (i, 0)),
                  pl.BlockSpec((TM_PROJ, hd), lambda i, p: (i, 0))]
                 + [slab_spec(s) for s in w_shapes],
        out_specs=[pl.BlockSpec((S5_SLABS, TM_PROJ * BATCH, LANES), lambda i, p: (0, i, 0)),
                   rows_spec(RET_WIDTH), rows_spec(RET_WIDTH), rows_spec(RET_WIDTH),
                   rows_spec(RET_WIDTH)] + [slab_spec(s) for s in w_shapes],
        out_shape=(jax.ShapeDtypeStruct((S5_SLABS, SEQ * BATCH, LANES), _F32),
                   ret_shape, ret_shape, ret_shape, ret_shape)
                  + tuple(jax.ShapeDtypeStruct(s, _BF) for s in w_shapes),
        compiler_params=arb2,
        name="in_proj",
    )(x, lig, lib, w_in_b, cos2[N_META:], sin2[N_META:], w_out[0], w_up[0], w_down[0])

    rows_s5 = TT_S5 * BATCH
    ys5_tm = pl.pallas_call(
        _s5_kernel,
        grid=(SEQ // TT_S5,),
        in_specs=[pl.BlockSpec((S5_SLABS, rows_s5, LANES), lambda i: (0, i, 0)),
                  _const_spec((1, 2 * S5_NSTATE)),
                  _const_spec((S5_WIDTH, 2 * S5_NSTATE)), _const_spec((2 * S5_NSTATE, S5_WIDTH)),
                  _const_spec((1, S5_NSTATE)), _const_spec((1, S5_NSTATE)),
                  _const_spec((1, S5_WIDTH)), _const_spec((S5_WIDTH, S5_WIDTH)),
                  _const_spec((1, S5_WIDTH))],
        out_specs=pl.BlockSpec((S5_SLABS, rows_s5, LANES), lambda i: (0, i, 0)),
        out_shape=jax.ShapeDtypeStruct((S5_SLABS, SEQ * BATCH, LANES), _F32),
        scratch_shapes=[pltpu.VMEM((SB_S5 * BATCH, 2 * S5_NSTATE), _F32),
                        pltpu.VMEM((SB_S5 * BATCH, 2 * S5_NSTATE), _F32),
                        pltpu.VMEM((SB_S5 * BATCH, 2 * S5_NSTATE), _BF),
                        pltpu.VMEM((SB_S5 * BATCH, 2 * S5_NSTATE), _BF),
                        pltpu.VMEM((SB_S5 * BATCH, S5_WIDTH), _F32),
                        pltpu.VMEM((SB_S5 * BATCH, S5_WIDTH), _F32),
                        pltpu.VMEM((BATCH, 2 * S5_NSTATE), _F32)],
        compiler_params=arb1,
        name="s5_scan",
    )(u_tm, s5_init, bblk, cblk, ar, ai, _row(s5_d[0]), s5_w_glu[0].astype(_BF), _row(s5_b_glu[0]))

    n_r = SEQ // TR_RET
    ret_spec = pl.BlockSpec((BPS_RET, TR_RET, RET_WIDTH), lambda b, c: (b, c, 0))
    tab = lambda n: _const_spec((RET_HEADS, n, hd))
    y_ret = pl.pallas_call(
        _ret_kernel,
        grid=(BATCH // BPS_RET, n_r),
        in_specs=[ret_spec, ret_spec, ret_spec, ret_spec, tab(hd), tab(CHUNK), tab(CHUNK),
                  tab(CHUNK), tab(1), _const_spec((1, RET_WIDTH)), _const_spec((1, RET_WIDTH))],
        out_specs=ret_spec,
        out_shape=jax.ShapeDtypeStruct((BATCH, SEQ, RET_WIDTH), _BF),
        scratch_shapes=[pltpu.VMEM((BPS_RET, RET_HEADS, hd, hd), _F32)],
        compiler_params=arb2,
        name="retention",
    )(q, k, v, sg, s0, dmat, xi_b, zeta_b, gc_b, _row(ret_gn_g[0]), _row(ret_gn_b[0]))

    n_f = SEQ // TM_FFN
    frow = lambda w: pl.BlockSpec((BPS_FFN, TM_FFN, w), lambda i, p: (p, i, 0))
    out = pl.pallas_call(
        _ffn_kernel,
        grid=(n_f, BATCH // BPS_FFN),
        in_specs=[frow(D_MODEL),
                  pl.BlockSpec((S5_SLABS, TM_FFN * BATCH, LANES), lambda i, p: (0, i, 0)),
                  frow(RET_WIDTH),
                  _const_spec((1, D_MODEL)), _const_spec((1, D_MODEL)),
                  _const_spec((D_MODEL, D_MODEL)),
                  _const_spec((1, D_MODEL)), _const_spec((1, D_MODEL)),
                  _const_spec((D_MODEL, D_FF)), _const_spec((D_FF, D_MODEL)),
                  _const_spec((1, D_MODEL)), _const_spec((1, D_MODEL))],
        out_specs=frow(D_MODEL),
        out_shape=jax.ShapeDtypeStruct((BATCH, SEQ, D_MODEL), _F32),
        compiler_params=arb2,
        name="out_ffn",
    )(x, ys5_tm, y_ret, lig, lib, w_out_b,
      _row(ln1_g[0]), _row(ln1_b[0]), w_up_b, w_dn_b, _row(ln2_g[0]), _row(ln2_b[0]))

    return out
```

```python
import functools

import jax
import jax.numpy as jnp
import numpy as np
from jax import lax
from jax.experimental import pallas as pl
from jax.experimental.pallas import tpu as pltpu

D_MODEL = 1024
BATCH = 8
SEQ = 4096
N_META = 16
S5_GROUP_CH = 16
S5_STATE = 64
S5_WIDTH = 256
S5_GROUPS = S5_WIDTH // S5_GROUP_CH
S5_NSTATE = S5_GROUPS * S5_STATE
RET_HEAD_DIM = 128
RET_WIDTH = 768
RET_HEADS = RET_WIDTH // RET_HEAD_DIM
CHUNK = 128
ROPE_BASE = 10000.0
D_FF = 4 * D_MODEL
LANES = 128
S5_SLABS = S5_WIDTH // LANES
LN_EPS = 1e-5
GN_EPS = 1e-5
IN_PROJ_WIDTH = S5_WIDTH + 4 * RET_WIDTH
DEEPNORM_ALPHA = 2.0 ** 0.25

_OFF_Q = S5_WIDTH
_OFF_K = _OFF_Q + RET_WIDTH
_OFF_V = _OFF_K + RET_WIDTH
_OFF_G = _OFF_V + RET_WIDTH

TM_PROJ = 512
BPS_PROJ = 2
SUB_PROJ = 256
MXU_TILE = 256
TT_S5 = 1024
SB_S5 = 32
TM_FFN = 512
BPS_FFN = 2
SUB_FFN = 256
FF_CHUNK = 1024
VMEM_LIMIT = 56 * 1024 * 1024

_BF = jnp.bfloat16
_F32 = jnp.float32


def _const_spec(shape):
    nd = len(shape)
    return pl.BlockSpec(shape, lambda *_: (0,) * nd, pipeline_mode=pl.Buffered(1))


def _layer_norm(x, g, b):
    mu = jnp.mean(x, axis=-1, keepdims=True)
    xc = x - mu
    var = jnp.mean(xc * xc, axis=-1, keepdims=True)
    return xc * lax.rsqrt(var + LN_EPS) * g + b


def _dot(a, b):
    return jnp.dot(a, b, preferred_element_type=_F32)


def _zero_after(values):
    if not values:
        return 0.0
    tok = values[0][0:1, 0:1]
    for v in values[1:]:
        tok = tok + v[0:1, 0:1]
    bits = lax.shift_right_logical(lax.shift_right_logical(tok.astype(jnp.int32), 16), 16)
    return bits.astype(_F32)


def _rope_head(t, cos2, sin2):
    return t * cos2 + pltpu.roll(t, RET_HEAD_DIM // 2, 1) * sin2


def _meta_kernel(meta_ref, g_ref, b_ref, w32_ref, bblk_ref, ar_ref, ai_ref, cos_ref, sin_ref,
                 zmeta_ref, s5_ref, s0_ref, w_ref):
    slab = D_MODEL // 8
    for r in range(0, D_MODEL, slab):
        w_ref[r:r + slab, :] = w32_ref[r:r + slab, :].astype(_BF)
    hm = _layer_norm(meta_ref[...], g_ref[...], b_ref[...]).astype(_BF)
    u = _dot(hm, w_ref[:, 0:S5_WIDTH])
    bu = _dot(u.astype(_BF), bblk_ref[...])
    ar = ar_ref[...]
    ai = ai_ref[...]
    xr = jnp.zeros((1, S5_NSTATE), _F32)
    xi = jnp.zeros((1, S5_NSTATE), _F32)
    for t in range(N_META):
        br = bu[t:t + 1, 0:S5_NSTATE]
        bi = bu[t:t + 1, S5_NSTATE:2 * S5_NSTATE]
        xr, xi = ar * xr - ai * xi + br, ar * xi + ai * xr + bi
    s5_ref[:, 0:S5_NSTATE] = xr
    s5_ref[:, S5_NSTATE:2 * S5_NSTATE] = xi

    k = _dot(hm, w_ref[:, _OFF_K:_OFF_V])
    v = _dot(hm, w_ref[:, _OFF_V:_OFF_G]).astype(_BF)
    cos2 = cos_ref[...]
    sin2 = sin_ref[...]
    for h in range(RET_HEADS):
        sl = slice(h * RET_HEAD_DIM, (h + 1) * RET_HEAD_DIM)
        kh = _rope_head(k[:, sl], cos2, sin2) * (RET_HEAD_DIM ** -0.5)
        kz = (kh * zmeta_ref[h]).astype(_BF)
        s0_ref[h] = lax.dot_general(kz, v[:, sl], (((0,), (0,)), ((), ())),
                                    preferred_element_type=_F32)


def _in_proj_ret_kernel(x_ref, g_ref, b_ref, w_ref, cos_ref, sin_ref, s0_ref, dmat_ref, xi_ref,
                        zeta_ref, gc_ref, gng_ref, gnb_ref, wo32_ref, wup32_ref, wdn32_ref,
                        u_ref, y_ref, wo16_ref, wup16_ref, wdn16_ref,
                        q_scr, k_scr, v_scr, sg_scr, sloc_scr, s_scr):
    pair = pl.program_id(1)
    halves = TM_PROJ // SUB_PROJ
    heads = [(h, slice(h * RET_HEAD_DIM, (h + 1) * RET_HEAD_DIM)) for h in range(RET_HEADS)]

    wo16_ref[...] = wo32_ref[...].astype(_BF)
    wup16_ref[...] = wup32_ref[...].astype(_BF)
    wdn16_ref[...] = wdn32_ref[...].astype(_BF)

    @pl.when(pl.program_id(0) == 0)
    def _():
        for bb in range(BPS_PROJ):
            sloc_scr[bb] = s0_ref[...]

    @pl.when(pl.program_id(0) != 0)
    def _():
        for bb in range(BPS_PROJ):
            sloc_scr[bb] = s_scr[pair * BPS_PROJ + bb]

    def where(s):
        return s // halves, (s % halves) * SUB_PROJ

    def normed(s):
        bb, r0 = where(s)
        return _layer_norm(x_ref[bb, r0:r0 + SUB_PROJ, :], g_ref[...], b_ref[...]).astype(_BF)

    def project_stages(s, hn, box):
        bb, r0 = where(s)
        rows = slice(r0, r0 + SUB_PROJ)

        def st_u():
            u = _dot(hn, w_ref[:, 0:S5_WIDTH])
            start = pair * BPS_PROJ + bb + r0 * BATCH
            for j in range(S5_SLABS):
                u_ref[j, pl.ds(start, SUB_PROJ, stride=BATCH), :] = u[:, j * LANES:(j + 1) * LANES]

        def st_q():
            box['q'] = _dot(hn, w_ref[:, _OFF_Q:_OFF_K])

        def st_q_rope():
            if s + 1 < BPS_PROJ * halves:
                box['next'] = normed(s + 1)
            for h, sl in heads:
                q_scr[bb, rows, sl] = _rope_head(box['q'][:, sl], cos_ref[rows, :],
                                                 sin_ref[rows, :]).astype(_BF)

        def st_k():
            k = _dot(hn, w_ref[:, _OFF_K:_OFF_V])
            for h, sl in heads:
                k_scr[bb, rows, sl] = (_rope_head(k[:, sl], cos_ref[rows, :], sin_ref[rows, :])
                                       * (RET_HEAD_DIM ** -0.5)).astype(_BF)

        def st_g():
            g = _dot(hn, w_ref[:, _OFF_G:IN_PROJ_WIDTH])
            sg_scr[bb, rows, :] = (g * jax.nn.sigmoid(g)).astype(_BF)

        def st_v():
            v_scr[bb, rows, :] = _dot(hn, w_ref[:, _OFF_V:_OFF_G]).astype(_BF)

        return [st_u, st_q, st_q_rope, st_k, st_g, st_v]

    def retention_stages(s):
        bb, r0 = where(s)
        stages = []
        for c in range(SUB_PROJ // CHUNK):
            rows = slice(r0 + c * CHUNK, r0 + (c + 1) * CHUNK)
            box = {}

            def wave_dots(rows=rows, box=box):
                scores, cross = [], []
                for h, cols in heads:
                    qh = q_scr[bb, rows, cols]
                    kh = k_scr[bb, rows, cols]
                    state = sloc_scr[bb, h]
                    scores.append(lax.dot_general(qh, kh, (((1,), (1,)), ((), ())),
                                                  preferred_element_type=_F32))
                    cross.append(_dot(qh, state.astype(_BF)))
                    kz = (kh.astype(_F32) * zeta_ref[h]).astype(_BF)
                    sloc_scr[bb, h] = gc_ref[h] * state + lax.dot_general(
                        kz, v_scr[bb, rows, cols], (((0,), (0,)), ((), ())),
                        preferred_element_type=_F32)
                box['scores'], box['cross'] = scores, cross

            def wave_mix(rows=rows, box=box):
                box['outs'] = [
                    _dot((box['scores'][h] * dmat_ref[h]).astype(_BF), v_scr[bb, rows, cols])
                    + box['cross'][h] * xi_ref[h] for h, cols in heads]

            def wave_norm(rows=rows, box=box):
                for h, cols in heads:
                    o = box['outs'][h]
                    mu = jnp.mean(o, axis=-1, keepdims=True)
                    oc = o - mu
                    var = jnp.mean(oc * oc, axis=-1, keepdims=True)
                    on = oc * lax.rsqrt(var + GN_EPS) * gng_ref[:, cols] + gnb_ref[:, cols]
                    y_ref[bb, rows, cols] = (sg_scr[bb, rows, cols].astype(_F32)
                                             * on).astype(_BF)

            stages.append([wave_dots, wave_mix, wave_norm])
        n_merged = len(stages) + 2
        merged = []
        for i in range(n_merged):
            waves = [stages[c][i - c] for c in range(len(stages)) if 0 <= i - c < 3]
            merged.append(lambda waves=waves: [w() for w in waves])
        return merged

    n_sub = BPS_PROJ * halves
    hn = normed(0)
    for s in range(n_sub):
        box = {}
        proj = project_stages(s, hn, box)
        ret = retention_stages(s - 1) if s >= 1 else []
        for i in range(max(len(proj), len(ret))):
            if i < len(proj):
                proj[i]()
            if i < len(ret):
                ret[i]()
        hn = box.get('next')
    for wave in retention_stages(n_sub - 1):
        wave()

    for bb in range(BPS_PROJ):
        s_scr[pair * BPS_PROJ + bb] = sloc_scr[bb]


def _s5_kernel(u_ref, init_ref, bblk_ref, cblk_ref, ar_ref, ai_ref, d_ref, wglu_ref, bglu_ref,
               y_ref, bu0_scr, bu1_scr, xb0_scr, xb1_scr, acc0_scr, acc1_scr, st_scr):
    @pl.when(pl.program_id(0) == 0)
    def _():
        st_scr[...] = jnp.broadcast_to(init_ref[...], (BATCH, 2 * S5_NSTATE))

    rows_sb = SB_S5 * BATCH
    n_sb = TT_S5 // SB_S5
    n_piece = 2 * S5_NSTATE // MXU_TILE
    steps_piece = SB_S5 // n_piece
    re = slice(0, S5_NSTATE)
    im = slice(S5_NSTATE, 2 * S5_NSTATE)
    ar = jnp.broadcast_to(ar_ref[...], (BATCH, S5_NSTATE))
    ai = jnp.broadcast_to(ai_ref[...], (BATCH, S5_NSTATE))

    def rows_of(j):
        if isinstance(j, int):
            return pl.ds(j * rows_sb, rows_sb)
        return pl.ds(pl.multiple_of(j * rows_sb, rows_sb), rows_sb)

    def load_u(j):
        return jnp.concatenate([u_ref[s, rows_of(j), :] for s in range(S5_SLABS)], axis=1)

    bu_scr = (bu0_scr, bu1_scr)
    xb_scr = (xb0_scr, xb1_scr)
    acc_scr = (acc0_scr, acc1_scr)

    def a_piece(ub, slot, c):
        cols = slice(c * MXU_TILE, (c + 1) * MXU_TILE)
        bu_scr[slot][:, cols] = _dot(ub, bblk_ref[:, cols])

    def b_piece(slot, c, xr, xi):
        for t in range(c * steps_piece, (c + 1) * steps_piece, 2):
            out_r, out_i = [], []
            for tt in (t, t + 1):
                rows = slice(tt * BATCH, (tt + 1) * BATCH)
                br = bu_scr[slot][rows, re]
                bi = bu_scr[slot][rows, im]
                xr, xi = ar * xr - ai * xi + br, ar * xi + ai * xr + bi
                out_r.append(xr)
                out_i.append(xi)
            rows2 = slice(t * BATCH, (t + 2) * BATCH)
            xb_scr[slot][rows2, re] = jnp.concatenate(out_r, axis=0).astype(_BF)
            xb_scr[slot][rows2, im] = jnp.concatenate(out_i, axis=0).astype(_BF)
        return xr, xi

    def c_piece(slot, c, acc):
        cols = slice(c * MXU_TILE, (c + 1) * MXU_TILE)
        part = _dot(xb_scr[slot][:, cols], cblk_ref[cols, :])
        return part if acc is None else acc + part

    def d_piece(j, slot, half):
        lo = half * (rows_sb // 2)
        if isinstance(j, int):
            rows = pl.ds(j * rows_sb + lo, rows_sb // 2)
        else:
            rows = pl.ds(pl.multiple_of(j * rows_sb + lo, rows_sb // 2), rows_sb // 2)
        u = jnp.concatenate([u_ref[s, rows, :] for s in range(S5_SLABS)], axis=1)
        y = jax.nn.gelu(acc_scr[slot][lo:lo + rows_sb // 2, :] + d_ref[...] * u)
        gate = jax.nn.sigmoid(_dot(y.astype(_BF), wglu_ref[...]) + bglu_ref[...])
        y = y * gate
        for s in range(S5_SLABS):
            y_ref[s, rows, :] = y[:, s * LANES:(s + 1) * LANES]

    def iteration(j, slot, xr, xi, do_a, do_b, do_c, do_d):
        other = 1 - slot
        ub = load_u(j + 1).astype(_BF) if do_a else None
        acc = None
        for c in range(n_piece):
            if do_a:
                a_piece(ub, other, c)
            if do_b:
                xr, xi = b_piece(slot, c, xr, xi)
            if do_c:
                acc = c_piece(other, c, acc)
            if do_d and c % (n_piece // 2) == 1:
                d_piece(j - 2, slot, c // (n_piece // 2))
        if do_c:
            acc_scr[other][...] = acc
        return xr, xi

    ub0 = load_u(0).astype(_BF)
    for c in range(n_piece):
        a_piece(ub0, 0, c)
    xr, xi = iteration(0, 0, st_scr[:, re], st_scr[:, im], True, True, False, False)
    xr, xi = iteration(1, 1, xr, xi, True, True, True, False)

    def pair(p, carry):
        j = 2 * p + 2
        xr, xi = iteration(j, 0, carry[0], carry[1], True, True, True, True)
        return iteration(j + 1, 1, xr, xi, True, True, True, True)

    xr, xi = lax.fori_loop(0, (n_sb - 4) // 2, pair, (xr, xi))
    xr, xi = iteration(n_sb - 2, 0, xr, xi, True, True, True, True)
    xr, xi = iteration(n_sb - 1, 1, xr, xi, False, True, True, True)
    st_scr[:, re] = xr
    st_scr[:, im] = xi
    iteration(n_sb, 0, xr, xi, False, False, True, True)
    iteration(n_sb + 1, 1, xr, xi, False, False, False, True)


def _ffn_kernel(x_ref, ys5_ref, yret_ref, lig_ref, lib_ref, wo_ref, l1g_ref, l1b_ref,
                wup_ref, wdn_ref, l2g_ref, l2b_ref, o_ref):
    pair = pl.program_id(1)
    n_ch = D_FF // FF_CHUNK
    piece = SUB_FFN // (2 * n_ch)
    halves = TM_FFN // SUB_FFN
    n_sub = BPS_FFN * halves

    def where(t):
        return t // halves, (t % halves) * SUB_FFN

    def mixed_of(t):
        bb, r0 = where(t)
        start = pair * BPS_FFN + bb + r0 * BATCH
        ys5 = jnp.concatenate(
            [ys5_ref[j, pl.ds(start, SUB_FFN, stride=BATCH), :] for j in range(S5_SLABS)],
            axis=1).astype(_BF)
        return (_dot(ys5, wo_ref[0:S5_WIDTH, :])
                + _dot(yret_ref[bb, r0:r0 + SUB_FFN, :], wo_ref[S5_WIDTH:D_MODEL, :]))

    def prep_rows(t, mixed, lo, hi):
        bb, r0 = where(t)
        h = _layer_norm(x_ref[bb, r0 + lo:r0 + hi, :], lig_ref[...], lib_ref[...])
        return _layer_norm(DEEPNORM_ALPHA * h + mixed[lo:hi], l1g_ref[...], l1b_ref[...])

    def ffn_up(h1b, c, floor):
        up = jnp.maximum(_dot(h1b, wup_ref[:, c * FF_CHUNK:(c + 1) * FF_CHUNK]), floor)
        return (up * up).astype(_BF)

    def ffn_down(act, c):
        return _dot(act, wdn_ref[c * FF_CHUNK:(c + 1) * FF_CHUNK, :])

    def finish_rows(t, pre, lo, hi):
        bb, r0 = where(t)
        out = _layer_norm(pre[lo:hi], l2g_ref[...], l2b_ref[...])
        o_ref[bb, r0 + lo:r0 + hi, :] = out
        return out

    h1 = prep_rows(0, mixed_of(0), 0, SUB_FFN)
    pre_prev = None
    floor = 0.0
    for t in range(n_sub):
        h1b = h1.astype(_BF)
        pre = DEEPNORM_ALPHA * h1
        mixed_next = mixed_of(t + 1) if t + 1 < n_sub else None
        next_parts = []

        def side_work(k):
            lo, hi = k * piece, (k + 1) * piece
            done = []
            if mixed_next is not None:
                next_parts.append(prep_rows(t + 1, mixed_next, lo, hi))
                done.append(next_parts[-1])
            if pre_prev is not None:
                done.append(finish_rows(t - 1, pre_prev, lo, hi))
            return done

        for c in range(n_ch):
            act = ffn_up(h1b, c, floor)
            done = side_work(2 * c)
            pre = pre + ffn_down(act, c)
            done += side_work(2 * c + 1)
            floor = _zero_after(done)
        if next_parts:
            h1 = jnp.concatenate(next_parts, axis=0)
        pre_prev = pre
    finish_rows(n_sub - 1, pre_prev, 0, SUB_FFN)


def _row(v):
    return v.reshape(1, -1).astype(_F32)


@functools.lru_cache(maxsize=None)
def _position_tables():
    f32 = np.float32

    def fn(f, a):
        return f(a.astype(np.float64)).astype(f32)

    pos = np.arange(N_META + SEQ, dtype=f32)
    expo = np.arange(0, RET_HEAD_DIM, 2, dtype=f32) / f32(RET_HEAD_DIM)
    inv_freq = (f32(1.0) / fn(lambda e: np.power(ROPE_BASE, e), expo)).astype(f32)
    ang = pos[:, None] * inv_freq[None, :]
    cos, sin = fn(np.cos, ang), fn(np.sin, ang)
    cos2 = np.concatenate([cos, cos], axis=1)
    sin2 = np.concatenate([-sin, sin], axis=1)

    heads = np.arange(RET_HEADS, dtype=f32)
    log_gamma = fn(np.log1p, -fn(np.exp2, f32(-5.0) - heads))
    idx = np.arange(CHUNK, dtype=f32)
    diff = idx[:, None] - idx[None, :]
    dmat = np.where(diff[None] >= 0,
                    fn(np.exp, np.maximum(diff, f32(0.0))[None] * log_gamma[:, None, None]),
                    f32(0.0)).astype(f32)
    zeta = fn(np.exp, (f32(CHUNK - 1.0) - idx)[None] * log_gamma[:, None])
    xi = fn(np.exp, (idx + f32(1.0))[None] * log_gamma[:, None])
    gamma_chunk = fn(np.exp, f32(CHUNK) * log_gamma)
    hd = RET_HEAD_DIM
    zeta_b = np.ascontiguousarray(np.broadcast_to(zeta[:, :, None], (RET_HEADS, CHUNK, hd)))
    xi_b = np.ascontiguousarray(np.broadcast_to(xi[:, :, None], (RET_HEADS, CHUNK, hd)))
    gc_b = np.ascontiguousarray(np.broadcast_to(gamma_chunk[:, None, None], (RET_HEADS, 1, hd)))
    return cos2, sin2, dmat, zeta_b, xi_b, gc_b


def kernel(x, meta_tokens, ln_in_g, ln_in_b, w_in, s5_lambda_re, s5_lambda_im, s5_log_dt, s5_b_re, s5_b_im, s5_c_re, s5_c_im, s5_d, s5_w_glu, s5_b_glu, ret_gn_g, ret_gn_b, w_out, ln1_g, ln1_b, w_up, w_down, ln2_g, ln2_b):
    assert x.shape == (BATCH, SEQ, D_MODEL) and w_in.shape[0] == 1
    G, P, H = S5_GROUPS, S5_STATE, S5_GROUP_CH
    arb2 = pltpu.CompilerParams(dimension_semantics=("arbitrary", "arbitrary"),
                                vmem_limit_bytes=VMEM_LIMIT)
    arb1 = pltpu.CompilerParams(dimension_semantics=("arbitrary",), vmem_limit_bytes=VMEM_LIMIT)

    lam_re, lam_im = s5_lambda_re[0], s5_lambda_im[0]
    dt = jnp.exp(s5_log_dt[0])[:, None]
    mag = jnp.exp(lam_re * dt)
    lbr = mag * jnp.cos(lam_im * dt)
    lbi = mag * jnp.sin(lam_im * dt)
    den = lam_re * lam_re + lam_im * lam_im
    nr = lbr - 1.0
    qr = (nr * lam_re + lbi * lam_im) / den
    qi = (lbi * lam_re - nr * lam_im) / den
    bbr = qr[..., None] * s5_b_re[0] - qi[..., None] * s5_b_im[0]
    bbi = qr[..., None] * s5_b_im[0] + qi[..., None] * s5_b_re[0]
    eye = jnp.eye(G, dtype=_F32)

    def blk_in(m):
        return (eye[:, None, :, None] * m.transpose(0, 2, 1)[:, :, None, :]).reshape(G * H, G * P)

    def blk_out(m):
        return (eye[:, None, :, None] * m.transpose(0, 2, 1)[:, :, None, :]).reshape(G * P, G * H)

    bblk = jnp.concatenate([blk_in(bbr), blk_in(bbi)], axis=1).astype(_BF)
    cblk = jnp.concatenate([blk_out(s5_c_re[0]), -blk_out(s5_c_im[0])], axis=0).astype(_BF)
    ar = lbr.reshape(1, S5_NSTATE)
    ai = lbi.reshape(1, S5_NSTATE)

    hd = RET_HEAD_DIM
    cos2, sin2, dmat, zeta_b, xi_b, gc_b = _position_tables()
    zmeta_b = zeta_b[:, CHUNK - N_META:, :]

    lig, lib = _row(ln_in_g), _row(ln_in_b)

    s5_init, s0, w_in_b = pl.pallas_call(
        _meta_kernel,
        out_shape=(jax.ShapeDtypeStruct((1, 2 * S5_NSTATE), _F32),
                   jax.ShapeDtypeStruct((RET_HEADS, hd, hd), _F32),
                   jax.ShapeDtypeStruct((D_MODEL, IN_PROJ_WIDTH), _BF)),
        compiler_params=pltpu.CompilerParams(vmem_limit_bytes=VMEM_LIMIT),
        name="meta_prologue",
    )(meta_tokens.astype(_F32), lig, lib, w_in[0], bblk, ar, ai,
      cos2[:N_META], sin2[:N_META], zmeta_b)

    n_t = SEQ // TM_PROJ
    rows_spec = lambda w: pl.BlockSpec((BPS_PROJ, TM_PROJ, w), lambda i, p: (p, i, 0))
    n_pairs = BATCH // BPS_PROJ
    n_steps = n_t * n_pairs

    def slab_spec(shape):
        return pl.BlockSpec((shape[0] // n_steps, shape[1]), lambda i, p: (i * n_pairs + p, 0))

    tab = lambda n: _const_spec((RET_HEADS, n, hd))
    w_shapes = [(D_MODEL, D_MODEL), (D_MODEL, D_FF), (D_FF, D_MODEL)]
    qkv_scratch = pltpu.VMEM((BPS_PROJ, TM_PROJ, RET_WIDTH), _BF)
    u_tm, y_ret, w_out_b, w_up_b, w_dn_b = pl.pallas_call(
        _in_proj_ret_kernel,
        grid=(n_t, n_pairs),
        in_specs=[rows_spec(D_MODEL), _const_spec((1, D_MODEL)), _const_spec((1, D_MODEL)),
                  _const_spec((D_MODEL, IN_PROJ_WIDTH)),
                  pl.BlockSpec((TM_PROJ, hd), lambda i, p: (i, 0)),
                  pl.BlockSpec((TM_PROJ, hd), lambda i, p: (i, 0)),
                  tab(hd), tab(CHUNK), tab(CHUNK), tab(CHUNK), tab(1),
                  _const_spec((1, RET_WIDTH)), _const_spec((1, RET_WIDTH))]
                 + [slab_spec(s) for s in w_shapes],
        out_specs=[pl.BlockSpec((S5_SLABS, TM_PROJ * BATCH, LANES), lambda i, p: (0, i, 0)),
                   rows_spec(RET_WIDTH)] + [slab_spec(s) for s in w_shapes],
        out_shape=(jax.ShapeDtypeStruct((S5_SLABS, SEQ * BATCH, LANES), _F32),
                   jax.ShapeDtypeStruct((BATCH, SEQ, RET_WIDTH), _BF))
                  + tuple(jax.ShapeDtypeStruct(s, _BF) for s in w_shapes),
        scratch_shapes=[qkv_scratch, qkv_scratch, qkv_scratch, qkv_scratch,
                        pltpu.VMEM((BPS_PROJ, RET_HEADS, hd, hd), _F32),
                        pltpu.VMEM((BATCH, RET_HEADS, hd, hd), _F32)],
        compiler_params=arb2,
        name="in_proj_ret",
    )(x, lig, lib, w_in_b, cos2[N_META:], sin2[N_META:], s0, dmat, xi_b, zeta_b, gc_b,
      _row(ret_gn_g[0]), _row(ret_gn_b[0]), w_out[0], w_up[0], w_down[0])

    rows_s5 = TT_S5 * BATCH
    ys5_tm = pl.pallas_call(
        _s5_kernel,
        grid=(SEQ // TT_S5,),
        in_specs=[pl.BlockSpec((S5_SLABS, rows_s5, LANES), lambda i: (0, i, 0)),
                  _const_spec((1, 2 * S5_NSTATE)),
                  _const_spec((S5_WIDTH, 2 * S5_NSTATE)), _const_spec((2 * S5_NSTATE, S5_WIDTH)),
                  _const_spec((1, S5_NSTATE)), _const_spec((1, S5_NSTATE)),
                  _const_spec((1, S5_WIDTH)), _const_spec((S5_WIDTH, S5_WIDTH)),
                  _const_spec((1, S5_WIDTH))],
        out_specs=pl.BlockSpec((S5_SLABS, rows_s5, LANES), lambda i: (0, i, 0)),
        out_shape=jax.ShapeDtypeStruct((S5_SLABS, SEQ * BATCH, LANES), _F32),
        scratch_shapes=[pltpu.VMEM((SB_S5 * BATCH, 2 * S5_NSTATE), _F32),
                        pltpu.VMEM((SB_S5 * BATCH, 2 * S5_NSTATE), _F32),
                        pltpu.VMEM((SB_S5 * BATCH, 2 * S5_NSTATE), _BF),
                        pltpu.VMEM((SB_S5 * BATCH, 2 * S5_NSTATE), _BF),
                        pltpu.VMEM((SB_S5 * BATCH, S5_WIDTH), _F32),
                        pltpu.VMEM((SB_S5 * BATCH, S5_WIDTH), _F32),
                        pltpu.VMEM((BATCH, 2 * S5_NSTATE), _F32)],
        compiler_params=arb1,
        name="s5_scan",
    )(u_tm, s5_init, bblk, cblk, ar, ai, _row(s5_d[0]), s5_w_glu[0].astype(_BF), _row(s5_b_glu[0]))

    n_f = SEQ // TM_FFN
    frow = lambda w: pl.BlockSpec((BPS_FFN, TM_FFN, w), lambda i, p: (p, i, 0))
    out = pl.pallas_call(
        _ffn_kernel,
        grid=(n_f, BATCH // BPS_FFN),
        in_specs=[frow(D_MODEL),
                  pl.BlockSpec((S5_SLABS, TM_FFN * BATCH, LANES), lambda i, p: (0, i, 0)),
                  frow(RET_WIDTH),
                  _const_spec((1, D_MODEL)), _const_spec((1, D_MODEL)),
                  _const_spec((D_MODEL, D_MODEL)),
                  _const_spec((1, D_MODEL)), _const_spec((1, D_MODEL)),
                  _const_spec((D_MODEL, D_FF)), _const_spec((D_FF, D_MODEL)),
                  _const_spec((1, D_MODEL)), _const_spec((1, D_MODEL))],
        out_specs=frow(D_MODEL),
        out_shape=jax.ShapeDtypeStruct((BATCH, SEQ, D_MODEL), _F32),
        compiler_params=arb2,
        name="out_ffn",
    )(x, ys5_tm, y_ret, lig, lib, w_out_b,
      _row(ln1_g[0]), _row(ln1_b[0]), w_up_b, w_dn_b, _row(ln2_g[0]), _row(ln2_b[0]))

    return out
```

```python
import functools

import jax
import jax.numpy as jnp
import numpy as np
from jax import lax
from jax.experimental import pallas as pl
from jax.experimental.pallas import tpu as pltpu

D_MODEL = 1024
BATCH = 8
SEQ = 4096
N_META = 16
S5_GROUP_CH = 16
S5_STATE = 64
S5_WIDTH = 256
S5_GROUPS = S5_WIDTH // S5_GROUP_CH
S5_NSTATE = S5_GROUPS * S5_STATE
RET_HEAD_DIM = 128
RET_WIDTH = 768
RET_HEADS = RET_WIDTH // RET_HEAD_DIM
CHUNK = 128
ROPE_BASE = 10000.0
D_FF = 4 * D_MODEL
LANES = 128
S5_SLABS = S5_WIDTH // LANES
LN_EPS = 1e-5
GN_EPS = 1e-5
IN_PROJ_WIDTH = S5_WIDTH + 4 * RET_WIDTH
DEEPNORM_ALPHA = 2.0 ** 0.25

_OFF_Q = S5_WIDTH
_OFF_K = _OFF_Q + RET_WIDTH
_OFF_V = _OFF_K + RET_WIDTH
_OFF_G = _OFF_V + RET_WIDTH

TM_PROJ = 512
BPS_PROJ = 2
SUB_PROJ = 256
MXU_TILE = 256
TT_S5 = 1024
SB_S5 = 32
TM_FFN = 512
BPS_FFN = 2
SUB_FFN = 256
FF_CHUNK = 1024
VMEM_LIMIT = 56 * 1024 * 1024

_BF = jnp.bfloat16
_F32 = jnp.float32


def _const_spec(shape):
    nd = len(shape)
    return pl.BlockSpec(shape, lambda *_: (0,) * nd, pipeline_mode=pl.Buffered(1))


def _layer_norm(x, g, b):
    mu = jnp.mean(x, axis=-1, keepdims=True)
    xc = x - mu
    var = jnp.mean(xc * xc, axis=-1, keepdims=True)
    return xc * lax.rsqrt(var + LN_EPS) * g + b


def _dot(a, b):
    return jnp.dot(a, b, preferred_element_type=_F32)


def _zero_after(values):
    if not values:
        return 0.0
    tok = values[0][0:1, 0:1]
    for v in values[1:]:
        tok = tok + v[0:1, 0:1]
    bits = lax.shift_right_logical(lax.shift_right_logical(tok.astype(jnp.int32), 16), 16)
    return bits.astype(_F32)


def _rope_head(t, cos2, sin2):
    return t * cos2 + pltpu.roll(t, RET_HEAD_DIM // 2, 1) * sin2


def _meta_kernel(meta_ref, g_ref, b_ref, w32_ref, bblk_ref, ar_ref, ai_ref, cos_ref, sin_ref,
                 zmeta_ref, s5_ref, s0_ref, w_ref):
    slab = D_MODEL // 8
    for r in range(0, D_MODEL, slab):
        w_ref[r:r + slab, :] = w32_ref[r:r + slab, :].astype(_BF)
    hm = _layer_norm(meta_ref[...], g_ref[...], b_ref[...]).astype(_BF)
    u = _dot(hm, w_ref[:, 0:S5_WIDTH])
    bu = _dot(u.astype(_BF), bblk_ref[...])
    ar = ar_ref[...]
    ai = ai_ref[...]
    xr = jnp.zeros((1, S5_NSTATE), _F32)
    xi = jnp.zeros((1, S5_NSTATE), _F32)
    for t in range(N_META):
        br = bu[t:t + 1, 0:S5_NSTATE]
        bi = bu[t:t + 1, S5_NSTATE:2 * S5_NSTATE]
        xr, xi = ar * xr - ai * xi + br, ar * xi + ai * xr + bi
    s5_ref[:, 0:S5_NSTATE] = xr
    s5_ref[:, S5_NSTATE:2 * S5_NSTATE] = xi

    k = _dot(hm, w_ref[:, _OFF_K:_OFF_V])
    v = _dot(hm, w_ref[:, _OFF_V:_OFF_G]).astype(_BF)
    cos2 = cos_ref[...]
    sin2 = sin_ref[...]
    for h in range(RET_HEADS):
        sl = slice(h * RET_HEAD_DIM, (h + 1) * RET_HEAD_DIM)
        kh = _rope_head(k[:, sl], cos2, sin2) * (RET_HEAD_DIM ** -0.5)
        kz = (kh * zmeta_ref[h]).astype(_BF)
        s0_ref[h] = lax.dot_general(kz, v[:, sl], (((0,), (0,)), ((), ())),
                                    preferred_element_type=_F32)


def _in_proj_ret_kernel(x_ref, g_ref, b_ref, w_ref, cos_ref, sin_ref, s0_ref, dmat_ref, xi_ref,
                        zeta_ref, gc_ref, gng_ref, gnb_ref, wo32_ref, wup32_ref, wdn32_ref,
                        u_ref, y_ref, wo16_ref, wup16_ref, wdn16_ref,
                        q_scr, qx_scr, k_scr, v_scr, sg_scr, sloc_scr, s_scr):
    pair = pl.program_id(1)
    halves = TM_PROJ // SUB_PROJ
    heads = [(h, slice(h * RET_HEAD_DIM, (h + 1) * RET_HEAD_DIM)) for h in range(RET_HEADS)]

    wo16_ref[...] = wo32_ref[...].astype(_BF)
    wup16_ref[...] = wup32_ref[...].astype(_BF)
    wdn16_ref[...] = wdn32_ref[...].astype(_BF)

    @pl.when(pl.program_id(0) == 0)
    def _():
        for bb in range(BPS_PROJ):
            sloc_scr[bb] = s0_ref[...]

    @pl.when(pl.program_id(0) != 0)
    def _():
        for bb in range(BPS_PROJ):
            sloc_scr[bb] = s_scr[pair * BPS_PROJ + bb]

    def where(s):
        return s // halves, (s % halves) * SUB_PROJ

    def normed(s):
        bb, r0 = where(s)
        return _layer_norm(x_ref[bb, r0:r0 + SUB_PROJ, :], g_ref[...], b_ref[...]).astype(_BF)

    def project_stages(s, hn, box):
        bb, r0 = where(s)
        rows = slice(r0, r0 + SUB_PROJ)

        def st_u():
            u = _dot(hn, w_ref[:, 0:S5_WIDTH])
            start = pair * BPS_PROJ + bb + r0 * BATCH
            for j in range(S5_SLABS):
                u_ref[j, pl.ds(start, SUB_PROJ, stride=BATCH), :] = u[:, j * LANES:(j + 1) * LANES]

        def st_q():
            box['q'] = _dot(hn, w_ref[:, _OFF_Q:_OFF_K])

        def st_q_rope():
            if s + 1 < BPS_PROJ * halves:
                box['next'] = normed(s + 1)
            for h, sl in heads:
                qh = _rope_head(box['q'][:, sl], cos_ref[rows, :], sin_ref[rows, :])
                q_scr[bb, rows, sl] = qh.astype(_BF)
                xi_rows = jnp.concatenate([xi_ref[h]] * (SUB_PROJ // CHUNK), axis=0)
                qx_scr[bb, rows, sl] = (qh * xi_rows).astype(_BF)

        def st_k():
            k = _dot(hn, w_ref[:, _OFF_K:_OFF_V])
            for h, sl in heads:
                k_scr[bb, rows, sl] = (_rope_head(k[:, sl], cos_ref[rows, :], sin_ref[rows, :])
                                       * (RET_HEAD_DIM ** -0.5)).astype(_BF)

        def st_g():
            g = _dot(hn, w_ref[:, _OFF_G:IN_PROJ_WIDTH])
            sg_scr[bb, rows, :] = (g * jax.nn.sigmoid(g)).astype(_BF)

        def st_v():
            v_scr[bb, rows, :] = _dot(hn, w_ref[:, _OFF_V:_OFF_G]).astype(_BF)

        return [st_u, st_q, st_q_rope, st_k, st_g, st_v]

    def retention_stages(s):
        bb, r0 = where(s)
        stages = []
        for c in range(SUB_PROJ // CHUNK):
            rows = slice(r0 + c * CHUNK, r0 + (c + 1) * CHUNK)
            box = {}

            def wave_dots(rows=rows, box=box):
                scores, prev = [], []
                for h, cols in heads:
                    qh = q_scr[bb, rows, cols]
                    kh = k_scr[bb, rows, cols]
                    state = sloc_scr[bb, h]
                    scores.append(lax.dot_general(qh, kh, (((1,), (1,)), ((), ())),
                                                  preferred_element_type=_F32))
                    prev.append(state.astype(_BF))
                    kz = (kh.astype(_F32) * zeta_ref[h]).astype(_BF)
                    sloc_scr[bb, h] = gc_ref[h] * state + lax.dot_general(
                        kz, v_scr[bb, rows, cols], (((0,), (0,)), ((), ())),
                        preferred_element_type=_F32)
                box['scores'], box['prev'] = scores, prev

            def wave_mix(rows=rows, box=box):
                outs = []
                for h, cols in heads:
                    p = (box['scores'][h] * dmat_ref[h]).astype(_BF)
                    lhs = jnp.concatenate([p, qx_scr[bb, rows, cols]], axis=1)
                    rhs = jnp.concatenate([v_scr[bb, rows, cols], box['prev'][h]], axis=0)
                    outs.append(_dot(lhs, rhs))
                box['outs'] = outs

            def wave_norm(rows=rows, box=box):
                for h, cols in heads:
                    o = box['outs'][h]
                    mu = jnp.mean(o, axis=-1, keepdims=True)
                    oc = o - mu
                    var = jnp.mean(oc * oc, axis=-1, keepdims=True)
                    on = oc * lax.rsqrt(var + GN_EPS) * gng_ref[:, cols] + gnb_ref[:, cols]
                    y_ref[bb, rows, cols] = (sg_scr[bb, rows, cols].astype(_F32)
                                             * on).astype(_BF)

            stages += [wave_dots, wave_mix, wave_norm]
        return stages

    n_sub = BPS_PROJ * halves
    hn = normed(0)
    for s in range(n_sub):
        box = {}
        proj = project_stages(s, hn, box)
        ret = retention_stages(s - 1) if s >= 1 else []
        for i in range(max(len(proj), len(ret))):
            if i < len(proj):
                proj[i]()
            if i < len(ret):
                ret[i]()
        hn = box.get('next')
    for wave in retention_stages(n_sub - 1):
        wave()

    for bb in range(BPS_PROJ):
        s_scr[pair * BPS_PROJ + bb] = sloc_scr[bb]


def _s5_kernel(u_ref, init_ref, bblk_ref, cblk_ref, ar_ref, ai_ref, d_ref, wglu_ref, bglu_ref,
               y_ref, bu0_scr, bu1_scr, xb0_scr, xb1_scr, acc0_scr, acc1_scr, st_scr):
    @pl.when(pl.program_id(0) == 0)
    def _():
        st_scr[...] = jnp.broadcast_to(init_ref[...], (BATCH, 2 * S5_NSTATE))

    rows_sb = SB_S5 * BATCH
    n_sb = TT_S5 // SB_S5
    n_piece = 2 * S5_NSTATE // MXU_TILE
    steps_piece = SB_S5 // n_piece
    re = slice(0, S5_NSTATE)
    im = slice(S5_NSTATE, 2 * S5_NSTATE)
    ar = jnp.broadcast_to(ar_ref[...], (BATCH, S5_NSTATE))
    ai = jnp.broadcast_to(ai_ref[...], (BATCH, S5_NSTATE))

    def rows_of(j):
        if isinstance(j, int):
            return pl.ds(j * rows_sb, rows_sb)
        return pl.ds(pl.multiple_of(j * rows_sb, rows_sb), rows_sb)

    def load_u(j):
        return jnp.concatenate([u_ref[s, rows_of(j), :] for s in range(S5_SLABS)], axis=1)

    bu_scr = (bu0_scr, bu1_scr)
    xb_scr = (xb0_scr, xb1_scr)
    acc_scr = (acc0_scr, acc1_scr)

    def a_piece(ub, slot, c):
        cols = slice(c * MXU_TILE, (c + 1) * MXU_TILE)
        bu_scr[slot][:, cols] = _dot(ub, bblk_ref[:, cols])

    def b_piece(slot, c, xr, xi):
        for t in range(c * steps_piece, (c + 1) * steps_piece, 2):
            out_r, out_i = [], []
            for tt in (t, t + 1):
                rows = slice(tt * BATCH, (tt + 1) * BATCH)
                br = bu_scr[slot][rows, re]
                bi = bu_scr[slot][rows, im]
                xr, xi = ar * xr - ai * xi + br, ar * xi + ai * xr + bi
                out_r.append(xr)
                out_i.append(xi)
            rows2 = slice(t * BATCH, (t + 2) * BATCH)
            xb_scr[slot][rows2, re] = jnp.concatenate(out_r, axis=0).astype(_BF)
            xb_scr[slot][rows2, im] = jnp.concatenate(out_i, axis=0).astype(_BF)
        return xr, xi

    def c_piece(slot, c, acc):
        cols = slice(c * MXU_TILE, (c + 1) * MXU_TILE)
        part = _dot(xb_scr[slot][:, cols], cblk_ref[cols, :])
        return part if acc is None else acc + part

    def d_piece(j, slot, half):
        lo = half * (rows_sb // 2)
        if isinstance(j, int):
            rows = pl.ds(j * rows_sb + lo, rows_sb // 2)
        else:
            rows = pl.ds(pl.multiple_of(j * rows_sb + lo, rows_sb // 2), rows_sb // 2)
        u = jnp.concatenate([u_ref[s, rows, :] for s in range(S5_SLABS)], axis=1)
        y = jax.nn.gelu(acc_scr[slot][lo:lo + rows_sb // 2, :] + d_ref[...] * u)
        gate = jax.nn.sigmoid(_dot(y.astype(_BF), wglu_ref[...]) + bglu_ref[...])
        y = y * gate
        for s in range(S5_SLABS):
            y_ref[s, rows, :] = y[:, s * LANES:(s + 1) * LANES]

    def iteration(j, slot, xr, xi, do_a, do_b, do_c, do_d):
        other = 1 - slot
        ub = load_u(j + 1).astype(_BF) if do_a else None
        acc = None
        for c in range(n_piece):
            if do_a:
                a_piece(ub, other, c)
            if do_b:
                xr, xi = b_piece(slot, c, xr, xi)
            if do_c:
                acc = c_piece(other, c, acc)
            if do_d and c % (n_piece // 2) == 1:
                d_piece(j - 2, slot, c // (n_piece // 2))
        if do_c:
            acc_scr[other][...] = acc
        return xr, xi

    ub0 = load_u(0).astype(_BF)
    for c in range(n_piece):
        a_piece(ub0, 0, c)
    xr, xi = iteration(0, 0, st_scr[:, re], st_scr[:, im], True, True, False, False)
    xr, xi = iteration(1, 1, xr, xi, True, True, True, False)

    def pair(p, carry):
        j = 2 * p + 2
        xr, xi = iteration(j, 0, carry[0], carry[1], True, True, True, True)
        return iteration(j + 1, 1, xr, xi, True, True, True, True)

    xr, xi = lax.fori_loop(0, (n_sb - 4) // 2, pair, (xr, xi))
    xr, xi = iteration(n_sb - 2, 0, xr, xi, True, True, True, True)
    xr, xi = iteration(n_sb - 1, 1, xr, xi, False, True, True, True)
    st_scr[:, re] = xr
    st_scr[:, im] = xi
    iteration(n_sb, 0, xr, xi, False, False, True, True)
    iteration(n_sb + 1, 1, xr, xi, False, False, False, True)


def _ffn_kernel(x_ref, ys5_ref, yret_ref, lig_ref, lib_ref, wo_ref, l1g_ref, l1b_ref,
                wup_ref, wdn_ref, l2g_ref, l2b_ref, o_ref):
    pair = pl.program_id(1)
    n_ch = D_FF // FF_CHUNK
    piece = SUB_FFN // (2 * n_ch)
    halves = TM_FFN // SUB_FFN
    n_sub = BPS_FFN * halves

    def where(t):
        return t // halves, (t % halves) * SUB_FFN

    def mixed_of(t):
        bb, r0 = where(t)
        start = pair * BPS_FFN + bb + r0 * BATCH
        ys5 = jnp.concatenate(
            [ys5_ref[j, pl.ds(start, SUB_FFN, stride=BATCH), :] for j in range(S5_SLABS)],
            axis=1).astype(_BF)
        return (_dot(ys5, wo_ref[0:S5_WIDTH, :])
                + _dot(yret_ref[bb, r0:r0 + SUB_FFN, :], wo_ref[S5_WIDTH:D_MODEL, :]))

    def prep_rows(t, mixed, lo, hi):
        bb, r0 = where(t)
        h = _layer_norm(x_ref[bb, r0 + lo:r0 + hi, :], lig_ref[...], lib_ref[...])
        return _layer_norm(DEEPNORM_ALPHA * h + mixed[lo:hi], l1g_ref[...], l1b_ref[...])

    def ffn_up(h1b, c, floor):
        up = jnp.maximum(_dot(h1b, wup_ref[:, c * FF_CHUNK:(c + 1) * FF_CHUNK]), floor)
        return (up * up).astype(_BF)

    def ffn_down(act, c):
        return _dot(act, wdn_ref[c * FF_CHUNK:(c + 1) * FF_CHUNK, :])

    def finish_rows(t, pre, lo, hi):
        bb, r0 = where(t)
        out = _layer_norm(pre[lo:hi], l2g_ref[...], l2b_ref[...])
        o_ref[bb, r0 + lo:r0 + hi, :] = out
        return out

    h1 = prep_rows(0, mixed_of(0), 0, SUB_FFN)
    pre_prev = None
    floor = 0.0
    for t in range(n_sub):
        h1b = h1.astype(_BF)
        pre = DEEPNORM_ALPHA * h1
        mixed_next = mixed_of(t + 1) if t + 1 < n_sub else None
        next_parts = []

        def side_work(k):
            lo, hi = k * piece, (k + 1) * piece
            done = []
            if mixed_next is not None:
                next_parts.append(prep_rows(t + 1, mixed_next, lo, hi))
                done.append(next_parts[-1])
            if pre_prev is not None:
                done.append(finish_rows(t - 1, pre_prev, lo, hi))
            return done

        for c in range(n_ch):
            act = ffn_up(h1b, c, floor)
            done = side_work(2 * c)
            pre = pre + ffn_down(act, c)
            done += side_work(2 * c + 1)
            floor = _zero_after(done)
        if next_parts:
            h1 = jnp.concatenate(next_parts, axis=0)
        pre_prev = pre
    finish_rows(n_sub - 1, pre_prev, 0, SUB_FFN)


def _row(v):
    return v.reshape(1, -1).astype(_F32)


@functools.lru_cache(maxsize=None)
def _position_tables():
    f32 = np.float32

    def fn(f, a):
        return f(a.astype(np.float64)).astype(f32)

    pos = np.arange(N_META + SEQ, dtype=f32)
    expo = np.arange(0, RET_HEAD_DIM, 2, dtype=f32) / f32(RET_HEAD_DIM)
    inv_freq = (f32(1.0) / fn(lambda e: np.power(ROPE_BASE, e), expo)).astype(f32)
    ang = pos[:, None] * inv_freq[None, :]
    cos, sin = fn(np.cos, ang), fn(np.sin, ang)
    cos2 = np.concatenate([cos, cos], axis=1)
    sin2 = np.concatenate([-sin, sin], axis=1)

    heads = np.arange(RET_HEADS, dtype=f32)
    log_gamma = fn(np.log1p, -fn(np.exp2, f32(-5.0) - heads))
    idx = np.arange(CHUNK, dtype=f32)
    diff = idx[:, None] - idx[None, :]
    dmat = np.where(diff[None] >= 0,
                    fn(np.exp, np.maximum(diff, f32(0.0))[None] * log_gamma[:, None, None]),
                    f32(0.0)).astype(f32)
    zeta = fn(np.exp, (f32(CHUNK - 1.0) - idx)[None] * log_gamma[:, None])
    xi = fn(np.exp, (idx + f32(1.0))[None] * log_gamma[:, None])
    gamma_chunk = fn(np.exp, f32(CHUNK) * log_gamma)
    hd = RET_HEAD_DIM
    zeta_b = np.ascontiguousarray(np.broadcast_to(zeta[:, :, None], (RET_HEADS, CHUNK, hd)))
    xi_b = np.ascontiguousarray(np.broadcast_to(xi[:, :, None], (RET_HEADS, CHUNK, hd)))
    gc_b = np.ascontiguousarray(np.broadcast_to(gamma_chunk[:, None, None], (RET_HEADS, 1, hd)))
    return cos2, sin2, dmat, zeta_b, xi_b, gc_b


def kernel(x, meta_tokens, ln_in_g, ln_in_b, w_in, s5_lambda_re, s5_lambda_im, s5_log_dt, s5_b_re, s5_b_im, s5_c_re, s5_c_im, s5_d, s5_w_glu, s5_b_glu, ret_gn_g, ret_gn_b, w_out, ln1_g, ln1_b, w_up, w_down, ln2_g, ln2_b):
    assert x.shape == (BATCH, SEQ, D_MODEL) and w_in.shape[0] == 1
    G, P, H = S5_GROUPS, S5_STATE, S5_GROUP_CH
    arb2 = pltpu.CompilerParams(dimension_semantics=("arbitrary", "arbitrary"),
                                vmem_limit_bytes=VMEM_LIMIT)
    arb1 = pltpu.CompilerParams(dimension_semantics=("arbitrary",), vmem_limit_bytes=VMEM_LIMIT)

    lam_re, lam_im = s5_lambda_re[0], s5_lambda_im[0]
    dt = jnp.exp(s5_log_dt[0])[:, None]
    mag = jnp.exp(lam_re * dt)
    lbr = mag * jnp.cos(lam_im * dt)
    lbi = mag * jnp.sin(lam_im * dt)
    den = lam_re * lam_re + lam_im * lam_im
    nr = lbr - 1.0
    qr = (nr * lam_re + lbi * lam_im) / den
    qi = (lbi * lam_re - nr * lam_im) / den
    bbr = qr[..., None] * s5_b_re[0] - qi[..., None] * s5_b_im[0]
    bbi = qr[..., None] * s5_b_im[0] + qi[..., None] * s5_b_re[0]
    eye = jnp.eye(G, dtype=_F32)

    def blk_in(m):
        return (eye[:, None, :, None] * m.transpose(0, 2, 1)[:, :, None, :]).reshape(G * H, G * P)

    def blk_out(m):
        return (eye[:, None, :, None] * m.transpose(0, 2, 1)[:, :, None, :]).reshape(G * P, G * H)

    bblk = jnp.concatenate([blk_in(bbr), blk_in(bbi)], axis=1).astype(_BF)
    cblk = jnp.concatenate([blk_out(s5_c_re[0]), -blk_out(s5_c_im[0])], axis=0).astype(_BF)
    ar = lbr.reshape(1, S5_NSTATE)
    ai = lbi.reshape(1, S5_NSTATE)

    hd = RET_HEAD_DIM
    cos2, sin2, dmat, zeta_b, xi_b, gc_b = _position_tables()
    zmeta_b = zeta_b[:, CHUNK - N_META:, :]

    lig, lib = _row(ln_in_g), _row(ln_in_b)

    s5_init, s0, w_in_b = pl.pallas_call(
        _meta_kernel,
        out_shape=(jax.ShapeDtypeStruct((1, 2 * S5_NSTATE), _F32),
                   jax.ShapeDtypeStruct((RET_HEADS, hd, hd), _F32),
                   jax.ShapeDtypeStruct((D_MODEL, IN_PROJ_WIDTH), _BF)),
        compiler_params=pltpu.CompilerParams(vmem_limit_bytes=VMEM_LIMIT),
        name="meta_prologue",
    )(meta_tokens.astype(_F32), lig, lib, w_in[0], bblk, ar, ai,
      cos2[:N_META], sin2[:N_META], zmeta_b)

    n_t = SEQ // TM_PROJ
    rows_spec = lambda w: pl.BlockSpec((BPS_PROJ, TM_PROJ, w), lambda i, p: (p, i, 0))
    n_pairs = BATCH // BPS_PROJ
    n_steps = n_t * n_pairs

    def slab_spec(shape):
        return pl.BlockSpec((shape[0] // n_steps, shape[1]), lambda i, p: (i * n_pairs + p, 0))

    tab = lambda n: _const_spec((RET_HEADS, n, hd))
    w_shapes = [(D_MODEL, D_MODEL), (D_MODEL, D_FF), (D_FF, D_MODEL)]
    qkv_scratch = pltpu.VMEM((BPS_PROJ, TM_PROJ, RET_WIDTH), _BF)
    u_tm, y_ret, w_out_b, w_up_b, w_dn_b = pl.pallas_call(
        _in_proj_ret_kernel,
        grid=(n_t, n_pairs),
        in_specs=[rows_spec(D_MODEL), _const_spec((1, D_MODEL)), _const_spec((1, D_MODEL)),
                  _const_spec((D_MODEL, IN_PROJ_WIDTH)),
                  pl.BlockSpec((TM_PROJ, hd), lambda i, p: (i, 0)),
                  pl.BlockSpec((TM_PROJ, hd), lambda i, p: (i, 0)),
                  tab(hd), tab(CHUNK), tab(CHUNK), tab(CHUNK), tab(1),
                  _const_spec((1, RET_WIDTH)), _const_spec((1, RET_WIDTH))]
                 + [slab_spec(s) for s in w_shapes],
        out_specs=[pl.BlockSpec((S5_SLABS, TM_PROJ * BATCH, LANES), lambda i, p: (0, i, 0)),
                   rows_spec(RET_WIDTH)] + [slab_spec(s) for s in w_shapes],
        out_shape=(jax.ShapeDtypeStruct((S5_SLABS, SEQ * BATCH, LANES), _F32),
                   jax.ShapeDtypeStruct((BATCH, SEQ, RET_WIDTH), _BF))
                  + tuple(jax.ShapeDtypeStruct(s, _BF) for s in w_shapes),
        scratch_shapes=[qkv_scratch, qkv_scratch, qkv_scratch, qkv_scratch, qkv_scratch,
                        pltpu.VMEM((BPS_PROJ, RET_HEADS, hd, hd), _F32),
                        pltpu.VMEM((BATCH, RET_HEADS, hd, hd), _F32)],
        compiler_params=arb2,
        name="in_proj_ret",
    )(x, lig, lib, w_in_b, cos2[N_META:], sin2[N_META:], s0, dmat, xi_b, zeta_b, gc_b,
      _row(ret_gn_g[0]), _row(ret_gn_b[0]), w_out[0], w_up[0], w_down[0])

    rows_s5 = TT_S5 * BATCH
    ys5_tm = pl.pallas_call(
        _s5_kernel,
        grid=(SEQ // TT_S5,),
        in_specs=[pl.BlockSpec((S5_SLABS, rows_s5, LANES), lambda i: (0, i, 0)),
                  _const_spec((1, 2 * S5_NSTATE)),
                  _const_spec((S5_WIDTH, 2 * S5_NSTATE)), _const_spec((2 * S5_NSTATE, S5_WIDTH)),
                  _const_spec((1, S5_NSTATE)), _const_spec((1, S5_NSTATE)),
                  _const_spec((1, S5_WIDTH)), _const_spec((S5_WIDTH, S5_WIDTH)),
                  _const_spec((1, S5_WIDTH))],
        out_specs=pl.BlockSpec((S5_SLABS, rows_s5, LANES), lambda i: (0, i, 0)),
        out_shape=jax.ShapeDtypeStruct((S5_SLABS, SEQ * BATCH, LANES), _F32),
        scratch_shapes=[pltpu.VMEM((SB_S5 * BATCH, 2 * S5_NSTATE), _F32),
                        pltpu.VMEM((SB_S5 * BATCH, 2 * S5_NSTATE), _F32),
                        pltpu.VMEM((SB_S5 * BATCH, 2 * S5_NSTATE), _BF),
                        pltpu.VMEM((SB_S5 * BATCH, 2 * S5_NSTATE), _BF),
                        pltpu.VMEM((SB_S5 * BATCH, S5_WIDTH), _F32),
                        pltpu.VMEM((SB_S5 * BATCH, S5_WIDTH), _F32),
                        pltpu.VMEM((BATCH, 2 * S5_NSTATE), _F32)],
        compiler_params=arb1,
        name="s5_scan",
    )(u_tm, s5_init, bblk, cblk, ar, ai, _row(s5_d[0]), s5_w_glu[0].astype(_BF), _row(s5_b_glu[0]))

    n_f = SEQ // TM_FFN
    frow = lambda w: pl.BlockSpec((BPS_FFN, TM_FFN, w), lambda i, p: (p, i, 0))
    out = pl.pallas_call(
        _ffn_kernel,
        grid=(n_f, BATCH // BPS_FFN),
        in_specs=[frow(D_MODEL),
                  pl.BlockSpec((S5_SLABS, TM_FFN * BATCH, LANES), lambda i, p: (0, i, 0)),
                  frow(RET_WIDTH),
                  _const_spec((1, D_MODEL)), _const_spec((1, D_MODEL)),
                  _const_spec((D_MODEL, D_MODEL)),
                  _const_spec((1, D_MODEL)), _const_spec((1, D_MODEL)),
                  _const_spec((D_MODEL, D_FF)), _const_spec((D_FF, D_MODEL)),
                  _const_spec((1, D_MODEL)), _const_spec((1, D_MODEL))],
        out_specs=frow(D_MODEL),
        out_shape=jax.ShapeDtypeStruct((BATCH, SEQ, D_MODEL), _F32),
        compiler_params=arb2,
        name="out_ffn",
    )(x, ys5_tm, y_ret, lig, lib, w_out_b,
      _row(ln1_g[0]), _row(ln1_b[0]), w_up_b, w_dn_b, _row(ln2_g[0]), _row(ln2_b[0]))

    return out
```

```python
import functools

import jax
import jax.numpy as jnp
import numpy as np
from jax import lax
from jax.experimental import pallas as pl
from jax.experimental.pallas import tpu as pltpu

D_MODEL = 1024
BATCH = 8
SEQ = 4096
N_META = 16
S5_GROUP_CH = 16
S5_STATE = 64
S5_WIDTH = 256
S5_GROUPS = S5_WIDTH // S5_GROUP_CH
S5_NSTATE = S5_GROUPS * S5_STATE
RET_HEAD_DIM = 128
RET_WIDTH = 768
RET_HEADS = RET_WIDTH // RET_HEAD_DIM
CHUNK = 128
ROPE_BASE = 10000.0
D_FF = 4 * D_MODEL
LANES = 128
S5_SLABS = S5_WIDTH // LANES
LN_EPS = 1e-5
GN_EPS = 1e-5
IN_PROJ_WIDTH = S5_WIDTH + 4 * RET_WIDTH
DEEPNORM_ALPHA = 2.0 ** 0.25

_OFF_Q = S5_WIDTH
_OFF_K = _OFF_Q + RET_WIDTH
_OFF_V = _OFF_K + RET_WIDTH
_OFF_G = _OFF_V + RET_WIDTH

TM_PROJ = 512
BPS_PROJ = 2
SUB_PROJ = 256
MXU_TILE = 256
TT_S5 = 1024
SB_S5 = 32
TM_FFN = 512
BPS_FFN = 2
SUB_FFN = 256
FF_CHUNK = 1024
VMEM_LIMIT = 56 * 1024 * 1024

_BF = jnp.bfloat16
_F32 = jnp.float32


def _const_spec(shape):
    nd = len(shape)
    return pl.BlockSpec(shape, lambda *_: (0,) * nd, pipeline_mode=pl.Buffered(1))


def _layer_norm(x, g, b):
    mu = jnp.mean(x, axis=-1, keepdims=True)
    xc = x - mu
    var = jnp.mean(xc * xc, axis=-1, keepdims=True)
    return xc * lax.rsqrt(var + LN_EPS) * g + b


def _dot(a, b):
    return jnp.dot(a, b, preferred_element_type=_F32)


def _zero_after(values):
    if not values:
        return 0.0
    tok = values[0][0:1, 0:1]
    for v in values[1:]:
        tok = tok + v[0:1, 0:1]
    bits = lax.shift_right_logical(lax.shift_right_logical(tok.astype(jnp.int32), 16), 16)
    return bits.astype(_F32)


def _rope_head(t, cos2, sin2):
    return t * cos2 + pltpu.roll(t, RET_HEAD_DIM // 2, 1) * sin2


def _meta_kernel(meta_ref, g_ref, b_ref, w32_ref, bblk_ref, ar_ref, ai_ref, cos_ref, sin_ref,
                 zmeta_ref, s5_ref, s0_ref, w_ref):
    slab = D_MODEL // 8
    for r in range(0, D_MODEL, slab):
        w_ref[r:r + slab, :] = w32_ref[r:r + slab, :].astype(_BF)
    hm = _layer_norm(meta_ref[...], g_ref[...], b_ref[...]).astype(_BF)
    u = _dot(hm, w_ref[:, 0:S5_WIDTH])
    bu = _dot(u.astype(_BF), bblk_ref[...])
    ar = ar_ref[...]
    ai = ai_ref[...]
    xr = jnp.zeros((1, S5_NSTATE), _F32)
    xi = jnp.zeros((1, S5_NSTATE), _F32)
    for t in range(N_META):
        br = bu[t:t + 1, 0:S5_NSTATE]
        bi = bu[t:t + 1, S5_NSTATE:2 * S5_NSTATE]
        xr, xi = ar * xr - ai * xi + br, ar * xi + ai * xr + bi
    s5_ref[:, 0:S5_NSTATE] = xr
    s5_ref[:, S5_NSTATE:2 * S5_NSTATE] = xi

    k = _dot(hm, w_ref[:, _OFF_K:_OFF_V])
    v = _dot(hm, w_ref[:, _OFF_V:_OFF_G]).astype(_BF)
    cos2 = cos_ref[...]
    sin2 = sin_ref[...]
    for h in range(RET_HEADS):
        sl = slice(h * RET_HEAD_DIM, (h + 1) * RET_HEAD_DIM)
        kh = _rope_head(k[:, sl], cos2, sin2) * (RET_HEAD_DIM ** -0.5)
        kz = (kh * zmeta_ref[h]).astype(_BF)
        s0_ref[h] = lax.dot_general(kz, v[:, sl], (((0,), (0,)), ((), ())),
                                    preferred_element_type=_F32)


def _in_proj_ret_kernel(x_ref, g_ref, b_ref, w_ref, cos_ref, sin_ref, s0_ref, dmat_ref, xi_ref,
                        zeta_ref, gc_ref, gng_ref, gnb_ref, wo32_ref, wup32_ref, wdn32_ref,
                        u_ref, y_ref, wo16_ref, wup16_ref, wdn16_ref,
                        q_scr, k_scr, v_scr, sg_scr, sloc_scr, s_scr):
    pair = pl.program_id(1)
    halves = TM_PROJ // SUB_PROJ
    heads = [(h, slice(h * RET_HEAD_DIM, (h + 1) * RET_HEAD_DIM)) for h in range(RET_HEADS)]

    wo16_ref[...] = wo32_ref[...].astype(_BF)
    wup16_ref[...] = wup32_ref[...].astype(_BF)
    wdn16_ref[...] = wdn32_ref[...].astype(_BF)

    @pl.when(pl.program_id(0) == 0)
    def _():
        for bb in range(BPS_PROJ):
            sloc_scr[bb] = s0_ref[...]

    @pl.when(pl.program_id(0) != 0)
    def _():
        for bb in range(BPS_PROJ):
            sloc_scr[bb] = s_scr[pair * BPS_PROJ + bb]

    def where(s):
        return s // halves, (s % halves) * SUB_PROJ

    def normed(s):
        bb, r0 = where(s)
        return _layer_norm(x_ref[bb, r0:r0 + SUB_PROJ, :], g_ref[...], b_ref[...]).astype(_BF)

    def project_stages(s, hn, box):
        bb, r0 = where(s)
        rows = slice(r0, r0 + SUB_PROJ)

        def st_u():
            u = _dot(hn, w_ref[:, 0:S5_WIDTH])
            start = pair * BPS_PROJ + bb + r0 * BATCH
            for j in range(S5_SLABS):
                u_ref[j, pl.ds(start, SUB_PROJ, stride=BATCH), :] = u[:, j * LANES:(j + 1) * LANES]

        def st_q():
            box['q'] = _dot(hn, w_ref[:, _OFF_Q:_OFF_K])

        def st_q_rope():
            if s + 1 < BPS_PROJ * halves:
                box['next'] = normed(s + 1)
            for h, sl in heads:
                q_scr[bb, rows, sl] = _rope_head(box['q'][:, sl], cos_ref[rows, :],
                                                 sin_ref[rows, :]).astype(_BF)

        def st_k():
            k = _dot(hn, w_ref[:, _OFF_K:_OFF_V])
            for h, sl in heads:
                k_scr[bb, rows, sl] = (_rope_head(k[:, sl], cos_ref[rows, :], sin_ref[rows, :])
                                       * (RET_HEAD_DIM ** -0.5)).astype(_BF)

        def st_g():
            g = _dot(hn, w_ref[:, _OFF_G:IN_PROJ_WIDTH])
            sg_scr[bb, rows, :] = (g * jax.nn.sigmoid(g)).astype(_BF)

        def st_v():
            v_scr[bb, rows, :] = _dot(hn, w_ref[:, _OFF_V:_OFF_G]).astype(_BF)

        return [st_q, st_q_rope, st_k, st_v, st_g, st_u]

    def retention_stages(s):
        bb, r0 = where(s)
        stages = []
        for c in range(SUB_PROJ // CHUNK):
            rows = slice(r0 + c * CHUNK, r0 + (c + 1) * CHUNK)
            box = {}

            def wave_dots(rows=rows, box=box):
                scores, cross = [], []
                for h, cols in heads:
                    qh = q_scr[bb, rows, cols]
                    kh = k_scr[bb, rows, cols]
                    state = sloc_scr[bb, h]
                    scores.append(lax.dot_general(qh, kh, (((1,), (1,)), ((), ())),
                                                  preferred_element_type=_F32))
                    cross.append(_dot(qh, state.astype(_BF)))
                    kz = (kh.astype(_F32) * zeta_ref[h]).astype(_BF)
                    sloc_scr[bb, h] = gc_ref[h] * state + lax.dot_general(
                        kz, v_scr[bb, rows, cols], (((0,), (0,)), ((), ())),
                        preferred_element_type=_F32)
                box['scores'], box['cross'] = scores, cross

            def wave_mix(rows=rows, box=box):
                box['outs'] = [
                    _dot((box['scores'][h] * dmat_ref[h]).astype(_BF), v_scr[bb, rows, cols])
                    + box['cross'][h] * xi_ref[h] for h, cols in heads]

            def wave_norm(rows=rows, box=box):
                for h, cols in heads:
                    o = box['outs'][h]
                    mu = jnp.mean(o, axis=-1, keepdims=True)
                    oc = o - mu
                    var = jnp.mean(oc * oc, axis=-1, keepdims=True)
                    on = oc * lax.rsqrt(var + GN_EPS) * gng_ref[:, cols] + gnb_ref[:, cols]
                    y_ref[bb, rows, cols] = (sg_scr[bb, rows, cols].astype(_F32)
                                             * on).astype(_BF)

            stages += [wave_dots, wave_mix, wave_norm]
        return stages

    n_sub = BPS_PROJ * halves
    assert SUB_PROJ == 2 * CHUNK
    hn = normed(0)
    late = []
    for s in range(n_sub):
        box = {}
        proj = project_stages(s, hn, box)
        waves = retention_stages(s)
        ret = (late or [None] * 3) + waves[:3]
        late = waves[3:]
        for stage, wave in zip(proj, ret):
            stage()
            if wave is not None:
                wave()
        hn = box.get('next')
    for wave in late:
        wave()

    for bb in range(BPS_PROJ):
        s_scr[pair * BPS_PROJ + bb] = sloc_scr[bb]


def _s5_kernel(u_ref, init_ref, bblk_ref, cblk_ref, ar_ref, ai_ref, d_ref, wglu_ref, bglu_ref,
               y_ref, bu0_scr, bu1_scr, xb0_scr, xb1_scr, acc0_scr, acc1_scr, st_scr):
    @pl.when(pl.program_id(0) == 0)
    def _():
        st_scr[...] = jnp.broadcast_to(init_ref[...], (BATCH, 2 * S5_NSTATE))

    rows_sb = SB_S5 * BATCH
    n_sb = TT_S5 // SB_S5
    n_piece = 2 * S5_NSTATE // MXU_TILE
    steps_piece = SB_S5 // n_piece
    re = slice(0, S5_NSTATE)
    im = slice(S5_NSTATE, 2 * S5_NSTATE)
    ar = jnp.broadcast_to(ar_ref[...], (BATCH, S5_NSTATE))
    ai = jnp.broadcast_to(ai_ref[...], (BATCH, S5_NSTATE))

    def rows_of(j):
        if isinstance(j, int):
            return pl.ds(j * rows_sb, rows_sb)
        return pl.ds(pl.multiple_of(j * rows_sb, rows_sb), rows_sb)

    def load_u(j):
        return jnp.concatenate([u_ref[s, rows_of(j), :] for s in range(S5_SLABS)], axis=1)

    bu_scr = (bu0_scr, bu1_scr)
    xb_scr = (xb0_scr, xb1_scr)
    acc_scr = (acc0_scr, acc1_scr)

    def a_piece(ub, slot, c):
        cols = slice(c * MXU_TILE, (c + 1) * MXU_TILE)
        bu_scr[slot][:, cols] = _dot(ub, bblk_ref[:, cols])

    def b_piece(slot, c, xr, xi):
        for t in range(c * steps_piece, (c + 1) * steps_piece, 2):
            out_r, out_i = [], []
            for tt in (t, t + 1):
                rows = slice(tt * BATCH, (tt + 1) * BATCH)
                br = bu_scr[slot][rows, re]
                bi = bu_scr[slot][rows, im]
                xr, xi = ar * xr - ai * xi + br, ar * xi + ai * xr + bi
                out_r.append(xr)
                out_i.append(xi)
            rows2 = slice(t * BATCH, (t + 2) * BATCH)
            xb_scr[slot][rows2, re] = jnp.concatenate(out_r, axis=0).astype(_BF)
            xb_scr[slot][rows2, im] = jnp.concatenate(out_i, axis=0).astype(_BF)
        return xr, xi

    def c_piece(slot, c, acc):
        cols = slice(c * MXU_TILE, (c + 1) * MXU_TILE)
        part = _dot(xb_scr[slot][:, cols], cblk_ref[cols, :])
        return part if acc is None else acc + part

    def d_piece(j, slot, half):
        lo = half * (rows_sb // 2)
        if isinstance(j, int):
            rows = pl.ds(j * rows_sb + lo, rows_sb // 2)
        else:
            rows = pl.ds(pl.multiple_of(j * rows_sb + lo, rows_sb // 2), rows_sb // 2)
        u = jnp.concatenate([u_ref[s, rows, :] for s in range(S5_SLABS)], axis=1)
        y = jax.nn.gelu(acc_scr[slot][lo:lo + rows_sb // 2, :] + d_ref[...] * u)
        gate = jax.nn.sigmoid(_dot(y.astype(_BF), wglu_ref[...]) + bglu_ref[...])
        y = y * gate
        for s in range(S5_SLABS):
            y_ref[s, rows, :] = y[:, s * LANES:(s + 1) * LANES]

    def iteration(j, slot, xr, xi, do_a, do_b, do_c, do_d):
        other = 1 - slot
        ub = load_u(j + 1).astype(_BF) if do_a else None
        acc = None
        for c in range(n_piece):
            if do_a:
                a_piece(ub, other, c)
            if do_b:
                xr, xi = b_piece(slot, c, xr, xi)
            if do_c:
                acc = c_piece(other, c, acc)
            if do_d and c % (n_piece // 2) == 1:
                d_piece(j - 2, slot, c // (n_piece // 2))
        if do_c:
            acc_scr[other][...] = acc
        return xr, xi

    ub0 = load_u(0).astype(_BF)
    for c in range(n_piece):
        a_piece(ub0, 0, c)
    xr, xi = iteration(0, 0, st_scr[:, re], st_scr[:, im], True, True, False, False)
    xr, xi = iteration(1, 1, xr, xi, True, True, True, False)

    def pair(p, carry):
        j = 2 * p + 2
        xr, xi = iteration(j, 0, carry[0], carry[1], True, True, True, True)
        return iteration(j + 1, 1, xr, xi, True, True, True, True)

    xr, xi = lax.fori_loop(0, (n_sb - 4) // 2, pair, (xr, xi))
    xr, xi = iteration(n_sb - 2, 0, xr, xi, True, True, True, True)
    xr, xi = iteration(n_sb - 1, 1, xr, xi, False, True, True, True)
    st_scr[:, re] = xr
    st_scr[:, im] = xi
    iteration(n_sb, 0, xr, xi, False, False, True, True)
    iteration(n_sb + 1, 1, xr, xi, False, False, False, True)


def _ffn_kernel(x_ref, ys5_ref, yret_ref, lig_ref, lib_ref, wo_ref, l1g_ref, l1b_ref,
                wup_ref, wdn_ref, l2g_ref, l2b_ref, o_ref):
    pair = pl.program_id(1)
    n_ch = D_FF // FF_CHUNK
    piece = SUB_FFN // (2 * n_ch)
    halves = TM_FFN // SUB_FFN
    n_sub = BPS_FFN * halves

    def where(t):
        return t // halves, (t % halves) * SUB_FFN

    def mixed_of(t):
        bb, r0 = where(t)
        start = pair * BPS_FFN + bb + r0 * BATCH
        ys5 = jnp.concatenate(
            [ys5_ref[j, pl.ds(start, SUB_FFN, stride=BATCH), :] for j in range(S5_SLABS)],
            axis=1).astype(_BF)
        return (_dot(ys5, wo_ref[0:S5_WIDTH, :])
                + _dot(yret_ref[bb, r0:r0 + SUB_FFN, :], wo_ref[S5_WIDTH:D_MODEL, :]))

    def prep_rows(t, mixed, lo, hi):
        bb, r0 = where(t)
        h = _layer_norm(x_ref[bb, r0 + lo:r0 + hi, :], lig_ref[...], lib_ref[...])
        return _layer_norm(DEEPNORM_ALPHA * h + mixed[lo:hi], l1g_ref[...], l1b_ref[...])

    def ffn_up(h1b, c, floor):
        up = jnp.maximum(_dot(h1b, wup_ref[:, c * FF_CHUNK:(c + 1) * FF_CHUNK]), floor)
        return (up * up).astype(_BF)

    def ffn_down(act, c):
        return _dot(act, wdn_ref[c * FF_CHUNK:(c + 1) * FF_CHUNK, :])

    def finish_rows(t, pre, lo, hi):
        bb, r0 = where(t)
        out = _layer_norm(pre[lo:hi], l2g_ref[...], l2b_ref[...])
        o_ref[bb, r0 + lo:r0 + hi, :] = out
        return out

    h1 = prep_rows(0, mixed_of(0), 0, SUB_FFN)
    pre_prev = None
    floor = 0.0
    for t in range(n_sub):
        h1b = h1.astype(_BF)
        pre = DEEPNORM_ALPHA * h1
        mixed_next = mixed_of(t + 1) if t + 1 < n_sub else None
        next_parts = []

        def side_work(k):
            lo, hi = k * piece, (k + 1) * piece
            done = []
            if mixed_next is not None:
                next_parts.append(prep_rows(t + 1, mixed_next, lo, hi))
                done.append(next_parts[-1])
            if pre_prev is not None:
                done.append(finish_rows(t - 1, pre_prev, lo, hi))
            return done

        for c in range(n_ch):
            act = ffn_up(h1b, c, floor)
            done = side_work(2 * c)
            pre = pre + ffn_down(act, c)
            done += side_work(2 * c + 1)
            floor = _zero_after(done)
        if next_parts:
            h1 = jnp.concatenate(next_parts, axis=0)
        pre_prev = pre
    finish_rows(n_sub - 1, pre_prev, 0, SUB_FFN)


def _row(v):
    return v.reshape(1, -1).astype(_F32)


@functools.lru_cache(maxsize=None)
def _position_tables():
    f32 = np.float32

    def fn(f, a):
        return f(a.astype(np.float64)).astype(f32)

    pos = np.arange(N_META + SEQ, dtype=f32)
    expo = np.arange(0, RET_HEAD_DIM, 2, dtype=f32) / f32(RET_HEAD_DIM)
    inv_freq = (f32(1.0) / fn(lambda e: np.power(ROPE_BASE, e), expo)).astype(f32)
    ang = pos[:, None] * inv_freq[None, :]
    cos, sin = fn(np.cos, ang), fn(np.sin, ang)
    cos2 = np.concatenate([cos, cos], axis=1)
    sin2 = np.concatenate([-sin, sin], axis=1)

    heads = np.arange(RET_HEADS, dtype=f32)
    log_gamma = fn(np.log1p, -fn(np.exp2, f32(-5.0) - heads))
    idx = np.arange(CHUNK, dtype=f32)
    diff = idx[:, None] - idx[None, :]
    dmat = np.where(diff[None] >= 0,
                    fn(np.exp, np.maximum(diff, f32(0.0))[None] * log_gamma[:, None, None]),
                    f32(0.0)).astype(f32)
    zeta = fn(np.exp, (f32(CHUNK - 1.0) - idx)[None] * log_gamma[:, None])
    xi = fn(np.exp, (idx + f32(1.0))[None] * log_gamma[:, None])
    gamma_chunk = fn(np.exp, f32(CHUNK) * log_gamma)
    hd = RET_HEAD_DIM
    zeta_b = np.ascontiguousarray(np.broadcast_to(zeta[:, :, None], (RET_HEADS, CHUNK, hd)))
    xi_b = np.ascontiguousarray(np.broadcast_to(xi[:, :, None], (RET_HEADS, CHUNK, hd)))
    gc_b = np.ascontiguousarray(np.broadcast_to(gamma_chunk[:, None, None], (RET_HEADS, 1, hd)))
    return cos2, sin2, dmat, zeta_b, xi_b, gc_b


def kernel(x, meta_tokens, ln_in_g, ln_in_b, w_in, s5_lambda_re, s5_lambda_im, s5_log_dt, s5_b_re, s5_b_im, s5_c_re, s5_c_im, s5_d, s5_w_glu, s5_b_glu, ret_gn_g, ret_gn_b, w_out, ln1_g, ln1_b, w_up, w_down, ln2_g, ln2_b):
    assert x.shape == (BATCH, SEQ, D_MODEL) and w_in.shape[0] == 1
    G, P, H = S5_GROUPS, S5_STATE, S5_GROUP_CH
    arb2 = pltpu.CompilerParams(dimension_semantics=("arbitrary", "arbitrary"),
                                vmem_limit_bytes=VMEM_LIMIT)
    arb1 = pltpu.CompilerParams(dimension_semantics=("arbitrary",), vmem_limit_bytes=VMEM_LIMIT)

    lam_re, lam_im = s5_lambda_re[0], s5_lambda_im[0]
    dt = jnp.exp(s5_log_dt[0])[:, None]
    mag = jnp.exp(lam_re * dt)
    lbr = mag * jnp.cos(lam_im * dt)
    lbi = mag * jnp.sin(lam_im * dt)
    den = lam_re * lam_re + lam_im * lam_im
    nr = lbr - 1.0
    qr = (nr * lam_re + lbi * lam_im) / den
    qi = (lbi * lam_re - nr * lam_im) / den
    bbr = qr[..., None] * s5_b_re[0] - qi[..., None] * s5_b_im[0]
    bbi = qr[..., None] * s5_b_im[0] + qi[..., None] * s5_b_re[0]
    eye = jnp.eye(G, dtype=_F32)

    def blk_in(m):
        return (eye[:, None, :, None] * m.transpose(0, 2, 1)[:, :, None, :]).reshape(G * H, G * P)

    def blk_out(m):
        return (eye[:, None, :, None] * m.transpose(0, 2, 1)[:, :, None, :]).reshape(G * P, G * H)

    bblk = jnp.concatenate([blk_in(bbr), blk_in(bbi)], axis=1).astype(_BF)
    cblk = jnp.concatenate([blk_out(s5_c_re[0]), -blk_out(s5_c_im[0])], axis=0).astype(_BF)
    ar = lbr.reshape(1, S5_NSTATE)
    ai = lbi.reshape(1, S5_NSTATE)

    hd = RET_HEAD_DIM
    cos2, sin2, dmat, zeta_b, xi_b, gc_b = _position_tables()
    zmeta_b = zeta_b[:, CHUNK - N_META:, :]

    lig, lib = _row(ln_in_g), _row(ln_in_b)

    s5_init, s0, w_in_b = pl.pallas_call(
        _meta_kernel,
        out_shape=(jax.ShapeDtypeStruct((1, 2 * S5_NSTATE), _F32),
                   jax.ShapeDtypeStruct((RET_HEADS, hd, hd), _F32),
                   jax.ShapeDtypeStruct((D_MODEL, IN_PROJ_WIDTH), _BF)),
        compiler_params=pltpu.CompilerParams(vmem_limit_bytes=VMEM_LIMIT),
        name="meta_prologue",
    )(meta_tokens.astype(_F32), lig, lib, w_in[0], bblk, ar, ai,
      cos2[:N_META], sin2[:N_META], zmeta_b)

    n_t = SEQ // TM_PROJ
    rows_spec = lambda w: pl.BlockSpec((BPS_PROJ, TM_PROJ, w), lambda i, p: (p, i, 0))
    n_pairs = BATCH // BPS_PROJ
    n_steps = n_t * n_pairs

    def slab_spec(shape):
        return pl.BlockSpec((shape[0] // n_steps, shape[1]), lambda i, p: (i * n_pairs + p, 0))

    tab = lambda n: _const_spec((RET_HEADS, n, hd))
    w_shapes = [(D_MODEL, D_MODEL), (D_MODEL, D_FF), (D_FF, D_MODEL)]
    qkv_scratch = pltpu.VMEM((BPS_PROJ, TM_PROJ, RET_WIDTH), _BF)
    u_tm, y_ret, w_out_b, w_up_b, w_dn_b = pl.pallas_call(
        _in_proj_ret_kernel,
        grid=(n_t, n_pairs),
        in_specs=[rows_spec(D_MODEL), _const_spec((1, D_MODEL)), _const_spec((1, D_MODEL)),
                  _const_spec((D_MODEL, IN_PROJ_WIDTH)),
                  pl.BlockSpec((TM_PROJ, hd), lambda i, p: (i, 0)),
                  pl.BlockSpec((TM_PROJ, hd), lambda i, p: (i, 0)),
                  tab(hd), tab(CHUNK), tab(CHUNK), tab(CHUNK), tab(1),
                  _const_spec((1, RET_WIDTH)), _const_spec((1, RET_WIDTH))]
                 + [slab_spec(s) for s in w_shapes],
        out_specs=[pl.BlockSpec((S5_SLABS, TM_PROJ * BATCH, LANES), lambda i, p: (0, i, 0)),
                   rows_spec(RET_WIDTH)] + [slab_spec(s) for s in w_shapes],
        out_shape=(jax.ShapeDtypeStruct((S5_SLABS, SEQ * BATCH, LANES), _F32),
                   jax.ShapeDtypeStruct((BATCH, SEQ, RET_WIDTH), _BF))
                  + tuple(jax.ShapeDtypeStruct(s, _BF) for s in w_shapes),
        scratch_shapes=[qkv_scratch, qkv_scratch, qkv_scratch, qkv_scratch,
                        pltpu.VMEM((BPS_PROJ, RET_HEADS, hd, hd), _F32),
                        pltpu.VMEM((BATCH, RET_HEADS, hd, hd), _F32)],
        compiler_params=arb2,
        name="in_proj_ret",
    )(x, lig, lib, w_in_b, cos2[N_META:], sin2[N_META:], s0, dmat, xi_b, zeta_b, gc_b,
      _row(ret_gn_g[0]), _row(ret_gn_b[0]), w_out[0], w_up[0], w_down[0])

    rows_s5 = TT_S5 * BATCH
    ys5_tm = pl.pallas_call(
        _s5_kernel,
        grid=(SEQ // TT_S5,),
        in_specs=[pl.BlockSpec((S5_SLABS, rows_s5, LANES), lambda i: (0, i, 0)),
                  _const_spec((1, 2 * S5_NSTATE)),
                  _const_spec((S5_WIDTH, 2 * S5_NSTATE)), _const_spec((2 * S5_NSTATE, S5_WIDTH)),
                  _const_spec((1, S5_NSTATE)), _const_spec((1, S5_NSTATE)),
                  _const_spec((1, S5_WIDTH)), _const_spec((S5_WIDTH, S5_WIDTH)),
                  _const_spec((1, S5_WIDTH))],
        out_specs=pl.BlockSpec((S5_SLABS, rows_s5, LANES), lambda i: (0, i, 0)),
        out_shape=jax.ShapeDtypeStruct((S5_SLABS, SEQ * BATCH, LANES), _F32),
        scratch_shapes=[pltpu.VMEM((SB_S5 * BATCH, 2 * S5_NSTATE), _F32),
                        pltpu.VMEM((SB_S5 * BATCH, 2 * S5_NSTATE), _F32),
                        pltpu.VMEM((SB_S5 * BATCH, 2 * S5_NSTATE), _BF),
                        pltpu.VMEM((SB_S5 * BATCH, 2 * S5_NSTATE), _BF),
                        pltpu.VMEM((SB_S5 * BATCH, S5_WIDTH), _F32),
                        pltpu.VMEM((SB_S5 * BATCH, S5_WIDTH), _F32),
                        pltpu.VMEM((BATCH, 2 * S5_NSTATE), _F32)],
        compiler_params=arb1,
        name="s5_scan",
    )(u_tm, s5_init, bblk, cblk, ar, ai, _row(s5_d[0]), s5_w_glu[0].astype(_BF), _row(s5_b_glu[0]))

    n_f = SEQ // TM_FFN
    frow = lambda w: pl.BlockSpec((BPS_FFN, TM_FFN, w), lambda i, p: (p, i, 0))
    out = pl.pallas_call(
        _ffn_kernel,
        grid=(n_f, BATCH // BPS_FFN),
        in_specs=[frow(D_MODEL),
                  pl.BlockSpec((S5_SLABS, TM_FFN * BATCH, LANES), lambda i, p: (0, i, 0)),
                  frow(RET_WIDTH),
                  _const_spec((1, D_MODEL)), _const_spec((1, D_MODEL)),
                  _const_spec((D_MODEL, D_MODEL)),
                  _const_spec((1, D_MODEL)), _const_spec((1, D_MODEL)),
                  _const_spec((D_MODEL, D_FF)), _const_spec((D_FF, D_MODEL)),
                  _const_spec((1, D_MODEL)), _const_spec((1, D_MODEL))],
        out_specs=frow(D_MODEL),
        out_shape=jax.ShapeDtypeStruct((BATCH, SEQ, D_MODEL), _F32),
        compiler_params=arb2,
        name="out_ffn",
    )(x, ys5_tm, y_ret, lig, lib, w_out_b,
      _row(ln1_g[0]), _row(ln1_b[0]), w_up_b, w_dn_b, _row(ln2_g[0]), _row(ln2_b[0]))

    return out
```

```python
import functools

import jax
import jax.numpy as jnp
import numpy as np
from jax import lax
from jax.experimental import pallas as pl
from jax.experimental.pallas import tpu as pltpu

D_MODEL = 1024
BATCH = 8
SEQ = 4096
N_META = 16
S5_GROUP_CH = 16
S5_STATE = 64
S5_WIDTH = 256
S5_GROUPS = S5_WIDTH // S5_GROUP_CH
S5_NSTATE = S5_GROUPS * S5_STATE
RET_HEAD_DIM = 128
RET_WIDTH = 768
RET_HEADS = RET_WIDTH // RET_HEAD_DIM
CHUNK = 128
ROPE_BASE = 10000.0
D_FF = 4 * D_MODEL
LANES = 128
S5_SLABS = S5_WIDTH // LANES
LN_EPS = 1e-5
GN_EPS = 1e-5
IN_PROJ_WIDTH = S5_WIDTH + 4 * RET_WIDTH
DEEPNORM_ALPHA = 2.0 ** 0.25

_OFF_Q = S5_WIDTH
_OFF_K = _OFF_Q + RET_WIDTH
_OFF_V = _OFF_K + RET_WIDTH
_OFF_G = _OFF_V + RET_WIDTH

TM_PROJ = 512
BPS_PROJ = 2
SUB_PROJ = 256
MXU_TILE = 256
TT_S5 = 1024
SB_S5 = 32
TM_FFN = 512
BPS_FFN = 2
SUB_FFN = 256
FF_CHUNK = 1024
VMEM_LIMIT = 56 * 1024 * 1024

_BF = jnp.bfloat16
_F32 = jnp.float32


def _const_spec(shape):
    nd = len(shape)
    return pl.BlockSpec(shape, lambda *_: (0,) * nd, pipeline_mode=pl.Buffered(1))


def _layer_norm(x, g, b):
    mu = jnp.mean(x, axis=-1, keepdims=True)
    xc = x - mu
    var = jnp.mean(xc * xc, axis=-1, keepdims=True)
    return xc * lax.rsqrt(var + LN_EPS) * g + b


def _dot(a, b):
    return jnp.dot(a, b, preferred_element_type=_F32)


def _zero_after(values):
    if not values:
        return 0.0
    tok = values[0][0:1, 0:1]
    for v in values[1:]:
        tok = tok + v[0:1, 0:1]
    bits = lax.shift_right_logical(lax.shift_right_logical(tok.astype(jnp.int32), 16), 16)
    return bits.astype(_F32)


def _rope_head(t, cos2, sin2):
    return t * cos2 + pltpu.roll(t, RET_HEAD_DIM // 2, 1) * sin2


def _meta_kernel(meta_ref, g_ref, b_ref, w32_ref, bblk_ref, ar_ref, ai_ref, cos_ref, sin_ref,
                 zmeta_ref, s5_ref, s0_ref, w_ref):
    slab = D_MODEL // 8
    for r in range(0, D_MODEL, slab):
        w_ref[r:r + slab, :] = w32_ref[r:r + slab, :].astype(_BF)
    hm = _layer_norm(meta_ref[...], g_ref[...], b_ref[...]).astype(_BF)
    u = _dot(hm, w_ref[:, 0:S5_WIDTH])
    bu = _dot(u.astype(_BF), bblk_ref[...])
    ar = ar_ref[...]
    ai = ai_ref[...]
    xr = jnp.zeros((1, S5_NSTATE), _F32)
    xi = jnp.zeros((1, S5_NSTATE), _F32)
    for t in range(N_META):
        br = bu[t:t + 1, 0:S5_NSTATE]
        bi = bu[t:t + 1, S5_NSTATE:2 * S5_NSTATE]
        xr, xi = ar * xr - ai * xi + br, ar * xi + ai * xr + bi
    s5_ref[:, 0:S5_NSTATE] = xr
    s5_ref[:, S5_NSTATE:2 * S5_NSTATE] = xi

    k = _dot(hm, w_ref[:, _OFF_K:_OFF_V])
    v = _dot(hm, w_ref[:, _OFF_V:_OFF_G]).astype(_BF)
    cos2 = cos_ref[...]
    sin2 = sin_ref[...]
    for h in range(RET_HEADS):
        sl = slice(h * RET_HEAD_DIM, (h + 1) * RET_HEAD_DIM)
        kh = _rope_head(k[:, sl], cos2, sin2) * (RET_HEAD_DIM ** -0.5)
        kz = (kh * zmeta_ref[h]).astype(_BF)
        s0_ref[h] = lax.dot_general(kz, v[:, sl], (((0,), (0,)), ((), ())),
                                    preferred_element_type=_F32)


def _in_proj_ret_kernel(x_ref, g_ref, b_ref, w_ref, cos_ref, sin_ref, s0_ref, dmat_ref, xi_ref,
                        zeta_ref, gc_ref, gng_ref, gnb_ref, wo32_ref, wup32_ref, wdn32_ref,
                        u_ref, y_ref, wo16_ref, wup16_ref, wdn16_ref,
                        q_scr, k_scr, v_scr, sg_scr, sloc_scr, s_scr):
    pair = pl.program_id(1)
    halves = TM_PROJ // SUB_PROJ
    heads = [(h, slice(h * RET_HEAD_DIM, (h + 1) * RET_HEAD_DIM)) for h in range(RET_HEADS)]

    wo16_ref[...] = wo32_ref[...].astype(_BF)
    wup16_ref[...] = wup32_ref[...].astype(_BF)
    wdn16_ref[...] = wdn32_ref[...].astype(_BF)

    @pl.when(pl.program_id(0) == 0)
    def _():
        for bb in range(BPS_PROJ):
            sloc_scr[bb] = s0_ref[...]

    @pl.when(pl.program_id(0) != 0)
    def _():
        for bb in range(BPS_PROJ):
            sloc_scr[bb] = s_scr[pair * BPS_PROJ + bb]

    def where(s):
        return s // halves, (s % halves) * SUB_PROJ

    def normed(s):
        bb, r0 = where(s)
        return _layer_norm(x_ref[bb, r0:r0 + SUB_PROJ, :], g_ref[...], b_ref[...]).astype(_BF)

    def project_stages(s, hn, box):
        bb, r0 = where(s)
        rows = slice(r0, r0 + SUB_PROJ)

        def st_u():
            u = _dot(hn, w_ref[:, 0:S5_WIDTH])
            start = pair * BPS_PROJ + bb + r0 * BATCH
            for j in range(S5_SLABS):
                u_ref[j, pl.ds(start, SUB_PROJ, stride=BATCH), :] = u[:, j * LANES:(j + 1) * LANES]

        def st_q():
            box['q'] = _dot(hn, w_ref[:, _OFF_Q:_OFF_K])

        def st_q_rope():
            if s + 1 < BPS_PROJ * halves:
                box['next'] = normed(s + 1)
            for h, sl in heads:
                q_scr[bb, rows, sl] = _rope_head(box['q'][:, sl], cos_ref[rows, :],
                                                 sin_ref[rows, :]).astype(_BF)

        def st_k():
            k = _dot(hn, w_ref[:, _OFF_K:_OFF_V])
            for h, sl in heads:
                k_scr[bb, rows, sl] = (_rope_head(k[:, sl], cos_ref[rows, :], sin_ref[rows, :])
                                       * (RET_HEAD_DIM ** -0.5)).astype(_BF)

        def st_g():
            g = _dot(hn, w_ref[:, _OFF_G:IN_PROJ_WIDTH])
            sg_scr[bb, rows, :] = (g * jax.nn.sigmoid(g)).astype(_BF)

        def st_v():
            v_scr[bb, rows, :] = _dot(hn, w_ref[:, _OFF_V:_OFF_G]).astype(_BF)

        return [st_u, st_q, st_q_rope, st_k, st_g, st_v]

    def retention_stages(s):
        bb, r0 = where(s)
        stages = []
        for c in range(SUB_PROJ // CHUNK):
            rows = slice(r0 + c * CHUNK, r0 + (c + 1) * CHUNK)
            box = {}

            def wave_dots(rows=rows, box=box):
                scores, cross = [], []
                for h, cols in heads:
                    qh = q_scr[bb, rows, cols]
                    kh = k_scr[bb, rows, cols]
                    state = sloc_scr[bb, h]
                    scores.append(lax.dot_general(qh, kh, (((1,), (1,)), ((), ())),
                                                  preferred_element_type=_F32))
                    cross.append(_dot(qh, state.astype(_BF)))
                    kz = (kh.astype(_F32) * zeta_ref[h]).astype(_BF)
                    sloc_scr[bb, h] = gc_ref[h] * state + lax.dot_general(
                        kz, v_scr[bb, rows, cols], (((0,), (0,)), ((), ())),
                        preferred_element_type=_F32)
                box['scores'], box['cross'] = scores, cross

            def wave_mix(rows=rows, box=box):
                box['outs'] = [
                    _dot((box['scores'][h] * dmat_ref[h]).astype(_BF), v_scr[bb, rows, cols])
                    + box['cross'][h] * xi_ref[h] for h, cols in heads]

            def wave_norm(rows=rows, box=box):
                for h, cols in heads:
                    o = box['outs'][h]
                    mu = jnp.mean(o, axis=-1, keepdims=True)
                    oc = o - mu
                    var = jnp.mean(oc * oc, axis=-1, keepdims=True)
                    on = oc * lax.rsqrt(var + GN_EPS) * gng_ref[:, cols] + gnb_ref[:, cols]
                    y_ref[bb, rows, cols] = (sg_scr[bb, rows, cols].astype(_F32)
                                             * on).astype(_BF)

            stages += [wave_dots, wave_mix, wave_norm]
        return stages

    n_sub = BPS_PROJ * halves
    hn = normed(0)
    for s in range(n_sub):
        box = {}
        proj = project_stages(s, hn, box)
        ret = retention_stages(s - 1) if s >= 1 else []
        for i in range(max(len(proj), len(ret))):
            if i < len(proj):
                proj[i]()
            if i < len(ret):
                ret[i]()
        hn = box.get('next')
    for wave in retention_stages(n_sub - 1):
        wave()

    for bb in range(BPS_PROJ):
        s_scr[pair * BPS_PROJ + bb] = sloc_scr[bb]


def _s5_kernel(u_ref, init_ref, bblk_ref, cblk_ref, ar_ref, ai_ref, d_ref, wglu_ref, bglu_ref,
               y_ref, bu0_scr, bu1_scr, xb0_scr, xb1_scr, acc0_scr, acc1_scr, st_scr):
    @pl.when(pl.program_id(0) == 0)
    def _():
        st_scr[...] = jnp.broadcast_to(init_ref[...], (BATCH, 2 * S5_NSTATE))

    rows_sb = SB_S5 * BATCH
    n_sb = TT_S5 // SB_S5
    n_piece = 2 * S5_NSTATE // MXU_TILE
    steps_piece = SB_S5 // n_piece
    re = slice(0, S5_NSTATE)
    im = slice(S5_NSTATE, 2 * S5_NSTATE)
    ar = jnp.broadcast_to(ar_ref[...], (BATCH, S5_NSTATE))
    ai = jnp.broadcast_to(ai_ref[...], (BATCH, S5_NSTATE))

    def rows_of(j):
        if isinstance(j, int):
            return pl.ds(j * rows_sb, rows_sb)
        return pl.ds(pl.multiple_of(j * rows_sb, rows_sb), rows_sb)

    def load_u(j):
        return jnp.concatenate([u_ref[s, rows_of(j), :] for s in range(S5_SLABS)], axis=1)

    bu_scr = (bu0_scr, bu1_scr)
    xb_scr = (xb0_scr, xb1_scr)
    acc_scr = (acc0_scr, acc1_scr)

    def a_piece(ub, slot, c):
        cols = slice(c * MXU_TILE, (c + 1) * MXU_TILE)
        bu_scr[slot][:, cols] = _dot(ub, bblk_ref[:, cols])

    def b_piece(slot, c, xr, xi):
        for t in range(c * steps_piece, (c + 1) * steps_piece, 2):
            out_r, out_i = [], []
            for tt in (t, t + 1):
                rows = slice(tt * BATCH, (tt + 1) * BATCH)
                br = bu_scr[slot][rows, re]
                bi = bu_scr[slot][rows, im]
                xr, xi = ar * xr - ai * xi + br, ar * xi + ai * xr + bi
                out_r.append(xr)
                out_i.append(xi)
            rows2 = slice(t * BATCH, (t + 2) * BATCH)
            xb_scr[slot][rows2, re] = jnp.concatenate(out_r, axis=0).astype(_BF)
            xb_scr[slot][rows2, im] = jnp.concatenate(out_i, axis=0).astype(_BF)
        return xr, xi

    def c_piece(slot, c, acc):
        cols = slice(c * MXU_TILE, (c + 1) * MXU_TILE)
        part = _dot(xb_scr[slot][:, cols], cblk_ref[cols, :])
        return part if acc is None else acc + part

    def d_piece(j, slot, half):
        lo = half * (rows_sb // 2)
        if isinstance(j, int):
            rows = pl.ds(j * rows_sb + lo, rows_sb // 2)
        else:
            rows = pl.ds(pl.multiple_of(j * rows_sb + lo, rows_sb // 2), rows_sb // 2)
        u = jnp.concatenate([u_ref[s, rows, :] for s in range(S5_SLABS)], axis=1)
        y = jax.nn.gelu(acc_scr[slot][lo:lo + rows_sb // 2, :] + d_ref[...] * u)
        gate = jax.nn.sigmoid(_dot(y.astype(_BF), wglu_ref[...]) + bglu_ref[...])
        y = y * gate
        for s in range(S5_SLABS):
            y_ref[s, rows, :] = y[:, s * LANES:(s + 1) * LANES]

    def iteration(j, slot, xr, xi, do_a, do_b, do_c, do_d):
        other = 1 - slot
        ub = load_u(j + 1).astype(_BF) if do_a else None
        acc = None
        for c in range(n_piece):
            if do_a:
                a_piece(ub, other, c)
            if do_b:
                xr, xi = b_piece(slot, c, xr, xi)
            if do_c:
                acc = c_piece(other, c, acc)
            if do_d and c % (n_piece // 2) == n_piece // 2 - 1:
                d_piece(j - 2, slot, c // (n_piece // 2))
        if do_c:
            acc_scr[other][...] = acc
        return xr, xi

    ub0 = load_u(0).astype(_BF)
    for c in range(n_piece):
        a_piece(ub0, 0, c)
    xr, xi = iteration(0, 0, st_scr[:, re], st_scr[:, im], True, True, False, False)
    xr, xi = iteration(1, 1, xr, xi, True, True, True, False)

    def pair(p, carry):
        j = 2 * p + 2
        xr, xi = iteration(j, 0, carry[0], carry[1], True, True, True, True)
        return iteration(j + 1, 1, xr, xi, True, True, True, True)

    xr, xi = lax.fori_loop(0, (n_sb - 4) // 2, pair, (xr, xi))
    xr, xi = iteration(n_sb - 2, 0, xr, xi, True, True, True, True)
    xr, xi = iteration(n_sb - 1, 1, xr, xi, False, True, True, True)
    st_scr[:, re] = xr
    st_scr[:, im] = xi
    iteration(n_sb, 0, xr, xi, False, False, True, True)
    iteration(n_sb + 1, 1, xr, xi, False, False, False, True)


def _ffn_kernel(x_ref, ys5_ref, yret_ref, lig_ref, lib_ref, wo_ref, l1g_ref, l1b_ref,
                wup_ref, wdn_ref, l2g_ref, l2b_ref, o_ref):
    pair = pl.program_id(1)
    n_ch = D_FF // FF_CHUNK
    piece = SUB_FFN // (2 * n_ch)
    halves = TM_FFN // SUB_FFN
    n_sub = BPS_FFN * halves

    def where(t):
        return t // halves, (t % halves) * SUB_FFN

    def mixed_of(t):
        bb, r0 = where(t)
        start = pair * BPS_FFN + bb + r0 * BATCH
        ys5 = jnp.concatenate(
            [ys5_ref[j, pl.ds(start, SUB_FFN, stride=BATCH), :] for j in range(S5_SLABS)],
            axis=1).astype(_BF)
        return (_dot(ys5, wo_ref[0:S5_WIDTH, :])
                + _dot(yret_ref[bb, r0:r0 + SUB_FFN, :], wo_ref[S5_WIDTH:D_MODEL, :]))

    def prep_rows(t, mixed, lo, hi):
        bb, r0 = where(t)
        h = _layer_norm(x_ref[bb, r0 + lo:r0 + hi, :], lig_ref[...], lib_ref[...])
        return _layer_norm(DEEPNORM_ALPHA * h + mixed[lo:hi], l1g_ref[...], l1b_ref[...])

    def ffn_up(h1b, c, floor):
        up = jnp.maximum(_dot(h1b, wup_ref[:, c * FF_CHUNK:(c + 1) * FF_CHUNK]), floor)
        return (up * up).astype(_BF)

    def ffn_down(act, c):
        return _dot(act, wdn_ref[c * FF_CHUNK:(c + 1) * FF_CHUNK, :])

    def finish_rows(t, pre, lo, hi):
        bb, r0 = where(t)
        out = _layer_norm(pre[lo:hi], l2g_ref[...], l2b_ref[...])
        o_ref[bb, r0 + lo:r0 + hi, :] = out
        return out

    h1 = prep_rows(0, mixed_of(0), 0, SUB_FFN)
    pre_prev = None
    floor = 0.0
    for t in range(n_sub):
        h1b = h1.astype(_BF)
        pre = DEEPNORM_ALPHA * h1
        mixed_next = mixed_of(t + 1) if t + 1 < n_sub else None
        next_parts = []

        def side_work(k):
            lo, hi = k * piece, (k + 1) * piece
            done = []
            if mixed_next is not None:
                next_parts.append(prep_rows(t + 1, mixed_next, lo, hi))
                done.append(next_parts[-1])
            if pre_prev is not None:
                done.append(finish_rows(t - 1, pre_prev, lo, hi))
            return done

        for c in range(n_ch):
            act = ffn_up(h1b, c, floor)
            done = side_work(2 * c)
            pre = pre + ffn_down(act, c)
            done += side_work(2 * c + 1)
            floor = _zero_after(done)
        if next_parts:
            h1 = jnp.concatenate(next_parts, axis=0)
        pre_prev = pre
    finish_rows(n_sub - 1, pre_prev, 0, SUB_FFN)


def _row(v):
    return v.reshape(1, -1).astype(_F32)


@functools.lru_cache(maxsize=None)
def _position_tables():
    f32 = np.float32

    def fn(f, a):
        return f(a.astype(np.float64)).astype(f32)

    pos = np.arange(N_META + SEQ, dtype=f32)
    expo = np.arange(0, RET_HEAD_DIM, 2, dtype=f32) / f32(RET_HEAD_DIM)
    inv_freq = (f32(1.0) / fn(lambda e: np.power(ROPE_BASE, e), expo)).astype(f32)
    ang = pos[:, None] * inv_freq[None, :]
    cos, sin = fn(np.cos, ang), fn(np.sin, ang)
    cos2 = np.concatenate([cos, cos], axis=1)
    sin2 = np.concatenate([-sin, sin], axis=1)

    heads = np.arange(RET_HEADS, dtype=f32)
    log_gamma = fn(np.log1p, -fn(np.exp2, f32(-5.0) - heads))
    idx = np.arange(CHUNK, dtype=f32)
    diff = idx[:, None] - idx[None, :]
    dmat = np.where(diff[None] >= 0,
                    fn(np.exp, np.maximum(diff, f32(0.0))[None] * log_gamma[:, None, None]),
                    f32(0.0)).astype(f32)
    zeta = fn(np.exp, (f32(CHUNK - 1.0) - idx)[None] * log_gamma[:, None])
    xi = fn(np.exp, (idx + f32(1.0))[None] * log_gamma[:, None])
    gamma_chunk = fn(np.exp, f32(CHUNK) * log_gamma)
    hd = RET_HEAD_DIM
    zeta_b = np.ascontiguousarray(np.broadcast_to(zeta[:, :, None], (RET_HEADS, CHUNK, hd)))
    xi_b = np.ascontiguousarray(np.broadcast_to(xi[:, :, None], (RET_HEADS, CHUNK, hd)))
    gc_b = np.ascontiguousarray(np.broadcast_to(gamma_chunk[:, None, None], (RET_HEADS, 1, hd)))
    return cos2, sin2, dmat, zeta_b, xi_b, gc_b


def kernel(x, meta_tokens, ln_in_g, ln_in_b, w_in, s5_lambda_re, s5_lambda_im, s5_log_dt, s5_b_re, s5_b_im, s5_c_re, s5_c_im, s5_d, s5_w_glu, s5_b_glu, ret_gn_g, ret_gn_b, w_out, ln1_g, ln1_b, w_up, w_down, ln2_g, ln2_b):
    assert x.shape == (BATCH, SEQ, D_MODEL) and w_in.shape[0] == 1
    G, P, H = S5_GROUPS, S5_STATE, S5_GROUP_CH
    arb2 = pltpu.CompilerParams(dimension_semantics=("arbitrary", "arbitrary"),
                                vmem_limit_bytes=VMEM_LIMIT)
    arb1 = pltpu.CompilerParams(dimension_semantics=("arbitrary",), vmem_limit_bytes=VMEM_LIMIT)

    lam_re, lam_im = s5_lambda_re[0], s5_lambda_im[0]
    dt = jnp.exp(s5_log_dt[0])[:, None]
    mag = jnp.exp(lam_re * dt)
    lbr = mag * jnp.cos(lam_im * dt)
    lbi = mag * jnp.sin(lam_im * dt)
    den = lam_re * lam_re + lam_im * lam_im
    nr = lbr - 1.0
    qr = (nr * lam_re + lbi * lam_im) / den
    qi = (lbi * lam_re - nr * lam_im) / den
    bbr = qr[..., None] * s5_b_re[0] - qi[..., None] * s5_b_im[0]
    bbi = qr[..., None] * s5_b_im[0] + qi[..., None] * s5_b_re[0]
    eye = jnp.eye(G, dtype=_F32)

    def blk_in(m):
        return (eye[:, None, :, None] * m.transpose(0, 2, 1)[:, :, None, :]).reshape(G * H, G * P)

    def blk_out(m):
        return (eye[:, None, :, None] * m.transpose(0, 2, 1)[:, :, None, :]).reshape(G * P, G * H)

    bblk = jnp.concatenate([blk_in(bbr), blk_in(bbi)], axis=1).astype(_BF)
    cblk = jnp.concatenate([blk_out(s5_c_re[0]), -blk_out(s5_c_im[0])], axis=0).astype(_BF)
    ar = lbr.reshape(1, S5_NSTATE)
    ai = lbi.reshape(1, S5_NSTATE)

    hd = RET_HEAD_DIM
    cos2, sin2, dmat, zeta_b, xi_b, gc_b = _position_tables()
    zmeta_b = zeta_b[:, CHUNK - N_META:, :]

    lig, lib = _row(ln_in_g), _row(ln_in_b)

    s5_init, s0, w_in_b = pl.pallas_call(
        _meta_kernel,
        out_shape=(jax.ShapeDtypeStruct((1, 2 * S5_NSTATE), _F32),
                   jax.ShapeDtypeStruct((RET_HEADS, hd, hd), _F32),
                   jax.ShapeDtypeStruct((D_MODEL, IN_PROJ_WIDTH), _BF)),
        compiler_params=pltpu.CompilerParams(vmem_limit_bytes=VMEM_LIMIT),
        name="meta_prologue",
    )(meta_tokens.astype(_F32), lig, lib, w_in[0], bblk, ar, ai,
      cos2[:N_META], sin2[:N_META], zmeta_b)

    n_t = SEQ // TM_PROJ
    rows_spec = lambda w: pl.BlockSpec((BPS_PROJ, TM_PROJ, w), lambda i, p: (p, i, 0))
    n_pairs = BATCH // BPS_PROJ
    n_steps = n_t * n_pairs

    def slab_spec(shape):
        return pl.BlockSpec((shape[0] // n_steps, shape[1]), lambda i, p: (i * n_pairs + p, 0))

    tab = lambda n: _const_spec((RET_HEADS, n, hd))
    w_shapes = [(D_MODEL, D_MODEL), (D_MODEL, D_FF), (D_FF, D_MODEL)]
    qkv_scratch = pltpu.VMEM((BPS_PROJ, TM_PROJ, RET_WIDTH), _BF)
    u_tm, y_ret, w_out_b, w_up_b, w_dn_b = pl.pallas_call(
        _in_proj_ret_kernel,
        grid=(n_t, n_pairs),
        in_specs=[rows_spec(D_MODEL), _const_spec((1, D_MODEL)), _const_spec((1, D_MODEL)),
                  _const_spec((D_MODEL, IN_PROJ_WIDTH)),
                  pl.BlockSpec((TM_PROJ, hd), lambda i, p: (i, 0)),
                  pl.BlockSpec((TM_PROJ, hd), lambda i, p: (i, 0)),
                  tab(hd), tab(CHUNK), tab(CHUNK), tab(CHUNK), tab(1),
                  _const_spec((1, RET_WIDTH)), _const_spec((1, RET_WIDTH))]
                 + [slab_spec(s) for s in w_shapes],
        out_specs=[pl.BlockSpec((S5_SLABS, TM_PROJ * BATCH, LANES), lambda i, p: (0, i, 0)),
                   rows_spec(RET_WIDTH)] + [slab_spec(s) for s in w_shapes],
        out_shape=(jax.ShapeDtypeStruct((S5_SLABS, SEQ * BATCH, LANES), _F32),
                   jax.ShapeDtypeStruct((BATCH, SEQ, RET_WIDTH), _BF))
                  + tuple(jax.ShapeDtypeStruct(s, _BF) for s in w_shapes),
        scratch_shapes=[qkv_scratch, qkv_scratch, qkv_scratch, qkv_scratch,
                        pltpu.VMEM((BPS_PROJ, RET_HEADS, hd, hd), _F32),
                        pltpu.VMEM((BATCH, RET_HEADS, hd, hd), _F32)],
        compiler_params=arb2,
        name="in_proj_ret",
    )(x, lig, lib, w_in_b, cos2[N_META:], sin2[N_META:], s0, dmat, xi_b, zeta_b, gc_b,
      _row(ret_gn_g[0]), _row(ret_gn_b[0]), w_out[0], w_up[0], w_down[0])

    rows_s5 = TT_S5 * BATCH
    ys5_tm = pl.pallas_call(
        _s5_kernel,
        grid=(SEQ // TT_S5,),
        in_specs=[pl.BlockSpec((S5_SLABS, rows_s5, LANES), lambda i: (0, i, 0)),
                  _const_spec((1, 2 * S5_NSTATE)),
                  _const_spec((S5_WIDTH, 2 * S5_NSTATE)), _const_spec((2 * S5_NSTATE, S5_WIDTH)),
                  _const_spec((1, S5_NSTATE)), _const_spec((1, S5_NSTATE)),
                  _const_spec((1, S5_WIDTH)), _const_spec((S5_WIDTH, S5_WIDTH)),
                  _const_spec((1, S5_WIDTH))],
        out_specs=pl.BlockSpec((S5_SLABS, rows_s5, LANES), lambda i: (0, i, 0)),
        out_shape=jax.ShapeDtypeStruct((S5_SLABS, SEQ * BATCH, LANES), _F32),
        scratch_shapes=[pltpu.VMEM((SB_S5 * BATCH, 2 * S5_NSTATE), _F32),
                        pltpu.VMEM((SB_S5 * BATCH, 2 * S5_NSTATE), _F32),
                        pltpu.VMEM((SB_S5 * BATCH, 2 * S5_NSTATE), _BF),
                        pltpu.VMEM((SB_S5 * BATCH, 2 * S5_NSTATE), _BF),
                        pltpu.VMEM((SB_S5 * BATCH, S5_WIDTH), _F32),
                        pltpu.VMEM((SB_S5 * BATCH, S5_WIDTH), _F32),
                        pltpu.VMEM((BATCH, 2 * S5_NSTATE), _F32)],
        compiler_params=arb1,
        name="s5_scan",
    )(u_tm, s5_init, bblk, cblk, ar, ai, _row(s5_d[0]), s5_w_glu[0].astype(_BF), _row(s5_b_glu[0]))

    n_f = SEQ // TM_FFN
    frow = lambda w: pl.BlockSpec((BPS_FFN, TM_FFN, w), lambda i, p: (p, i, 0))
    out = pl.pallas_call(
        _ffn_kernel,
        grid=(n_f, BATCH // BPS_FFN),
        in_specs=[frow(D_MODEL),
                  pl.BlockSpec((S5_SLABS, TM_FFN * BATCH, LANES), lambda i, p: (0, i, 0)),
                  frow(RET_WIDTH),
                  _const_spec((1, D_MODEL)), _const_spec((1, D_MODEL)),
                  _const_spec((D_MODEL, D_MODEL)),
                  _const_spec((1, D_MODEL)), _const_spec((1, D_MODEL)),
                  _const_spec((D_MODEL, D_FF)), _const_spec((D_FF, D_MODEL)),
                  _const_spec((1, D_MODEL)), _const_spec((1, D_MODEL))],
        out_specs=frow(D_MODEL),
        out_shape=jax.ShapeDtypeStruct((BATCH, SEQ, D_MODEL), _F32),
        compiler_params=arb2,
        name="out_ffn",
    )(x, ys5_tm, y_ret, lig, lib, w_out_b,
      _row(ln1_g[0]), _row(ln1_b[0]), w_up_b, w_dn_b, _row(ln2_g[0]), _row(ln2_b[0]))

    return out
```

```python
import functools

import jax
import jax.numpy as jnp
import numpy as np
from jax import lax
from jax.experimental import pallas as pl
from jax.experimental.pallas import tpu as pltpu

D_MODEL = 1024
BATCH = 8
SEQ = 4096
N_META = 16
S5_GROUP_CH = 16
S5_STATE = 64
S5_WIDTH = 256
S5_GROUPS = S5_WIDTH // S5_GROUP_CH
S5_NSTATE = S5_GROUPS * S5_STATE
RET_HEAD_DIM = 128
RET_WIDTH = 768
RET_HEADS = RET_WIDTH // RET_HEAD_DIM
CHUNK = 128
ROPE_BASE = 10000.0
D_FF = 4 * D_MODEL
LANES = 128
S5_SLABS = S5_WIDTH // LANES
LN_EPS = 1e-5
GN_EPS = 1e-5
IN_PROJ_WIDTH = S5_WIDTH + 4 * RET_WIDTH
DEEPNORM_ALPHA = 2.0 ** 0.25

_OFF_Q = S5_WIDTH
_OFF_K = _OFF_Q + RET_WIDTH
_OFF_V = _OFF_K + RET_WIDTH
_OFF_G = _OFF_V + RET_WIDTH

TM_PROJ = 512
BPS_PROJ = 2
SUB_PROJ = 256
MXU_TILE = 256
TT_S5 = 1024
SB_S5 = 32
TM_FFN = 512
BPS_FFN = 2
SUB_FFN = 256
FF_CHUNK = 1024
VMEM_LIMIT = 56 * 1024 * 1024

_BF = jnp.bfloat16
_F32 = jnp.float32


def _const_spec(shape):
    nd = len(shape)
    return pl.BlockSpec(shape, lambda *_: (0,) * nd, pipeline_mode=pl.Buffered(1))


def _layer_norm(x, g, b):
    mu = jnp.mean(x, axis=-1, keepdims=True)
    xc = x - mu
    var = jnp.mean(xc * xc, axis=-1, keepdims=True)
    return xc * lax.rsqrt(var + LN_EPS) * g + b


def _dot(a, b):
    return jnp.dot(a, b, preferred_element_type=_F32)


def _zero_after(values):
    if not values:
        return 0.0
    tok = values[0][0:1, 0:1]
    for v in values[1:]:
        tok = tok + v[0:1, 0:1]
    bits = lax.shift_right_logical(lax.shift_right_logical(tok.astype(jnp.int32), 16), 16)
    return bits.astype(_F32)


def _rope_head(t, cos2, sin2):
    return t * cos2 + pltpu.roll(t, RET_HEAD_DIM // 2, 1) * sin2


def _meta_kernel(meta_ref, g_ref, b_ref, w32_ref, bblk_ref, ar_ref, ai_ref, cos_ref, sin_ref,
                 zmeta_ref, s5_ref, s0_ref, w_ref):
    slab = D_MODEL // 8
    for r in range(0, D_MODEL, slab):
        w_ref[r:r + slab, :] = w32_ref[r:r + slab, :].astype(_BF)
    hm = _layer_norm(meta_ref[...], g_ref[...], b_ref[...]).astype(_BF)
    u = _dot(hm, w_ref[:, 0:S5_WIDTH])
    bu = _dot(u.astype(_BF), bblk_ref[...])
    ar = ar_ref[...]
    ai = ai_ref[...]
    xr = jnp.zeros((1, S5_NSTATE), _F32)
    xi = jnp.zeros((1, S5_NSTATE), _F32)
    for t in range(N_META):
        br = bu[t:t + 1, 0:S5_NSTATE]
        bi = bu[t:t + 1, S5_NSTATE:2 * S5_NSTATE]
        xr, xi = ar * xr - ai * xi + br, ar * xi + ai * xr + bi
    s5_ref[:, 0:S5_NSTATE] = xr
    s5_ref[:, S5_NSTATE:2 * S5_NSTATE] = xi

    k = _dot(hm, w_ref[:, _OFF_K:_OFF_V])
    v = _dot(hm, w_ref[:, _OFF_V:_OFF_G]).astype(_BF)
    cos2 = cos_ref[...]
    sin2 = sin_ref[...]
    for h in range(RET_HEADS):
        sl = slice(h * RET_HEAD_DIM, (h + 1) * RET_HEAD_DIM)
        kh = _rope_head(k[:, sl], cos2, sin2) * (RET_HEAD_DIM ** -0.5)
        kz = (kh * zmeta_ref[h]).astype(_BF)
        s0_ref[h] = lax.dot_general(kz, v[:, sl], (((0,), (0,)), ((), ())),
                                    preferred_element_type=_F32)


def _in_proj_ret_kernel(x_ref, g_ref, b_ref, w_ref, cos_ref, sin_ref, s0_ref, dmat_ref, xi_ref,
                        zeta_ref, gc_ref, gng_ref, gnb_ref, wo32_ref, wup32_ref, wdn32_ref,
                        u_ref, y_ref, wo16_ref, wup16_ref, wdn16_ref,
                        q_scr, k_scr, v_scr, sg_scr, sloc_scr, s_scr):
    pair = pl.program_id(1)
    halves = TM_PROJ // SUB_PROJ
    heads = [(h, slice(h * RET_HEAD_DIM, (h + 1) * RET_HEAD_DIM)) for h in range(RET_HEADS)]

    wo16_ref[...] = wo32_ref[...].astype(_BF)
    wup16_ref[...] = wup32_ref[...].astype(_BF)
    wdn16_ref[...] = wdn32_ref[...].astype(_BF)

    @pl.when(pl.program_id(0) == 0)
    def _():
        for bb in range(BPS_PROJ):
            sloc_scr[bb] = s0_ref[...]

    @pl.when(pl.program_id(0) != 0)
    def _():
        for bb in range(BPS_PROJ):
            sloc_scr[bb] = s_scr[pair * BPS_PROJ + bb]

    def where(s):
        return s // halves, (s % halves) * SUB_PROJ

    def normed(s):
        bb, r0 = where(s)
        return _layer_norm(x_ref[bb, r0:r0 + SUB_PROJ, :], g_ref[...], b_ref[...]).astype(_BF)

    def project_stages(s, hn, box):
        bb, r0 = where(s)
        rows = slice(r0, r0 + SUB_PROJ)

        def st_u():
            u = _dot(hn, w_ref[:, 0:S5_WIDTH])
            start = pair * BPS_PROJ + bb + r0 * BATCH
            for j in range(S5_SLABS):
                u_ref[j, pl.ds(start, SUB_PROJ, stride=BATCH), :] = u[:, j * LANES:(j + 1) * LANES]

        def st_q():
            box['q'] = _dot(hn, w_ref[:, _OFF_Q:_OFF_K])

        def st_q_rope():
            if s + 1 < BPS_PROJ * halves:
                box['next'] = normed(s + 1)
            for h, sl in heads:
                q_scr[bb, rows, sl] = _rope_head(box['q'][:, sl], cos_ref[rows, :],
                                                 sin_ref[rows, :]).astype(_BF)

        def st_k():
            k = _dot(hn, w_ref[:, _OFF_K:_OFF_V])
            for h, sl in heads:
                k_scr[bb, rows, sl] = (_rope_head(k[:, sl], cos_ref[rows, :], sin_ref[rows, :])
                                       * (RET_HEAD_DIM ** -0.5)).astype(_BF)

        def st_g():
            g = _dot(hn, w_ref[:, _OFF_G:IN_PROJ_WIDTH])
            sg_scr[bb, rows, :] = (g * jax.nn.sigmoid(g)).astype(_BF)

        def st_v():
            v_scr[bb, rows, :] = _dot(hn, w_ref[:, _OFF_V:_OFF_G]).astype(_BF)

        return [st_u, st_q, st_q_rope, st_k, st_g, st_v]

    def retention_stages(s):
        bb, r0 = where(s)
        stages = []
        for c in range(SUB_PROJ // CHUNK):
            rows = slice(r0 + c * CHUNK, r0 + (c + 1) * CHUNK)
            box = {}

            def wave_dots(rows=rows, box=box):
                scores, cross = [], []
                for h, cols in heads:
                    qh = q_scr[bb, rows, cols]
                    kh = k_scr[bb, rows, cols]
                    state = sloc_scr[bb, h]
                    scores.append(lax.dot_general(qh, kh, (((1,), (1,)), ((), ())),
                                                  preferred_element_type=_F32))
                    cross.append(_dot(qh, state.astype(_BF)))
                    kz = (kh.astype(_F32) * zeta_ref[h]).astype(_BF)
                    sloc_scr[bb, h] = gc_ref[h] * state + lax.dot_general(
                        kz, v_scr[bb, rows, cols], (((0,), (0,)), ((), ())),
                        preferred_element_type=_F32)
                box['scores'], box['cross'] = scores, cross

            def wave_mix(rows=rows, box=box):
                box['outs'] = [
                    _dot((box['scores'][h] * dmat_ref[h]).astype(_BF), v_scr[bb, rows, cols])
                    + box['cross'][h] * xi_ref[h] for h, cols in heads]

            def wave_norm(rows=rows, box=box):
                for h, cols in heads:
                    o = box['outs'][h]
                    mu = jnp.mean(o, axis=-1, keepdims=True)
                    oc = o - mu
                    var = jnp.mean(oc * oc, axis=-1, keepdims=True)
                    on = oc * lax.rsqrt(var + GN_EPS) * gng_ref[:, cols] + gnb_ref[:, cols]
                    y_ref[bb, rows, cols] = (sg_scr[bb, rows, cols].astype(_F32)
                                             * on).astype(_BF)

            stages += [wave_dots, wave_mix, wave_norm]
        return stages

    n_sub = BPS_PROJ * halves
    hn = normed(0)
    for s in range(n_sub):
        box = {}
        proj = project_stages(s, hn, box)
        ret = retention_stages(s - 1) if s >= 1 else []
        for i in range(max(len(proj), len(ret))):
            if i < len(proj):
                proj[i]()
            if i < len(ret):
                ret[i]()
        hn = box.get('next')
    for wave in retention_stages(n_sub - 1):
        wave()

    for bb in range(BPS_PROJ):
        s_scr[pair * BPS_PROJ + bb] = sloc_scr[bb]


def _s5_kernel(u_ref, init_ref, bblk_ref, cblk_ref, ar_ref, ai_ref, d_ref, wglu_ref, bglu_ref,
               y_ref, bu0_scr, bu1_scr, xb0_scr, xb1_scr, acc0_scr, acc1_scr, st_scr):
    @pl.when(pl.program_id(0) == 0)
    def _():
        st_scr[...] = jnp.broadcast_to(init_ref[...], (BATCH, 2 * S5_NSTATE))

    rows_sb = SB_S5 * BATCH
    n_sb = TT_S5 // SB_S5
    n_piece = 2 * S5_NSTATE // MXU_TILE
    steps_piece = SB_S5 // n_piece
    re = slice(0, S5_NSTATE)
    im = slice(S5_NSTATE, 2 * S5_NSTATE)
    ar = jnp.broadcast_to(ar_ref[...], (BATCH, S5_NSTATE))
    ai = jnp.broadcast_to(ai_ref[...], (BATCH, S5_NSTATE))

    def rows_of(j):
        if isinstance(j, int):
            return pl.ds(j * rows_sb, rows_sb)
        return pl.ds(pl.multiple_of(j * rows_sb, rows_sb), rows_sb)

    def load_u(j):
        return jnp.concatenate([u_ref[s, rows_of(j), :] for s in range(S5_SLABS)], axis=1)

    bu_scr = (bu0_scr, bu1_scr)
    xb_scr = (xb0_scr, xb1_scr)
    acc_scr = (acc0_scr, acc1_scr)

    def a_piece(ub, slot, c):
        cols = slice(c * MXU_TILE, (c + 1) * MXU_TILE)
        bu_scr[slot][:, cols] = _dot(ub, bblk_ref[:, cols])

    def b_piece(slot, c, xr, xi):
        for t in range(c * steps_piece, (c + 1) * steps_piece, 2):
            out_r, out_i = [], []
            for tt in (t, t + 1):
                rows = slice(tt * BATCH, (tt + 1) * BATCH)
                br = bu_scr[slot][rows, re]
                bi = bu_scr[slot][rows, im]
                xr, xi = ar * xr - ai * xi + br, ar * xi + ai * xr + bi
                out_r.append(xr)
                out_i.append(xi)
            rows2 = slice(t * BATCH, (t + 2) * BATCH)
            xb_scr[slot][rows2, re] = jnp.concatenate(out_r, axis=0).astype(_BF)
            xb_scr[slot][rows2, im] = jnp.concatenate(out_i, axis=0).astype(_BF)
        return xr, xi

    def c_piece(slot, c, acc):
        cols = slice(c * MXU_TILE, (c + 1) * MXU_TILE)
        part = _dot(xb_scr[slot][:, cols], cblk_ref[cols, :])
        return part if acc is None else acc + part

    def d_piece(j, slot, half):
        lo = half * (rows_sb // 2)
        if isinstance(j, int):
            rows = pl.ds(j * rows_sb + lo, rows_sb // 2)
        else:
            rows = pl.ds(pl.multiple_of(j * rows_sb + lo, rows_sb // 2), rows_sb // 2)
        u = jnp.concatenate([u_ref[s, rows, :] for s in range(S5_SLABS)], axis=1)
        y = jax.nn.gelu(acc_scr[slot][lo:lo + rows_sb // 2, :] + d_ref[...] * u)
        gate = jax.nn.sigmoid(_dot(y.astype(_BF), wglu_ref[...]) + bglu_ref[...])
        y = y * gate
        for s in range(S5_SLABS):
            y_ref[s, rows, :] = y[:, s * LANES:(s + 1) * LANES]

    def iteration(j, slot, xr, xi, do_a, do_b, do_c, do_d):
        other = 1 - slot
        ub = load_u(j + 1).astype(_BF) if do_a else None
        acc = None
        for c in range(n_piece):
            if do_a:
                a_piece(ub, other, c)
            if do_b:
                xr, xi = b_piece(slot, c, xr, xi)
            if do_c:
                acc = c_piece(other, c, acc)
            if do_d and c >= n_piece - 2:
                d_piece(j - 2, slot, c - (n_piece - 2))
        if do_c:
            acc_scr[other][...] = acc
        return xr, xi

    ub0 = load_u(0).astype(_BF)
    for c in range(n_piece):
        a_piece(ub0, 0, c)
    xr, xi = iteration(0, 0, st_scr[:, re], st_scr[:, im], True, True, False, False)
    xr, xi = iteration(1, 1, xr, xi, True, True, True, False)

    def pair(p, carry):
        j = 2 * p + 2
        xr, xi = iteration(j, 0, carry[0], carry[1], True, True, True, True)
        return iteration(j + 1, 1, xr, xi, True, True, True, True)

    xr, xi = lax.fori_loop(0, (n_sb - 4) // 2, pair, (xr, xi))
    xr, xi = iteration(n_sb - 2, 0, xr, xi, True, True, True, True)
    xr, xi = iteration(n_sb - 1, 1, xr, xi, False, True, True, True)
    st_scr[:, re] = xr
    st_scr[:, im] = xi
    iteration(n_sb, 0, xr, xi, False, False, True, True)
    iteration(n_sb + 1, 1, xr, xi, False, False, False, True)


def _ffn_kernel(x_ref, ys5_ref, yret_ref, lig_ref, lib_ref, wo_ref, l1g_ref, l1b_ref,
                wup_ref, wdn_ref, l2g_ref, l2b_ref, o_ref):
    pair = pl.program_id(1)
    n_ch = D_FF // FF_CHUNK
    piece = SUB_FFN // (2 * n_ch)
    halves = TM_FFN // SUB_FFN
    n_sub = BPS_FFN * halves

    def where(t):
        return t // halves, (t % halves) * SUB_FFN

    def mixed_of(t):
        bb, r0 = where(t)
        start = pair * BPS_FFN + bb + r0 * BATCH
        ys5 = jnp.concatenate(
            [ys5_ref[j, pl.ds(start, SUB_FFN, stride=BATCH), :] for j in range(S5_SLABS)],
            axis=1).astype(_BF)
        return (_dot(ys5, wo_ref[0:S5_WIDTH, :])
                + _dot(yret_ref[bb, r0:r0 + SUB_FFN, :], wo_ref[S5_WIDTH:D_MODEL, :]))

    def prep_rows(t, mixed, lo, hi):
        bb, r0 = where(t)
        h = _layer_norm(x_ref[bb, r0 + lo:r0 + hi, :], lig_ref[...], lib_ref[...])
        return _layer_norm(DEEPNORM_ALPHA * h + mixed[lo:hi], l1g_ref[...], l1b_ref[...])

    def ffn_up(h1b, c, floor):
        up = jnp.maximum(_dot(h1b, wup_ref[:, c * FF_CHUNK:(c + 1) * FF_CHUNK]), floor)
        return (up * up).astype(_BF)

    def ffn_down(act, c):
        return _dot(act, wdn_ref[c * FF_CHUNK:(c + 1) * FF_CHUNK, :])

    def finish_rows(t, pre, lo, hi):
        bb, r0 = where(t)
        out = _layer_norm(pre[lo:hi], l2g_ref[...], l2b_ref[...])
        o_ref[bb, r0 + lo:r0 + hi, :] = out
        return out

    h1 = prep_rows(0, mixed_of(0), 0, SUB_FFN)
    pre_prev = None
    floor = 0.0
    for t in range(n_sub):
        h1b = h1.astype(_BF)
        pre = DEEPNORM_ALPHA * h1
        mixed_next = mixed_of(t + 1) if t + 1 < n_sub else None
        next_parts = []

        def side_work(k):
            lo, hi = k * piece, (k + 1) * piece
            done = []
            if mixed_next is not None:
                next_parts.append(prep_rows(t + 1, mixed_next, lo, hi))
                done.append(next_parts[-1])
            if pre_prev is not None:
                done.append(finish_rows(t - 1, pre_prev, lo, hi))
            return done

        for c in range(n_ch):
            act = ffn_up(h1b, c, floor)
            done = side_work(2 * c)
            pre = pre + ffn_down(act, c)
            done += side_work(2 * c + 1)
            floor = _zero_after(done)
        if next_parts:
            h1 = jnp.concatenate(next_parts, axis=0)
        pre_prev = pre
    finish_rows(n_sub - 1, pre_prev, 0, SUB_FFN)


def _row(v):
    return v.reshape(1, -1).astype(_F32)


@functools.lru_cache(maxsize=None)
def _position_tables():
    f32 = np.float32

    def fn(f, a):
        return f(a.astype(np.float64)).astype(f32)

    pos = np.arange(N_META + SEQ, dtype=f32)
    expo = np.arange(0, RET_HEAD_DIM, 2, dtype=f32) / f32(RET_HEAD_DIM)
    inv_freq = (f32(1.0) / fn(lambda e: np.power(ROPE_BASE, e), expo)).astype(f32)
    ang = pos[:, None] * inv_freq[None, :]
    cos, sin = fn(np.cos, ang), fn(np.sin, ang)
    cos2 = np.concatenate([cos, cos], axis=1)
    sin2 = np.concatenate([-sin, sin], axis=1)

    heads = np.arange(RET_HEADS, dtype=f32)
    log_gamma = fn(np.log1p, -fn(np.exp2, f32(-5.0) - heads))
    idx = np.arange(CHUNK, dtype=f32)
    diff = idx[:, None] - idx[None, :]
    dmat = np.where(diff[None] >= 0,
                    fn(np.exp, np.maximum(diff, f32(0.0))[None] * log_gamma[:, None, None]),
                    f32(0.0)).astype(f32)
    zeta = fn(np.exp, (f32(CHUNK - 1.0) - idx)[None] * log_gamma[:, None])
    xi = fn(np.exp, (idx + f32(1.0))[None] * log_gamma[:, None])
    gamma_chunk = fn(np.exp, f32(CHUNK) * log_gamma)
    hd = RET_HEAD_DIM
    zeta_b = np.ascontiguousarray(np.broadcast_to(zeta[:, :, None], (RET_HEADS, CHUNK, hd)))
    xi_b = np.ascontiguousarray(np.broadcast_to(xi[:, :, None], (RET_HEADS, CHUNK, hd)))
    gc_b = np.ascontiguousarray(np.broadcast_to(gamma_chunk[:, None, None], (RET_HEADS, 1, hd)))
    return cos2, sin2, dmat, zeta_b, xi_b, gc_b


def kernel(x, meta_tokens, ln_in_g, ln_in_b, w_in, s5_lambda_re, s5_lambda_im, s5_log_dt, s5_b_re, s5_b_im, s5_c_re, s5_c_im, s5_d, s5_w_glu, s5_b_glu, ret_gn_g, ret_gn_b, w_out, ln1_g, ln1_b, w_up, w_down, ln2_g, ln2_b):
    assert x.shape == (BATCH, SEQ, D_MODEL) and w_in.shape[0] == 1
    G, P, H = S5_GROUPS, S5_STATE, S5_GROUP_CH
    arb2 = pltpu.CompilerParams(dimension_semantics=("arbitrary", "arbitrary"),
                                vmem_limit_bytes=VMEM_LIMIT)
    arb1 = pltpu.CompilerParams(dimension_semantics=("arbitrary",), vmem_limit_bytes=VMEM_LIMIT)

    lam_re, lam_im = s5_lambda_re[0], s5_lambda_im[0]
    dt = jnp.exp(s5_log_dt[0])[:, None]
    mag = jnp.exp(lam_re * dt)
    lbr = mag * jnp.cos(lam_im * dt)
    lbi = mag * jnp.sin(lam_im * dt)
    den = lam_re * lam_re + lam_im * lam_im
    nr = lbr - 1.0
    qr = (nr * lam_re + lbi * lam_im) / den
    qi = (lbi * lam_re - nr * lam_im) / den
    bbr = qr[..., None] * s5_b_re[0] - qi[..., None] * s5_b_im[0]
    bbi = qr[..., None] * s5_b_im[0] + qi[..., None] * s5_b_re[0]
    eye = jnp.eye(G, dtype=_F32)

    def blk_in(m):
        return (eye[:, None, :, None] * m.transpose(0, 2, 1)[:, :, None, :]).reshape(G * H, G * P)

    def blk_out(m):
        return (eye[:, None, :, None] * m.transpose(0, 2, 1)[:, :, None, :]).reshape(G * P, G * H)

    bblk = jnp.concatenate([blk_in(bbr), blk_in(bbi)], axis=1).astype(_BF)
    cblk = jnp.concatenate([blk_out(s5_c_re[0]), -blk_out(s5_c_im[0])], axis=0).astype(_BF)
    ar = lbr.reshape(1, S5_NSTATE)
    ai = lbi.reshape(1, S5_NSTATE)

    hd = RET_HEAD_DIM
    cos2, sin2, dmat, zeta_b, xi_b, gc_b = _position_tables()
    zmeta_b = zeta_b[:, CHUNK - N_META:, :]

    lig, lib = _row(ln_in_g), _row(ln_in_b)

    s5_init, s0, w_in_b = pl.pallas_call(
        _meta_kernel,
        out_shape=(jax.ShapeDtypeStruct((1, 2 * S5_NSTATE), _F32),
                   jax.ShapeDtypeStruct((RET_HEADS, hd, hd), _F32),
                   jax.ShapeDtypeStruct((D_MODEL, IN_PROJ_WIDTH), _BF)),
        compiler_params=pltpu.CompilerParams(vmem_limit_bytes=VMEM_LIMIT),
        name="meta_prologue",
    )(meta_tokens.astype(_F32), lig, lib, w_in[0], bblk, ar, ai,
      cos2[:N_META], sin2[:N_META], zmeta_b)

    n_t = SEQ // TM_PROJ
    rows_spec = lambda w: pl.BlockSpec((BPS_PROJ, TM_PROJ, w), lambda i, p: (p, i, 0))
    n_pairs = BATCH // BPS_PROJ
    n_steps = n_t * n_pairs

    def slab_spec(shape):
        return pl.BlockSpec((shape[0] // n_steps, shape[1]), lambda i, p: (i * n_pairs + p, 0))

    tab = lambda n: _const_spec((RET_HEADS, n, hd))
    w_shapes = [(D_MODEL, D_MODEL), (D_MODEL, D_FF), (D_FF, D_MODEL)]
    qkv_scratch = pltpu.VMEM((BPS_PROJ, TM_PROJ, RET_WIDTH), _BF)
    u_tm, y_ret, w_out_b, w_up_b, w_dn_b = pl.pallas_call(
        _in_proj_ret_kernel,
        grid=(n_t, n_pairs),
        in_specs=[rows_spec(D_MODEL), _const_spec((1, D_MODEL)), _const_spec((1, D_MODEL)),
                  _const_spec((D_MODEL, IN_PROJ_WIDTH)),
                  pl.BlockSpec((TM_PROJ, hd), lambda i, p: (i, 0)),
                  pl.BlockSpec((TM_PROJ, hd), lambda i, p: (i, 0)),
                  tab(hd), tab(CHUNK), tab(CHUNK), tab(CHUNK), tab(1),
                  _const_spec((1, RET_WIDTH)), _const_spec((1, RET_WIDTH))]
                 + [slab_spec(s) for s in w_shapes],
        out_specs=[pl.BlockSpec((S5_SLABS, TM_PROJ * BATCH, LANES), lambda i, p: (0, i, 0)),
                   rows_spec(RET_WIDTH)] + [slab_spec(s) for s in w_shapes],
        out_shape=(jax.ShapeDtypeStruct((S5_SLABS, SEQ * BATCH, LANES), _F32),
                   jax.ShapeDtypeStruct((BATCH, SEQ, RET_WIDTH), _BF))
                  + tuple(jax.ShapeDtypeStruct(s, _BF) for s in w_shapes),
        scratch_shapes=[qkv_scratch, qkv_scratch, qkv_scratch, qkv_scratch,
                        pltpu.VMEM((BPS_PROJ, RET_HEADS, hd, hd), _F32),
                        pltpu.VMEM((BATCH, RET_HEADS, hd, hd), _F32)],
        compiler_params=arb2,
        name="in_proj_ret",
    )(x, lig, lib, w_in_b, cos2[N_META:], sin2[N_META:], s0, dmat, xi_b, zeta_b, gc_b,
      _row(ret_gn_g[0]), _row(ret_gn_b[0]), w_out[0], w_up[0], w_down[0])

    rows_s5 = TT_S5 * BATCH
    ys5_tm = pl.pallas_call(
        _s5_kernel,
        grid=(SEQ // TT_S5,),
        in_specs=[pl.BlockSpec((S5_SLABS, rows_s5, LANES), lambda i: (0, i, 0)),
                  _const_spec((1, 2 * S5_NSTATE)),
                  _const_spec((S5_WIDTH, 2 * S5_NSTATE)), _const_spec((2 * S5_NSTATE, S5_WIDTH)),
                  _const_spec((1, S5_NSTATE)), _const_spec((1, S5_NSTATE)),
                  _const_spec((1, S5_WIDTH)), _const_spec((S5_WIDTH, S5_WIDTH)),
                  _const_spec((1, S5_WIDTH))],
        out_specs=pl.BlockSpec((S5_SLABS, rows_s5, LANES), lambda i: (0, i, 0)),
        out_shape=jax.ShapeDtypeStruct((S5_SLABS, SEQ * BATCH, LANES), _F32),
        scratch_shapes=[pltpu.VMEM((SB_S5 * BATCH, 2 * S5_NSTATE), _F32),
                        pltpu.VMEM((SB_S5 * BATCH, 2 * S5_NSTATE), _F32),
                        pltpu.VMEM((SB_S5 * BATCH, 2 * S5_NSTATE), _BF),
                        pltpu.VMEM((SB_S5 * BATCH, 2 * S5_NSTATE), _BF),
                        pltpu.VMEM((SB_S5 * BATCH, S5_WIDTH), _F32),
                        pltpu.VMEM((SB_S5 * BATCH, S5_WIDTH), _F32),
                        pltpu.VMEM((BATCH, 2 * S5_NSTATE), _F32)],
        compiler_params=arb1,
        name="s5_scan",
    )(u_tm, s5_init, bblk, cblk, ar, ai, _row(s5_d[0]), s5_w_glu[0].astype(_BF), _row(s5_b_glu[0]))

    n_f = SEQ // TM_FFN
    frow = lambda w: pl.BlockSpec((BPS_FFN, TM_FFN, w), lambda i, p: (p, i, 0))
    out = pl.pallas_call(
        _ffn_kernel,
        grid=(n_f, BATCH // BPS_FFN),
        in_specs=[frow(D_MODEL),
                  pl.BlockSpec((S5_SLABS, TM_FFN * BATCH, LANES), lambda i, p: (0, i, 0)),
                  frow(RET_WIDTH),
                  _const_spec((1, D_MODEL)), _const_spec((1, D_MODEL)),
                  _const_spec((D_MODEL, D_MODEL)),
                  _const_spec((1, D_MODEL)), _const_spec((1, D_MODEL)),
                  _const_spec((D_MODEL, D_FF)), _const_spec((D_FF, D_MODEL)),
                  _const_spec((1, D_MODEL)), _const_spec((1, D_MODEL))],
        out_specs=frow(D_MODEL),
        out_shape=jax.ShapeDtypeStruct((BATCH, SEQ, D_MODEL), _F32),
        compiler_params=arb2,
        name="out_ffn",
    )(x, ys5_tm, y_ret, lig, lib, w_out_b,
      _row(ln1_g[0]), _row(ln1_b[0]), w_up_b, w_dn_b, _row(ln2_g[0]), _row(ln2_b[0]))

    return out
```

```python
import functools

import jax
import jax.numpy as jnp
import numpy as np
from jax import lax
from jax.experimental import pallas as pl
from jax.experimental.pallas import tpu as pltpu

D_MODEL = 1024
BATCH = 8
SEQ = 4096
N_META = 16
S5_GROUP_CH = 16
S5_STATE = 64
S5_WIDTH = 256
S5_GROUPS = S5_WIDTH // S5_GROUP_CH
S5_NSTATE = S5_GROUPS * S5_STATE
RET_HEAD_DIM = 128
RET_WIDTH = 768
RET_HEADS = RET_WIDTH // RET_HEAD_DIM
CHUNK = 128
ROPE_BASE = 10000.0
D_FF = 4 * D_MODEL
LANES = 128
S5_SLABS = S5_WIDTH // LANES
LN_EPS = 1e-5
GN_EPS = 1e-5
IN_PROJ_WIDTH = S5_WIDTH + 4 * RET_WIDTH
DEEPNORM_ALPHA = 2.0 ** 0.25

_OFF_Q = S5_WIDTH
_OFF_K = _OFF_Q + RET_WIDTH
_OFF_V = _OFF_K + RET_WIDTH
_OFF_G = _OFF_V + RET_WIDTH

TM_PROJ = 512
BPS_PROJ = 2
SUB_PROJ = 256
MXU_TILE = 256
TT_S5 = 1024
SB_S5 = 32
TM_FFN = 512
BPS_FFN = 2
SUB_FFN = 256
FF_CHUNK = 1024
VMEM_LIMIT = 56 * 1024 * 1024

_BF = jnp.bfloat16
_F32 = jnp.float32


def _const_spec(shape):
    nd = len(shape)
    return pl.BlockSpec(shape, lambda *_: (0,) * nd, pipeline_mode=pl.Buffered(1))


def _layer_norm(x, g, b):
    mu = jnp.mean(x, axis=-1, keepdims=True)
    xc = x - mu
    var = jnp.mean(xc * xc, axis=-1, keepdims=True)
    return xc * lax.rsqrt(var + LN_EPS) * g + b


def _dot(a, b):
    return jnp.dot(a, b, preferred_element_type=_F32)


def _zero_after(values):
    if not values:
        return 0.0
    tok = values[0][0:1, 0:1]
    for v in values[1:]:
        tok = tok + v[0:1, 0:1]
    bits = lax.shift_right_logical(lax.shift_right_logical(tok.astype(jnp.int32), 16), 16)
    return bits.astype(_F32)


def _rope_head(t, cos2, sin2):
    return t * cos2 + pltpu.roll(t, RET_HEAD_DIM // 2, 1) * sin2


def _meta_kernel(meta_ref, g_ref, b_ref, w32_ref, bblk_ref, ar_ref, ai_ref, cos_ref, sin_ref,
                 zmeta_ref, s5_ref, s0_ref, w_ref):
    slab = D_MODEL // 8
    for r in range(0, D_MODEL, slab):
        w_ref[r:r + slab, :] = w32_ref[r:r + slab, :].astype(_BF)
    hm = _layer_norm(meta_ref[...], g_ref[...], b_ref[...]).astype(_BF)
    u = _dot(hm, w_ref[:, 0:S5_WIDTH])
    bu = _dot(u.astype(_BF), bblk_ref[...])
    ar = ar_ref[...]
    ai = ai_ref[...]
    xr = jnp.zeros((1, S5_NSTATE), _F32)
    xi = jnp.zeros((1, S5_NSTATE), _F32)
    for t in range(N_META):
        br = bu[t:t + 1, 0:S5_NSTATE]
        bi = bu[t:t + 1, S5_NSTATE:2 * S5_NSTATE]
        xr, xi = ar * xr - ai * xi + br, ar * xi + ai * xr + bi
    s5_ref[:, 0:S5_NSTATE] = xr
    s5_ref[:, S5_NSTATE:2 * S5_NSTATE] = xi

    k = _dot(hm, w_ref[:, _OFF_K:_OFF_V])
    v = _dot(hm, w_ref[:, _OFF_V:_OFF_G]).astype(_BF)
    cos2 = cos_ref[...]
    sin2 = sin_ref[...]
    for h in range(RET_HEADS):
        sl = slice(h * RET_HEAD_DIM, (h + 1) * RET_HEAD_DIM)
        kh = _rope_head(k[:, sl], cos2, sin2) * (RET_HEAD_DIM ** -0.5)
        kz = (kh * zmeta_ref[h]).astype(_BF)
        s0_ref[h] = lax.dot_general(kz, v[:, sl], (((0,), (0,)), ((), ())),
                                    preferred_element_type=_F32)


def _in_proj_ret_kernel(x_ref, g_ref, b_ref, w_ref, cos_ref, sin_ref, s0_ref, dmat_ref, xi_ref,
                        zeta_ref, gc_ref, gng_ref, gnb_ref, wo32_ref, wup32_ref, wdn32_ref,
                        u_ref, y_ref, wo16_ref, wup16_ref, wdn16_ref,
                        q_scr, k_scr, v_scr, sg_scr, sloc_scr, s_scr):
    pair = pl.program_id(1)
    halves = TM_PROJ // SUB_PROJ
    heads = [(h, slice(h * RET_HEAD_DIM, (h + 1) * RET_HEAD_DIM)) for h in range(RET_HEADS)]

    wo16_ref[...] = wo32_ref[...].astype(_BF)
    wup16_ref[...] = wup32_ref[...].astype(_BF)
    wdn16_ref[...] = wdn32_ref[...].astype(_BF)

    @pl.when(pl.program_id(0) == 0)
    def _():
        for bb in range(BPS_PROJ):
            sloc_scr[bb] = s0_ref[...]

    @pl.when(pl.program_id(0) != 0)
    def _():
        for bb in range(BPS_PROJ):
            sloc_scr[bb] = s_scr[pair * BPS_PROJ + bb]

    def where(s):
        return s // halves, (s % halves) * SUB_PROJ

    def normed(s):
        bb, r0 = where(s)
        return _layer_norm(x_ref[bb, r0:r0 + SUB_PROJ, :], g_ref[...], b_ref[...]).astype(_BF)

    def project_stages(s, hn, box):
        bb, r0 = where(s)
        rows = slice(r0, r0 + SUB_PROJ)

        def st_u():
            u = _dot(hn, w_ref[:, 0:S5_WIDTH])
            start = pair * BPS_PROJ + bb + r0 * BATCH
            for j in range(S5_SLABS):
                u_ref[j, pl.ds(start, SUB_PROJ, stride=BATCH), :] = u[:, j * LANES:(j + 1) * LANES]

        def st_q():
            box['q'] = _dot(hn, w_ref[:, _OFF_Q:_OFF_K])

        def st_q_rope():
            if s + 1 < BPS_PROJ * halves:
                box['next'] = normed(s + 1)
            for h, sl in heads:
                q_scr[bb, rows, sl] = _rope_head(box['q'][:, sl], cos_ref[rows, :],
                                                 sin_ref[rows, :]).astype(_BF)

        def st_k():
            k = _dot(hn, w_ref[:, _OFF_K:_OFF_V])
            for h, sl in heads:
                k_scr[bb, rows, sl] = (_rope_head(k[:, sl], cos_ref[rows, :], sin_ref[rows, :])
                                       * (RET_HEAD_DIM ** -0.5)).astype(_BF)

        def st_g():
            g = _dot(hn, w_ref[:, _OFF_G:IN_PROJ_WIDTH])
            sg_scr[bb, rows, :] = (g * jax.nn.sigmoid(g)).astype(_BF)

        def st_v():
            v_scr[bb, rows, :] = _dot(hn, w_ref[:, _OFF_V:_OFF_G]).astype(_BF)

        return [st_u, st_q, st_q_rope, st_k, st_g, st_v]

    def retention_stages(s):
        bb, r0 = where(s)
        stages = []
        for c in range(SUB_PROJ // CHUNK):
            rows = slice(r0 + c * CHUNK, r0 + (c + 1) * CHUNK)
            box = {}

            def wave_dots(rows=rows, box=box):
                scores, cross = [], []
                for h, cols in heads:
                    qh = q_scr[bb, rows, cols]
                    kh = k_scr[bb, rows, cols]
                    state = sloc_scr[bb, h]
                    scores.append(lax.dot_general(qh, kh, (((1,), (1,)), ((), ())),
                                                  preferred_element_type=_F32))
                    cross.append(_dot(qh, state.astype(_BF)))
                    kz = (kh.astype(_F32) * zeta_ref[h]).astype(_BF)
                    sloc_scr[bb, h] = gc_ref[h] * state + lax.dot_general(
                        kz, v_scr[bb, rows, cols], (((0,), (0,)), ((), ())),
                        preferred_element_type=_F32)
                box['scores'], box['cross'] = scores, cross

            def wave_mix(rows=rows, box=box):
                box['outs'] = [
                    _dot((box['scores'][h] * dmat_ref[h]).astype(_BF), v_scr[bb, rows, cols])
                    + box['cross'][h] * xi_ref[h] for h, cols in heads]

            def wave_norm(rows=rows, box=box):
                for h, cols in heads:
                    o = box['outs'][h]
                    mu = jnp.mean(o, axis=-1, keepdims=True)
                    oc = o - mu
                    var = jnp.mean(oc * oc, axis=-1, keepdims=True)
                    on = oc * lax.rsqrt(var + GN_EPS) * gng_ref[:, cols] + gnb_ref[:, cols]
                    y_ref[bb, rows, cols] = (sg_scr[bb, rows, cols].astype(_F32)
                                             * on).astype(_BF)

            stages += [wave_dots, wave_mix, wave_norm]
        return stages

    n_sub = BPS_PROJ * halves
    hn = normed(0)
    for s in range(n_sub):
        box = {}
        proj = project_stages(s, hn, box)
        ret = retention_stages(s - 1) if s >= 1 else []
        for i in range(max(len(proj), len(ret))):
            if i < len(proj):
                proj[i]()
            if i < len(ret):
                ret[i]()
        hn = box.get('next')
    for wave in retention_stages(n_sub - 1):
        wave()

    for bb in range(BPS_PROJ):
        s_scr[pair * BPS_PROJ + bb] = sloc_scr[bb]


def _s5_kernel(u_ref, init_ref, bblk_ref, cblk_ref, ar_ref, ai_ref, d_ref, wglu_ref, bglu_ref,
               y_ref, bu0_scr, bu1_scr, xb0_scr, xb1_scr, acc0_scr, acc1_scr, st_scr):
    @pl.when(pl.program_id(0) == 0)
    def _():
        st_scr[...] = jnp.broadcast_to(init_ref[...], (BATCH, 2 * S5_NSTATE))

    rows_sb = SB_S5 * BATCH
    n_sb = TT_S5 // SB_S5
    n_piece = 2 * S5_NSTATE // MXU_TILE
    steps_piece = SB_S5 // n_piece
    re = slice(0, S5_NSTATE)
    im = slice(S5_NSTATE, 2 * S5_NSTATE)
    ar = jnp.broadcast_to(ar_ref[...], (BATCH, S5_NSTATE))
    ai = jnp.broadcast_to(ai_ref[...], (BATCH, S5_NSTATE))

    def rows_of(j):
        if isinstance(j, int):
            return pl.ds(j * rows_sb, rows_sb)
        return pl.ds(pl.multiple_of(j * rows_sb, rows_sb), rows_sb)

    def load_u(j):
        return jnp.concatenate([u_ref[s, rows_of(j), :] for s in range(S5_SLABS)], axis=1)

    bu_scr = (bu0_scr, bu1_scr)
    xb_scr = (xb0_scr, xb1_scr)
    acc_scr = (acc0_scr, acc1_scr)

    def a_piece(ub, slot, c):
        cols = slice(c * MXU_TILE, (c + 1) * MXU_TILE)
        bu_scr[slot][:, cols] = _dot(ub, bblk_ref[:, cols])

    def b_piece(slot, c, xr, xi):
        for t in range(c * steps_piece, (c + 1) * steps_piece, 2):
            out_r, out_i = [], []
            for tt in (t, t + 1):
                rows = slice(tt * BATCH, (tt + 1) * BATCH)
                br = bu_scr[slot][rows, re]
                bi = bu_scr[slot][rows, im]
                xr, xi = ar * xr - ai * xi + br, ar * xi + ai * xr + bi
                out_r.append(xr)
                out_i.append(xi)
            rows2 = slice(t * BATCH, (t + 2) * BATCH)
            xb_scr[slot][rows2, re] = jnp.concatenate(out_r, axis=0).astype(_BF)
            xb_scr[slot][rows2, im] = jnp.concatenate(out_i, axis=0).astype(_BF)
        return xr, xi

    def c_piece(slot, c, acc):
        cols = slice(c * MXU_TILE, (c + 1) * MXU_TILE)
        part = _dot(xb_scr[slot][:, cols], cblk_ref[cols, :])
        return part if acc is None else acc + part

    def d_piece(j, slot):
        rows = rows_of(j)
        u = jnp.concatenate([u_ref[s, rows, :] for s in range(S5_SLABS)], axis=1)
        y = jax.nn.gelu(acc_scr[slot][...] + d_ref[...] * u)
        gate = jax.nn.sigmoid(_dot(y.astype(_BF), wglu_ref[...]) + bglu_ref[...])
        y = y * gate
        for s in range(S5_SLABS):
            y_ref[s, rows, :] = y[:, s * LANES:(s + 1) * LANES]

    def iteration(j, slot, xr, xi, do_a, do_b, do_c, do_d):
        other = 1 - slot
        ub = load_u(j + 1).astype(_BF) if do_a else None
        acc = None
        for c in range(n_piece):
            if do_a:
                a_piece(ub, other, c)
            if do_b:
                xr, xi = b_piece(slot, c, xr, xi)
            if do_c:
                acc = c_piece(other, c, acc)
            if do_d and c == n_piece - 1:
                d_piece(j - 2, slot)
        if do_c:
            acc_scr[other][...] = acc
        return xr, xi

    ub0 = load_u(0).astype(_BF)
    for c in range(n_piece):
        a_piece(ub0, 0, c)
    xr, xi = iteration(0, 0, st_scr[:, re], st_scr[:, im], True, True, False, False)
    xr, xi = iteration(1, 1, xr, xi, True, True, True, False)

    def pair(p, carry):
        j = 2 * p + 2
        xr, xi = iteration(j, 0, carry[0], carry[1], True, True, True, True)
        return iteration(j + 1, 1, xr, xi, True, True, True, True)

    xr, xi = lax.fori_loop(0, (n_sb - 4) // 2, pair, (xr, xi))
    xr, xi = iteration(n_sb - 2, 0, xr, xi, True, True, True, True)
    xr, xi = iteration(n_sb - 1, 1, xr, xi, False, True, True, True)
    st_scr[:, re] = xr
    st_scr[:, im] = xi
    iteration(n_sb, 0, xr, xi, False, False, True, True)
    iteration(n_sb + 1, 1, xr, xi, False, False, False, True)


def _ffn_kernel(x_ref, ys5_ref, yret_ref, lig_ref, lib_ref, wo_ref, l1g_ref, l1b_ref,
                wup_ref, wdn_ref, l2g_ref, l2b_ref, o_ref):
    pair = pl.program_id(1)
    n_ch = D_FF // FF_CHUNK
    piece = SUB_FFN // (2 * n_ch)
    halves = TM_FFN // SUB_FFN
    n_sub = BPS_FFN * halves

    def where(t):
        return t // halves, (t % halves) * SUB_FFN

    def mixed_of(t):
        bb, r0 = where(t)
        start = pair * BPS_FFN + bb + r0 * BATCH
        ys5 = jnp.concatenate(
            [ys5_ref[j, pl.ds(start, SUB_FFN, stride=BATCH), :] for j in range(S5_SLABS)],
            axis=1).astype(_BF)
        return (_dot(ys5, wo_ref[0:S5_WIDTH, :])
                + _dot(yret_ref[bb, r0:r0 + SUB_FFN, :], wo_ref[S5_WIDTH:D_MODEL, :]))

    def prep_rows(t, mixed, lo, hi):
        bb, r0 = where(t)
        h = _layer_norm(x_ref[bb, r0 + lo:r0 + hi, :], lig_ref[...], lib_ref[...])
        return _layer_norm(DEEPNORM_ALPHA * h + mixed[lo:hi], l1g_ref[...], l1b_ref[...])

    def ffn_up(h1b, c, floor):
        up = jnp.maximum(_dot(h1b, wup_ref[:, c * FF_CHUNK:(c + 1) * FF_CHUNK]), floor)
        return (up * up).astype(_BF)

    def ffn_down(act, c):
        return _dot(act, wdn_ref[c * FF_CHUNK:(c + 1) * FF_CHUNK, :])

    def finish_rows(t, pre, lo, hi):
        bb, r0 = where(t)
        out = _layer_norm(pre[lo:hi], l2g_ref[...], l2b_ref[...])
        o_ref[bb, r0 + lo:r0 + hi, :] = out
        return out

    h1 = prep_rows(0, mixed_of(0), 0, SUB_FFN)
    pre_prev = None
    floor = 0.0
    for t in range(n_sub):
        h1b = h1.astype(_BF)
        pre = DEEPNORM_ALPHA * h1
        mixed_next = mixed_of(t + 1) if t + 1 < n_sub else None
        next_parts = []

        def side_work(k):
            lo, hi = k * piece, (k + 1) * piece
            done = []
            if mixed_next is not None:
                next_parts.append(prep_rows(t + 1, mixed_next, lo, hi))
                done.append(next_parts[-1])
            if pre_prev is not None:
                done.append(finish_rows(t - 1, pre_prev, lo, hi))
            return done

        for c in range(n_ch):
            act = ffn_up(h1b, c, floor)
            done = side_work(2 * c)
            pre = pre + ffn_down(act, c)
            done += side_work(2 * c + 1)
            floor = _zero_after(done)
        if next_parts:
            h1 = jnp.concatenate(next_parts, axis=0)
        pre_prev = pre
    finish_rows(n_sub - 1, pre_prev, 0, SUB_FFN)


def _row(v):
    return v.reshape(1, -1).astype(_F32)


@functools.lru_cache(maxsize=None)
def _position_tables():
    f32 = np.float32

    def fn(f, a):
        return f(a.astype(np.float64)).astype(f32)

    pos = np.arange(N_META + SEQ, dtype=f32)
    expo = np.arange(0, RET_HEAD_DIM, 2, dtype=f32) / f32(RET_HEAD_DIM)
    inv_freq = (f32(1.0) / fn(lambda e: np.power(ROPE_BASE, e), expo)).astype(f32)
    ang = pos[:, None] * inv_freq[None, :]
    cos, sin = fn(np.cos, ang), fn(np.sin, ang)
    cos2 = np.concatenate([cos, cos], axis=1)
    sin2 = np.concatenate([-sin, sin], axis=1)

    heads = np.arange(RET_HEADS, dtype=f32)
    log_gamma = fn(np.log1p, -fn(np.exp2, f32(-5.0) - heads))
    idx = np.arange(CHUNK, dtype=f32)
    diff = idx[:, None] - idx[None, :]
    dmat = np.where(diff[None] >= 0,
                    fn(np.exp, np.maximum(diff, f32(0.0))[None] * log_gamma[:, None, None]),
                    f32(0.0)).astype(f32)
    zeta = fn(np.exp, (f32(CHUNK - 1.0) - idx)[None] * log_gamma[:, None])
    xi = fn(np.exp, (idx + f32(1.0))[None] * log_gamma[:, None])
    gamma_chunk = fn(np.exp, f32(CHUNK) * log_gamma)
    hd = RET_HEAD_DIM
    zeta_b = np.ascontiguousarray(np.broadcast_to(zeta[:, :, None], (RET_HEADS, CHUNK, hd)))
    xi_b = np.ascontiguousarray(np.broadcast_to(xi[:, :, None], (RET_HEADS, CHUNK, hd)))
    gc_b = np.ascontiguousarray(np.broadcast_to(gamma_chunk[:, None, None], (RET_HEADS, 1, hd)))
    return cos2, sin2, dmat, zeta_b, xi_b, gc_b


def kernel(x, meta_tokens, ln_in_g, ln_in_b, w_in, s5_lambda_re, s5_lambda_im, s5_log_dt, s5_b_re, s5_b_im, s5_c_re, s5_c_im, s5_d, s5_w_glu, s5_b_glu, ret_gn_g, ret_gn_b, w_out, ln1_g, ln1_b, w_up, w_down, ln2_g, ln2_b):
    assert x.shape == (BATCH, SEQ, D_MODEL) and w_in.shape[0] == 1
    G, P, H = S5_GROUPS, S5_STATE, S5_GROUP_CH
    arb2 = pltpu.CompilerParams(dimension_semantics=("arbitrary", "arbitrary"),
                                vmem_limit_bytes=VMEM_LIMIT)
    arb1 = pltpu.CompilerParams(dimension_semantics=("arbitrary",), vmem_limit_bytes=VMEM_LIMIT)

    lam_re, lam_im = s5_lambda_re[0], s5_lambda_im[0]
    dt = jnp.exp(s5_log_dt[0])[:, None]
    mag = jnp.exp(lam_re * dt)
    lbr = mag * jnp.cos(lam_im * dt)
    lbi = mag * jnp.sin(lam_im * dt)
    den = lam_re * lam_re + lam_im * lam_im
    nr = lbr - 1.0
    qr = (nr * lam_re + lbi * lam_im) / den
    qi = (lbi * lam_re - nr * lam_im) / den
    bbr = qr[..., None] * s5_b_re[0] - qi[..., None] * s5_b_im[0]
    bbi = qr[..., None] * s5_b_im[0] + qi[..., None] * s5_b_re[0]
    eye = jnp.eye(G, dtype=_F32)

    def blk_in(m):
        return (eye[:, None, :, None] * m.transpose(0, 2, 1)[:, :, None, :]).reshape(G * H, G * P)

    def blk_out(m):
        return (eye[:, None, :, None] * m.transpose(0, 2, 1)[:, :, None, :]).reshape(G * P, G * H)

    bblk = jnp.concatenate([blk_in(bbr), blk_in(bbi)], axis=1).astype(_BF)
    cblk = jnp.concatenate([blk_out(s5_c_re[0]), -blk_out(s5_c_im[0])], axis=0).astype(_BF)
    ar = lbr.reshape(1, S5_NSTATE)
    ai = lbi.reshape(1, S5_NSTATE)

    hd = RET_HEAD_DIM
    cos2, sin2, dmat, zeta_b, xi_b, gc_b = _position_tables()
    zmeta_b = zeta_b[:, CHUNK - N_META:, :]

    lig, lib = _row(ln_in_g), _row(ln_in_b)

    s5_init, s0, w_in_b = pl.pallas_call(
        _meta_kernel,
        out_shape=(jax.ShapeDtypeStruct((1, 2 * S5_NSTATE), _F32),
                   jax.ShapeDtypeStruct((RET_HEADS, hd, hd), _F32),
                   jax.ShapeDtypeStruct((D_MODEL, IN_PROJ_WIDTH), _BF)),
        compiler_params=pltpu.CompilerParams(vmem_limit_bytes=VMEM_LIMIT),
        name="meta_prologue",
    )(meta_tokens.astype(_F32), lig, lib, w_in[0], bblk, ar, ai,
      cos2[:N_META], sin2[:N_META], zmeta_b)

    n_t = SEQ // TM_PROJ
    rows_spec = lambda w: pl.BlockSpec((BPS_PROJ, TM_PROJ, w), lambda i, p: (p, i, 0))
    n_pairs = BATCH // BPS_PROJ
    n_steps = n_t * n_pairs

    def slab_spec(shape):
        return pl.BlockSpec((shape[0] // n_steps, shape[1]), lambda i, p: (i * n_pairs + p, 0))

    tab = lambda n: _const_spec((RET_HEADS, n, hd))
    w_shapes = [(D_MODEL, D_MODEL), (D_MODEL, D_FF), (D_FF, D_MODEL)]
    qkv_scratch = pltpu.VMEM((BPS_PROJ, TM_PROJ, RET_WIDTH), _BF)
    u_tm, y_ret, w_out_b, w_up_b, w_dn_b = pl.pallas_call(
        _in_proj_ret_kernel,
        grid=(n_t, n_pairs),
        in_specs=[rows_spec(D_MODEL), _const_spec((1, D_MODEL)), _const_spec((1, D_MODEL)),
                  _const_spec((D_MODEL, IN_PROJ_WIDTH)),
                  pl.BlockSpec((TM_PROJ, hd), lambda i, p: (i, 0)),
                  pl.BlockSpec((TM_PROJ, hd), lambda i, p: (i, 0)),
                  tab(hd), tab(CHUNK), tab(CHUNK), tab(CHUNK), tab(1),
                  _const_spec((1, RET_WIDTH)), _const_spec((1, RET_WIDTH))]
                 + [slab_spec(s) for s in w_shapes],
        out_specs=[pl.BlockSpec((S5_SLABS, TM_PROJ * BATCH, LANES), lambda i, p: (0, i, 0)),
                   rows_spec(RET_WIDTH)] + [slab_spec(s) for s in w_shapes],
        out_shape=(jax.ShapeDtypeStruct((S5_SLABS, SEQ * BATCH, LANES), _F32),
                   jax.ShapeDtypeStruct((BATCH, SEQ, RET_WIDTH), _BF))
                  + tuple(jax.ShapeDtypeStruct(s, _BF) for s in w_shapes),
        scratch_shapes=[qkv_scratch, qkv_scratch, qkv_scratch, qkv_scratch,
                        pltpu.VMEM((BPS_PROJ, RET_HEADS, hd, hd), _F32),
                        pltpu.VMEM((BATCH, RET_HEADS, hd, hd), _F32)],
        compiler_params=arb2,
        name="in_proj_ret",
    )(x, lig, lib, w_in_b, cos2[N_META:], sin2[N_META:], s0, dmat, xi_b, zeta_b, gc_b,
      _row(ret_gn_g[0]), _row(ret_gn_b[0]), w_out[0], w_up[0], w_down[0])

    rows_s5 = TT_S5 * BATCH
    ys5_tm = pl.pallas_call(
        _s5_kernel,
        grid=(SEQ // TT_S5,),
        in_specs=[pl.BlockSpec((S5_SLABS, rows_s5, LANES), lambda i: (0, i, 0)),
                  _const_spec((1, 2 * S5_NSTATE)),
                  _const_spec((S5_WIDTH, 2 * S5_NSTATE)), _const_spec((2 * S5_NSTATE, S5_WIDTH)),
                  _const_spec((1, S5_NSTATE)), _const_spec((1, S5_NSTATE)),
                  _const_spec((1, S5_WIDTH)), _const_spec((S5_WIDTH, S5_WIDTH)),
                  _const_spec((1, S5_WIDTH))],
        out_specs=pl.BlockSpec((S5_SLABS, rows_s5, LANES), lambda i: (0, i, 0)),
        out_shape=jax.ShapeDtypeStruct((S5_SLABS, SEQ * BATCH, LANES), _F32),
        scratch_shapes=[pltpu.VMEM((SB_S5 * BATCH, 2 * S5_NSTATE), _F32),
                        pltpu.VMEM((SB_S5 * BATCH, 2 * S5_NSTATE), _F32),
                        pltpu.VMEM((SB_S5 * BATCH, 2 * S5_NSTATE), _BF),
                        pltpu.VMEM((SB_S5 * BATCH, 2 * S5_NSTATE), _BF),
                        pltpu.VMEM((SB_S5 * BATCH, S5_WIDTH), _F32),
                        pltpu.VMEM((SB_S5 * BATCH, S5_WIDTH), _F32),
                        pltpu.VMEM((BATCH, 2 * S5_NSTATE), _F32)],
        compiler_params=arb1,
        name="s5_scan",
    )(u_tm, s5_init, bblk, cblk, ar, ai, _row(s5_d[0]), s5_w_glu[0].astype(_BF), _row(s5_b_glu[0]))

    n_f = SEQ // TM_FFN
    frow = lambda w: pl.BlockSpec((BPS_FFN, TM_FFN, w), lambda i, p: (p, i, 0))
    out = pl.pallas_call(
        _ffn_kernel,
        grid=(n_f, BATCH // BPS_FFN),
        in_specs=[frow(D_MODEL),
                  pl.BlockSpec((S5_SLABS, TM_FFN * BATCH, LANES), lambda i, p: (0, i, 0)),
                  frow(RET_WIDTH),
                  _const_spec((1, D_MODEL)), _const_spec((1, D_MODEL)),
                  _const_spec((D_MODEL, D_MODEL)),
                  _const_spec((1, D_MODEL)), _const_spec((1, D_MODEL)),
                  _const_spec((D_MODEL, D_FF)), _const_spec((D_FF, D_MODEL)),
                  _const_spec((1, D_MODEL)), _const_spec((1, D_MODEL))],
        out_specs=frow(D_MODEL),
        out_shape=jax.ShapeDtypeStruct((BATCH, SEQ, D_MODEL), _F32),
        compiler_params=arb2,
        name="out_ffn",
    )(x, ys5_tm, y_ret, lig, lib, w_out_b,
      _row(ln1_g[0]), _row(ln1_b[0]), w_up_b, w_dn_b, _row(ln2_g[0]), _row(ln2_b[0]))

    return out
```
